```python
import math
import jax
import jax.numpy as jnp
from jax import lax
import numpy as np

D_MODEL = 1024
BATCH = 2
SEQ = 8192
DEPTH = 1

GRID_W = 64
D_MIX = 1024
EPS = 1e-6

HG_HEADS = 4
HG_DK = 128
HG_DV = 128
HG_KW = HG_HEADS * HG_DK
HG_VW = HG_HEADS * HG_DV
CHUNK = 64

ATT_HEADS = 8
ATT_KV_HEADS = 2
ATT_DH = 64
ATT_GROUP = ATT_HEADS // ATT_KV_HEADS
ATT_QW = ATT_HEADS * ATT_DH
ATT_KVW = ATT_KV_HEADS * ATT_DH
ROPE_THETA = 10000.0
Q_BLOCK = 128

D_IN = 2 * HG_KW + HG_KW + 2 * HG_VW + ATT_QW + 2 * ATT_KVW

D_FF = -(-8 * D_MODEL // (3 * 256)) * 256

kernel_name = "hybrid_hgrn2_axial_gqa_encoder"


def _rmsnorm(x, w):
    xf = x.astype(jnp.float32)
    y = xf * lax.rsqrt(jnp.mean(xf * xf, axis=-1, keepdims=True) + EPS)
    return (y * w.astype(jnp.float32)).astype(x.dtype)


def _gla_chunk_scan(q, k, v, log_f):
    B, L, H, DK = q.shape
    DV = v.shape[-1]
    n = L // CHUNK

    def to_chunks(a):
        return a.reshape(B, n, CHUNK, H, a.shape[-1]).transpose(1, 0, 3, 2, 4)

    qc, kc, vc, gc = to_chunks(q), to_chunks(k), to_chunks(v), to_chunks(log_f)
    b = jnp.cumsum(gc, axis=3)
    b_ref = b[:, :, :, CHUNK // 2 - 1:CHUNK // 2, :]
    b_last = b[:, :, :, -1:, :]
    q_in = qc * jnp.exp(b - b_ref)
    k_in = kc * jnp.exp(b_ref - b)
    scores = jnp.einsum('nbhtk,nbhsk->nbhts', q_in, k_in)
    causal_in_scan = jnp.tril(jnp.ones((CHUNK, CHUNK), dtype=bool))
    scores = jnp.where(causal_in_scan, scores, 0.0)
    o_intra = jnp.einsum('nbhts,nbhsv->nbhtv', scores, vc)
    contrib = jnp.einsum('nbhsk,nbhsv->nbhkv', kc * jnp.exp(b_last - b), vc).astype(jnp.float32)
    decay = jnp.exp(b_last[:, :, :, 0, :]).astype(jnp.float32)

    def step(S, inp):
        d, c = inp
        return d[..., None] * S + c, S

    S0 = jnp.zeros((B, H, DK, DV), jnp.float32)
    _, S_prev = lax.scan(step, S0, (decay, contrib))
    o_inter = jnp.einsum('nbhtk,nbhkv->nbhtv', qc * jnp.exp(b), S_prev)
    o = (o_intra + o_inter).transpose(1, 0, 3, 2, 4).reshape(B, L, H, DV)
    return o.astype(v.dtype)


def _hgrn2_group(u_q, u_ff, u_fb, u_i, u_g, lb_fwd, lb_bwd, norm_w):
    B, L, _ = u_q.shape
    q = jax.nn.silu(u_q).reshape(B, L, HG_HEADS, HG_DK)
    i = u_i.reshape(B, L, HG_HEADS, HG_DV)

    def gates(z, lb):
        zf = z.astype(jnp.float32)
        f = lb + (1.0 - lb) * jax.nn.sigmoid(zf)
        k = (1.0 - lb) * jax.nn.sigmoid(-zf)
        return (jnp.log(f).reshape(B, L, HG_HEADS, HG_DK),
                k.reshape(B, L, HG_HEADS, HG_DK).astype(z.dtype))

    logf_f, k_f = gates(u_ff, lb_fwd)
    logf_b, k_b = gates(u_fb, lb_bwd)
    o_fwd = _gla_chunk_scan(q, k_f, i, logf_f)
    flip = lambda a: jnp.flip(a, axis=1)
    o_bwd = flip(_gla_chunk_scan(flip(q), flip(k_b), flip(i), flip(logf_b)))
    o = _rmsnorm(o_fwd + o_bwd, norm_w)
    o = o * jax.nn.silu(u_g.reshape(B, L, HG_HEADS, HG_DV))
    return o.reshape(B, L, HG_VW)


def _axial_rope_tables(L):
    rows = L // GRID_W
    row = jnp.repeat(jnp.arange(rows), GRID_W).astype(jnp.float32)
    col = jnp.tile(jnp.arange(GRID_W), rows).astype(jnp.float32)
    axis_dim = ATT_DH // 2
    freqs = ROPE_THETA ** (-jnp.arange(0, axis_dim, 2, dtype=jnp.float32) / axis_dim)
    ang = jnp.concatenate([row[:, None] * freqs, col[:, None] * freqs], axis=-1)
    return jnp.cos(ang), jnp.sin(ang)


def _apply_rope(x, cos, sin):
    xf = x.astype(jnp.float32)
    x1, x2 = xf[..., 0::2], xf[..., 1::2]
    c, s = cos[None, :, None, :], sin[None, :, None, :]
    out = jnp.stack([x1 * c - x2 * s, x1 * s + x2 * c], axis=-1).reshape(x.shape)
    return out.astype(x.dtype)


def _block_attention(q, k, v):
    B, L, _, dh = q.shape
    nq = L // Q_BLOCK
    qb = q.reshape(B, nq, Q_BLOCK, ATT_KV_HEADS, ATT_GROUP, dh).transpose(1, 0, 3, 4, 2, 5)
    kt = k.transpose(0, 2, 1, 3)
    vt = v.transpose(0, 2, 1, 3)
    scale = dh ** -0.5

    def one_block(qblk):
        s = jnp.einsum('bhgqd,bhkd->bhgqk', qblk, kt).astype(jnp.float32) * scale
        p = jax.nn.softmax(s, axis=-1)
        return jnp.einsum('bhgqk,bhkd->bhgqd', p.astype(vt.dtype), vt)

    o = lax.map(one_block, qb)
    return o.transpose(1, 0, 4, 2, 3, 5).reshape(B, L, ATT_QW)


def _attention_group(u_q, u_k, u_v, q_norm_w, k_norm_w, out_norm_w):
    B, L, _ = u_q.shape
    q = _rmsnorm(u_q.reshape(B, L, ATT_HEADS, ATT_DH), q_norm_w)
    k = _rmsnorm(u_k.reshape(B, L, ATT_KV_HEADS, ATT_DH), k_norm_w)
    v = u_v.reshape(B, L, ATT_KV_HEADS, ATT_DH)
    cos, sin = _axial_rope_tables(L)
    q, k = _apply_rope(q, cos, sin), _apply_rope(k, cos, sin)
    o = _block_attention(q, k, v)
    return _rmsnorm(o, out_norm_w)


def setup_inputs(seed: int = 0) -> dict:
    key = jax.random.key(seed)
    ks = jax.random.split(key, 16)
    f32 = jnp.float32
    gain = lambda k, shape: 1.0 + 0.02 * jax.random.normal(k, shape, f32)
    return {
        "x": jax.random.normal(ks[0], (BATCH, SEQ, D_MODEL), f32),
        "norm1_w": gain(ks[1], (DEPTH, D_MODEL)),
        "w_in": jax.random.normal(ks[2], (DEPTH, D_MODEL, D_IN), f32) * D_MODEL ** -0.5,
        "lb_logits": 0.5 * jax.random.normal(ks[3], (2, DEPTH + 1, HG_KW), f32),
        "hg_norm_w": gain(ks[4], (DEPTH, HG_DV)),
        "q_norm_w": gain(ks[5], (DEPTH, ATT_DH)),
        "k_norm_w": gain(ks[6], (DEPTH, ATT_DH)),
        "att_norm_w": gain(ks[7], (DEPTH, ATT_QW)),
        "w_out": jax.random.normal(ks[8], (DEPTH, D_MIX, D_MODEL), f32) * D_MIX ** -0.5,
        "norm2_w": gain(ks[9], (DEPTH, D_MODEL)),
        "w_gate_up": jax.random.normal(ks[10], (DEPTH, D_MODEL, 2 * D_FF), f32) * D_MODEL ** -0.5,
        "w_down": jax.random.normal(ks[11], (DEPTH, D_FF, D_MODEL), f32) * D_FF ** -0.5,
        "final_norm_w": gain(ks[12], (D_MODEL,)),
    }


def reference(x, norm1_w, w_in, lb_logits, hg_norm_w, q_norm_w, k_norm_w, att_norm_w,
              w_out, norm2_w, w_gate_up, w_down, final_norm_w):
    lb_all = jnp.cumsum(jax.nn.softmax(lb_logits.astype(jnp.float32), axis=1), axis=1)
    splits = np.cumsum([HG_KW, HG_KW, HG_KW, HG_VW, HG_VW, ATT_QW, ATT_KVW]).tolist()
    for l in range(DEPTH):
        h = _rmsnorm(x, norm1_w[l])
        u = jnp.einsum('bld,de->ble', h, w_in[l])
        u_q, u_ff, u_fb, u_i, u_g, a_q, a_k, a_v = jnp.split(u, splits, axis=-1)
        o_hg = _hgrn2_group(u_q, u_ff, u_fb, u_i, u_g, lb_all[0, l], lb_all[1, l], hg_norm_w[l])
        o_att = _attention_group(a_q, a_k, a_v, q_norm_w[l], k_norm_w[l], att_norm_w[l])
        mix = jnp.concatenate([o_hg, o_att], axis=-1)
        x = x + jnp.einsum('ble,ed->bld', mix, w_out[l])
        h2 = _rmsnorm(x, norm2_w[l])
        gate, up = jnp.split(jnp.einsum('bld,df->blf', h2, w_gate_up[l]), 2, axis=-1)
        x = x + jnp.einsum('blf,fd->bld', jax.nn.silu(gate) * up, w_down[l])
    return _rmsnorm(x, final_norm_w)
```

```python
import functools
import math

import jax
import jax.numpy as jnp
from jax import lax
from jax.experimental import pallas as pl
from jax.experimental.pallas import tpu as pltpu

F32 = jnp.float32
BF16 = jnp.bfloat16

EPS = 1e-6
GRID_W = 64
ROPE_THETA = 10000.0

HG_HEADS = 4
HG_D = 128
HG_W = HG_HEADS * HG_D
CHUNK = 64

ATT_HEADS = 8
ATT_KV = 2
ATT_GROUP = ATT_HEADS // ATT_KV
ATT_DH = 64
ATT_QW = ATT_HEADS * ATT_DH
ATT_KVW = ATT_KV * ATT_DH

C_Q, C_FF, C_FB, C_I, C_G = 0, HG_W, 2 * HG_W, 3 * HG_W, 4 * HG_W
C_AQ = 5 * HG_W
C_AK = C_AQ + ATT_QW
C_AV = C_AK + ATT_KVW
D_IN = C_AV + ATT_KVW

LANES = 128
VMEM_LIMIT = 56 * 1024 * 1024

TM_IN = 256
TQ = 256
TK = 512
T_SCAN = 256
TM_OUT = 256
TM_FFN = 512
FF_CHUNK = 256


def _sigmoid(u):
    return 1.0 / (1.0 + jnp.exp(-u))


def _dot(a, b):
    return jnp.dot(a, b, preferred_element_type=F32)


def _dot_nt(a, b):
    return lax.dot_general(a, b, (((1,), (1,)), ((), ())), preferred_element_type=F32)


def _dot_tn(a, b):
    return lax.dot_general(a, b, (((0,), (0,)), ((), ())), preferred_element_type=F32)


def _split_dot(a, m):
    hi = a.astype(BF16)
    lo = (a - hi.astype(F32)).astype(BF16)
    return _dot(hi, m) + _dot(lo, m)


def _inproj_kernel(x_ref, n1w_ref, w_ref, lbf_ref, lbb_ref, wq_ref, wk_ref, cos_ref, sin_ref, mq_ref,
                   hq_ref, gf_ref, kf_ref, gb_ref, kb_ref, iv_ref, sg_ref, qt_ref, k_ref, vt_ref):
    tm = x_ref.shape[0]
    x = x_ref[...]
    ms = jnp.mean(x * x, axis=-1, keepdims=True)
    h = (x * lax.rsqrt(ms + EPS) * n1w_ref[...]).astype(BF16)

    def proj(lo, hi):
        return _dot(h, w_ref[:, lo:hi])

    u = proj(C_Q, C_FF)
    hq_ref[...] = (u * _sigmoid(u)).astype(BF16)

    def gates(z, lb, g_ref, k_out_ref):
        e = jnp.exp(-jnp.abs(z))
        r = 1.0 / (1.0 + e)
        er = e * r
        pos = z >= 0.0
        s = jnp.where(pos, r, er)
        sn = jnp.where(pos, er, r)
        g_ref[...] = jnp.log(lb + (1.0 - lb) * s)
        k_out_ref[...] = ((1.0 - lb) * sn).astype(BF16)

    gates(proj(C_FF, C_FB), lbf_ref[...], gf_ref, kf_ref)
    gates(proj(C_FB, C_I), lbb_ref[...], gb_ref, kb_ref)
    iv_ref[...] = proj(C_I, C_G).astype(BF16)
    u = proj(C_G, C_AQ)
    sg_ref[...] = (u * _sigmoid(u)).astype(BF16)

    cos = cos_ref[...]
    sin = sin_ref[...]
    lane = lax.broadcasted_iota(jnp.int32, (tm, LANES), 1)
    even = (lane & 1) == 0

    def rope(xc):
        partner = jnp.where(even, pltpu.roll(xc, LANES - 1, 1), pltpu.roll(xc, 1, 1))
        return xc * cos + partner * sin

    aq = proj(C_AQ, C_AK)
    yq = aq * lax.rsqrt(_split_dot(aq * aq, mq_ref[...]) + EPS) * wq_ref[...]
    for c in range(ATT_QW // LANES):
        qct = rope(yq[:, c * LANES:(c + 1) * LANES]).T.astype(BF16)
        grp, pair = divmod(c, 2)
        for e in range(2):
            hl = 2 * pair + e
            qt_ref[grp, :, hl * tm:(hl + 1) * tm] = qct[e * ATT_DH:(e + 1) * ATT_DH, :]

    ak = proj(C_AK, C_AV)
    yk = ak * lax.rsqrt(_split_dot(ak * ak, mq_ref[0:LANES, 0:LANES]) + EPS) * wk_ref[...]
    yk = rope(yk).astype(BF16)
    k_ref[0] = yk[:, 0:ATT_DH]
    k_ref[1] = yk[:, ATT_DH:2 * ATT_DH]

    avt = proj(C_AV, D_IN).T.astype(BF16)
    vt_ref[0, 0] = avt[0:ATT_DH, :]
    vt_ref[1, 0] = avt[ATT_DH:2 * ATT_DH, :]


def _inproj(x2d, n1w, w_bf, lbf, lbb, wq, wk, cos_t, sin_t, mq, seq_len):
    rows = x2d.shape[0]
    tm = TM_IN
    n_tiles = rows // tm
    tiles_per_seq = seq_len // tm
    per_tk = TK // tm
    row_blk = lambda i: (i, 0)
    const = lambda i: (0, 0)
    full = lambda shape: pl.BlockSpec(shape, const)
    act_bf = jax.ShapeDtypeStruct((rows, HG_W), BF16)
    act_f32 = jax.ShapeDtypeStruct((rows, HG_W), F32)
    act_spec = pl.BlockSpec((tm, HG_W), row_blk)
    return pl.pallas_call(
        _inproj_kernel,
        grid=(n_tiles,),
        in_specs=[
            pl.BlockSpec((tm, x2d.shape[1]), row_blk),
            full(n1w.shape), full(w_bf.shape), full(lbf.shape), full(lbb.shape),
            full(wq.shape), full(wk.shape),
            pl.BlockSpec((tm, LANES), lambda i: (i % tiles_per_seq, 0)),
            pl.BlockSpec((tm, LANES), lambda i: (i % tiles_per_seq, 0)),
            full(mq.shape),
        ],
        out_specs=[
            act_spec, act_spec, act_spec, act_spec, act_spec, act_spec, act_spec,
            pl.BlockSpec((ATT_KV, ATT_DH, ATT_GROUP * tm), lambda i: (0, 0, i)),
            pl.BlockSpec((ATT_KV, tm, ATT_DH), lambda i: (0, i, 0)),
            pl.BlockSpec((ATT_KV, 1, ATT_DH, tm), lambda i: (0, i // per_tk, 0, i % per_tk)),
        ],
        out_shape=[
            act_bf, act_f32, act_bf, act_f32, act_bf, act_bf, act_bf,
            jax.ShapeDtypeStruct((ATT_KV, ATT_DH, ATT_GROUP * rows), BF16),
            jax.ShapeDtypeStruct((ATT_KV, rows, ATT_DH), BF16),
            jax.ShapeDtypeStruct((ATT_KV, rows // TK, ATT_DH, TK), BF16),
        ],
        compiler_params=pltpu.CompilerParams(
            dimension_semantics=("arbitrary",), vmem_limit_bytes=VMEM_LIMIT),
        name="inproj",
    )(x2d, n1w, w_bf, lbf, lbb, wq, wk, cos_t, sin_t, mq)


def _hgrn_kernel(qf_ref, gf_ref, kf_ref, vf_ref, qb_ref, gb_ref, kb_ref, vb_ref, tri_ref,
                 of_ref, ob_ref, st_ref):
    @pl.when(pl.program_id(1) == 0)
    def _():
        st_ref[...] = jnp.zeros_like(st_ref)

    n_chunks = qf_ref.shape[0] // CHUNK
    row = lax.broadcasted_iota(jnp.int32, (CHUNK, CHUNK), 0)
    col = lax.broadcasted_iota(jnp.int32, (CHUNK, CHUNK), 1)
    plans = (
        (qf_ref, gf_ref, kf_ref, vf_ref, of_ref, row >= col, CHUNK // 2 - 1, CHUNK - 1, range(n_chunks)),
        (qb_ref, gb_ref, kb_ref, vb_ref, ob_ref, col >= row, CHUNK // 2, 0, range(n_chunks - 1, -1, -1)),
    )
    for d, (q_ref, g_ref, k_ref, v_ref, o_ref, keep, ref_row, end_row, order) in enumerate(plans):
        tri = tri_ref[d]
        for c in order:
            rs = slice(c * CHUNK, (c + 1) * CHUNK)
            for hh in range(HG_HEADS):
                cs = slice(hh * HG_D, (hh + 1) * HG_D)
                b = _split_dot_lhs(tri, g_ref[rs, cs])
                b_ref = b[ref_row:ref_row + 1, :]
                b_end = b[end_row:end_row + 1, :]
                q = q_ref[rs, cs].astype(F32)
                k = k_ref[rs, cs].astype(F32)
                v = v_ref[rs, cs]
                q_in = (q * jnp.exp(b - b_ref)).astype(BF16)
                k_in = (k * jnp.exp(b_ref - b)).astype(BF16)
                scores = jnp.where(keep, _dot_nt(q_in, k_in), 0.0).astype(BF16)
                st = st_ref[d * HG_HEADS + hh]
                q_dec = (q * jnp.exp(b)).astype(BF16)
                o_ref[rs, cs] = _dot(scores, v) + _dot_nt(q_dec, st.astype(BF16))
                k_dec = (k * jnp.exp(b_end - b)).astype(BF16)
                st_ref[d * HG_HEADS + hh] = st * jnp.exp(b_end) + _dot_tn(v, k_dec)


def _split_dot_lhs(m, a):
    hi = a.astype(BF16)
    lo = (a - hi.astype(F32)).astype(BF16)
    return _dot(m, hi) + _dot(m, lo)


def _hgrn(hq, gf, kf, gb, kb, iv, tri, batch, seq_len):
    rows = hq.shape[0]
    t = T_SCAN
    nb = seq_len // t
    fwd = lambda b, n: (b * nb + n, 0)
    bwd = lambda b, n: (b * nb + nb - 1 - n, 0)
    sf = pl.BlockSpec((t, HG_W), fwd)
    sb = pl.BlockSpec((t, HG_W), bwd)
    out = jax.ShapeDtypeStruct((rows, HG_W), F32)
    return pl.pallas_call(
        _hgrn_kernel,
        grid=(batch, nb),
        in_specs=[sf, sf, sf, sf, sb, sb, sb, sb, pl.BlockSpec(tri.shape, lambda b, n: (0, 0, 0))],
        out_specs=[sf, sb],
        out_shape=[out, out],
        scratch_shapes=[pltpu.VMEM((2 * HG_HEADS, HG_D, HG_D), F32)],
        compiler_params=pltpu.CompilerParams(
            dimension_semantics=("arbitrary", "arbitrary"), vmem_limit_bytes=VMEM_LIMIT),
        name="hgrn_scan",
    )(hq, gf, kf, iv, hq, gb, kb, iv, tri)


def _attn_kernel(q_ref, k_ref, v_ref, o_ref, m_ref, l_ref, acc_ref):
    tq = o_ref.shape[0]
    tk = v_ref.shape[3]
    n_kb = v_ref.shape[1]
    m_ref[...] = jnp.full(m_ref.shape, -1e30, F32)
    l_ref[...] = jnp.zeros_like(l_ref)
    acc_ref[...] = jnp.zeros_like(acc_ref)

    def kv_step(j, carry):
        kblk = k_ref[0, pl.ds(pl.multiple_of(j * tk, tk), tk), :]
        vblk = v_ref[0, j]
        for hh in range(ATT_GROUP):
            s = _dot(kblk, q_ref[0, :, hh * tq:(hh + 1) * tq])
            m_old = m_ref[hh]
            m_new = jnp.maximum(m_old, jnp.max(s, axis=0, keepdims=True))
            alpha = jnp.exp2(m_old - m_new)
            p = jnp.exp2(s - m_new)
            l_ref[hh] = alpha * l_ref[hh] + jnp.sum(p, axis=0, keepdims=True)
            acc_ref[hh] = alpha * acc_ref[hh] + _dot(vblk, p.astype(BF16))
            m_ref[hh] = m_new
        return carry

    lax.fori_loop(0, n_kb, kv_step, 0)
    for pair in range(ATT_GROUP // 2):
        ot = jnp.concatenate(
            [acc_ref[2 * pair + e] / l_ref[2 * pair + e] for e in range(2)], axis=0)
        o_ref[:, pair * LANES:(pair + 1) * LANES] = ot.T


def _attention(qt, k, vt, batch, seq_len):
    rows = k.shape[1]
    nq = seq_len // TQ
    n_kb = seq_len // TK
    return pl.pallas_call(
        _attn_kernel,
        grid=(batch, ATT_KV, nq),
        in_specs=[
            pl.BlockSpec((1, ATT_DH, ATT_GROUP * TQ), lambda b, g, i: (g, 0, b * nq + i)),
            pl.BlockSpec((1, seq_len, ATT_DH), lambda b, g, i: (g, b, 0)),
            pl.BlockSpec((1, n_kb, ATT_DH, TK), lambda b, g, i: (g, b, 0, 0)),
        ],
        out_specs=pl.BlockSpec((TQ, ATT_GROUP * ATT_DH), lambda b, g, i: (b * nq + i, g)),
        out_shape=jax.ShapeDtypeStruct((rows, ATT_QW), F32),
        scratch_shapes=[
            pltpu.VMEM((ATT_GROUP, 1, TQ), F32),
            pltpu.VMEM((ATT_GROUP, 1, TQ), F32),
            pltpu.VMEM((ATT_GROUP, ATT_DH, TQ), F32),
        ],
        compiler_params=pltpu.CompilerParams(
            dimension_semantics=("arbitrary", "arbitrary", "arbitrary"), vmem_limit_bytes=VMEM_LIMIT),
        name="attention",
    )(qt, k, vt)


def _rms(y, w):
    ms = jnp.mean(y * y, axis=-1, keepdims=True)
    return y * lax.rsqrt(ms + EPS) * w


def _outproj_kernel(of_ref, ob_ref, sg_ref, att_ref, x_ref, w_ref, hgw_ref, attw_ref, n2w_ref,
                    x1_ref, h2_ref):
    o = of_ref[...] + ob_ref[...]
    hgw = hgw_ref[...]
    o_hg = jnp.concatenate(
        [_rms(o[:, hh * HG_D:(hh + 1) * HG_D], hgw) for hh in range(HG_HEADS)], axis=1)
    o_hg = (o_hg * sg_ref[...].astype(F32)).astype(BF16)
    o_att = _rms(att_ref[...], attw_ref[...]).astype(BF16)
    y = x_ref[...] + _dot(o_hg, w_ref[0:HG_W, :]) + _dot(o_att, w_ref[HG_W:HG_W + ATT_QW, :])
    x1_ref[...] = y
    h2_ref[...] = _rms(y, n2w_ref[...]).astype(BF16)


def _outproj(o_f, o_b, sg, att, x2d, w_bf, hgw, attw, n2w):
    rows, d = x2d.shape
    tm = TM_OUT
    row_blk = lambda i: (i, 0)
    const = lambda i: (0, 0)
    half = pl.BlockSpec((tm, HG_W), row_blk)
    wide = pl.BlockSpec((tm, d), row_blk)
    return pl.pallas_call(
        _outproj_kernel,
        grid=(rows // tm,),
        in_specs=[half, half, half, half, wide,
                  pl.BlockSpec(w_bf.shape, const), pl.BlockSpec(hgw.shape, const),
                  pl.BlockSpec(attw.shape, const), pl.BlockSpec(n2w.shape, const)],
        out_specs=[wide, wide],
        out_shape=[jax.ShapeDtypeStruct((rows, d), F32), jax.ShapeDtypeStruct((rows, d), BF16)],
        compiler_params=pltpu.CompilerParams(
            dimension_semantics=("arbitrary",), vmem_limit_bytes=VMEM_LIMIT),
        name="outproj",
    )(o_f, o_b, sg, att, x2d, w_bf, hgw, attw, n2w)


def _ffn_kernel(h_ref, x1_ref, wgu_ref, wd_ref, fw_ref, o_ref, *, d_ff, final_norm):
    h = h_ref[...]
    acc = x1_ref[...]
    for c in range(d_ff // FF_CHUNK):
        lo = c * FF_CHUNK
        gate = _dot(h, wgu_ref[:, lo:lo + FF_CHUNK])
        up = _dot(h, wgu_ref[:, d_ff + lo:d_ff + lo + FF_CHUNK])
        act = (gate * _sigmoid(gate) * up).astype(BF16)
        acc = acc + _dot(act, wd_ref[lo:lo + FF_CHUNK, :])
    o_ref[...] = _rms(acc, fw_ref[...]) if final_norm else acc


def _ffn(h2, x1, wgu_bf, wd_bf, fw, final_norm):
    rows, d = x1.shape
    d_ff = wd_bf.shape[0]
    tm = TM_FFN
    row_blk = lambda i: (i, 0)
    const = lambda i: (0, 0)
    wide = pl.BlockSpec((tm, d), row_blk)
    resident = lambda shape: pl.BlockSpec(shape, const, pipeline_mode=pl.Buffered(1))
    return pl.pallas_call(
        functools.partial(_ffn_kernel, d_ff=d_ff, final_norm=final_norm),
        grid=(rows // tm,),
        in_specs=[wide, wide, resident(wgu_bf.shape), resident(wd_bf.shape), pl.BlockSpec(fw.shape, const)],
        out_specs=wide,
        out_shape=jax.ShapeDtypeStruct((rows, d), F32),
        compiler_params=pltpu.CompilerParams(
            dimension_semantics=("arbitrary",), vmem_limit_bytes=VMEM_LIMIT),
        name="ffn",
    )(h2, x1, wgu_bf, wd_bf, fw)


def _rope_tables(seq_len):
    rows = seq_len // GRID_W
    row = jnp.repeat(jnp.arange(rows), GRID_W).astype(F32)
    col = jnp.tile(jnp.arange(GRID_W), rows).astype(F32)
    axis_dim = ATT_DH // 2
    freqs = ROPE_THETA ** (-jnp.arange(0, axis_dim, 2, dtype=F32) / axis_dim)
    ang = jnp.concatenate([row[:, None] * freqs, col[:, None] * freqs], axis=-1)
    cos = jnp.repeat(jnp.cos(ang), 2, axis=1)
    sin = jnp.repeat(jnp.sin(ang), 2, axis=1) * jnp.tile(jnp.array([-1.0, 1.0], F32), ATT_DH // 2)
    reps = LANES // ATT_DH
    return jnp.tile(cos, (1, reps)), jnp.tile(sin, (1, reps))


def kernel(x, norm1_w, w_in, lb_logits, hg_norm_w, q_norm_w, k_norm_w, att_norm_w, w_out, norm2_w,
           w_gate_up, w_down, final_norm_w):
    batch, seq_len, d_model = x.shape
    depth = norm1_w.shape[0]
    rows = batch * seq_len
    assert seq_len % TK == 0 and seq_len % TQ == 0 and seq_len % T_SCAN == 0 and TK % TM_IN == 0
    assert TM_IN == TQ and rows % TM_FFN == 0 and w_in.shape[2] == D_IN

    lb_all = jnp.cumsum(jax.nn.softmax(lb_logits.astype(F32), axis=1), axis=1)
    cos_t, sin_t = _rope_tables(seq_len)
    blk = jnp.arange(ATT_QW) // ATT_DH
    mq = jnp.where(blk[:, None] == blk[None, :], 1.0 / ATT_DH, 0.0).astype(BF16)
    t_idx = jnp.arange(CHUNK)
    tri = jnp.stack([t_idx[:, None] >= t_idx[None, :], t_idx[None, :] >= t_idx[:, None]]).astype(BF16)
    q_scale = ATT_DH ** -0.5 * math.log2(math.e)

    x2d = x.reshape(rows, d_model)
    row = lambda v: v.astype(F32).reshape(1, -1)
    for l in range(depth):
        hq, gf, kf, gb, kb, iv, sg, qt, k, vt = _inproj(
            x2d, row(norm1_w[l]), w_in[l].astype(BF16), row(lb_all[0, l]), row(lb_all[1, l]),
            row(jnp.tile(q_norm_w[l], ATT_HEADS)) * q_scale, row(jnp.tile(k_norm_w[l], ATT_KV)),
            cos_t, sin_t, mq, seq_len)
        o_f, o_b = _hgrn(hq, gf, kf, gb, kb, iv, tri, batch, seq_len)
        att = _attention(qt, k, vt, batch, seq_len)
        x1, h2 = _outproj(o_f, o_b, sg, att, x2d, w_out[l].astype(BF16),
                          row(hg_norm_w[l]), row(att_norm_w[l]), row(norm2_w[l]))
        last = l == depth - 1
        x2d = _ffn(h2, x1, w_gate_up[l].astype(BF16), w_down[l].astype(BF16), row(final_norm_w), last)
    return x2d.reshape(batch, seq_len, d_model)
```

```python
import functools
import math

import jax
import jax.numpy as jnp
from jax import lax
from jax.experimental import pallas as pl
from jax.experimental.pallas import tpu as pltpu

F32 = jnp.float32
BF16 = jnp.bfloat16

EPS = 1e-6
GRID_W = 64
ROPE_THETA = 10000.0

HG_HEADS = 4
HG_D = 128
HG_W = HG_HEADS * HG_D
CHUNK = 64

ATT_HEADS = 8
ATT_KV = 2
ATT_GROUP = ATT_HEADS // ATT_KV
ATT_DH = 64
ATT_QW = ATT_HEADS * ATT_DH
ATT_KVW = ATT_KV * ATT_DH
V_ROWS = ATT_DH + 16

C_Q, C_FF, C_FB, C_I, C_G = 0, HG_W, 2 * HG_W, 3 * HG_W, 4 * HG_W
C_AQ = 5 * HG_W
C_AK = C_AQ + ATT_QW
C_AV = C_AK + ATT_KVW
D_IN = C_AV + ATT_KVW

LANES = 128
VMEM_LIMIT = 56 * 1024 * 1024

TM_IN = 256
TQ = 256
TK = 512
T_SCAN = 256
TM_OUT = 256
TM_FFN = 512
FF_CHUNK = 256


def _sigmoid(u):
    return 1.0 / (1.0 + jnp.exp(-u))


def _dot(a, b):
    return jnp.dot(a, b, preferred_element_type=F32)


def _dot_nt(a, b):
    return lax.dot_general(a, b, (((1,), (1,)), ((), ())), preferred_element_type=F32)


def _dot_tn(a, b):
    return lax.dot_general(a, b, (((0,), (0,)), ((), ())), preferred_element_type=F32)


def _split_dot(a, m):
    hi = a.astype(BF16)
    lo = (a - hi.astype(F32)).astype(BF16)
    return _dot(hi, m) + _dot(lo, m)


def _inproj_kernel(x_ref, n1w_ref, w_ref, lbf_ref, lbb_ref, wq_ref, wk_ref, cos_ref, sin_ref, mq_ref,
                   hq_ref, gf_ref, kf_ref, gb_ref, kb_ref, iv_ref, sg_ref, qt_ref, k_ref, vt_ref):
    tm = x_ref.shape[0]
    x = x_ref[...]
    ms = jnp.mean(x * x, axis=-1, keepdims=True)
    h = (x * lax.rsqrt(ms + EPS) * n1w_ref[...]).astype(BF16)

    def proj(lo, hi):
        return _dot(h, w_ref[:, lo:hi])

    u = proj(C_Q, C_FF)
    hq_ref[...] = (u * _sigmoid(u)).astype(BF16)

    def gates(z, lb, g_ref, k_out_ref):
        e = jnp.exp(-jnp.abs(z))
        r = 1.0 / (1.0 + e)
        er = e * r
        pos = z >= 0.0
        s = jnp.where(pos, r, er)
        sn = jnp.where(pos, er, r)
        g_ref[...] = jnp.log2(lb + (1.0 - lb) * s)
        k_out_ref[...] = ((1.0 - lb) * sn).astype(BF16)

    gates(proj(C_FF, C_FB), lbf_ref[...], gf_ref, kf_ref)
    gates(proj(C_FB, C_I), lbb_ref[...], gb_ref, kb_ref)
    iv_ref[...] = proj(C_I, C_G).astype(BF16)
    u = proj(C_G, C_AQ)
    sg_ref[...] = (u * _sigmoid(u)).astype(BF16)

    cos = cos_ref[...]
    sin = sin_ref[...]
    lane = lax.broadcasted_iota(jnp.int32, (tm, LANES), 1)
    even = (lane & 1) == 0

    def rope(xc):
        partner = jnp.where(even, pltpu.roll(xc, LANES - 1, 1), pltpu.roll(xc, 1, 1))
        return xc * cos + partner * sin

    aq = proj(C_AQ, C_AK)
    yq = aq * lax.rsqrt(_split_dot(aq * aq, mq_ref[...]) + EPS) * wq_ref[...]
    for c in range(ATT_QW // LANES):
        qct = rope(yq[:, c * LANES:(c + 1) * LANES]).T.astype(BF16)
        grp, pair = divmod(c, 2)
        for e in range(2):
            hl = 2 * pair + e
            qt_ref[grp, :, hl * tm:(hl + 1) * tm] = qct[e * ATT_DH:(e + 1) * ATT_DH, :]

    ak = proj(C_AK, C_AV)
    yk = ak * lax.rsqrt(_split_dot(ak * ak, mq_ref[0:LANES, 0:LANES]) + EPS) * wk_ref[...]
    yk = rope(yk).astype(BF16)
    k_ref[0] = yk[:, 0:ATT_DH]
    k_ref[1] = yk[:, ATT_DH:2 * ATT_DH]

    avt = proj(C_AV, D_IN).T.astype(BF16)
    ones = jnp.ones((V_ROWS - ATT_DH, tm), BF16)
    for g in range(ATT_KV):
        vt_ref[g, 0, 0:ATT_DH, :] = avt[g * ATT_DH:(g + 1) * ATT_DH, :]
        vt_ref[g, 0, ATT_DH:V_ROWS, :] = ones


def _inproj(x2d, n1w, w_bf, lbf, lbb, wq, wk, cos_t, sin_t, mq, seq_len):
    rows = x2d.shape[0]
    tm = TM_IN
    n_tiles = rows // tm
    tiles_per_seq = seq_len // tm
    per_tk = TK // tm
    row_blk = lambda i: (i, 0)
    const = lambda i: (0, 0)
    full = lambda shape: pl.BlockSpec(shape, const)
    act_bf = jax.ShapeDtypeStruct((rows, HG_W), BF16)
    act_f32 = jax.ShapeDtypeStruct((rows, HG_W), F32)
    act_spec = pl.BlockSpec((tm, HG_W), row_blk)
    return pl.pallas_call(
        _inproj_kernel,
        grid=(n_tiles,),
        in_specs=[
            pl.BlockSpec((tm, x2d.shape[1]), row_blk),
            full(n1w.shape), full(w_bf.shape), full(lbf.shape), full(lbb.shape),
            full(wq.shape), full(wk.shape),
            pl.BlockSpec((tm, LANES), lambda i: (i % tiles_per_seq, 0)),
            pl.BlockSpec((tm, LANES), lambda i: (i % tiles_per_seq, 0)),
            full(mq.shape),
        ],
        out_specs=[
            act_spec, act_spec, act_spec, act_spec, act_spec, act_spec, act_spec,
            pl.BlockSpec((ATT_KV, ATT_DH, ATT_GROUP * tm), lambda i: (0, 0, i)),
            pl.BlockSpec((ATT_KV, tm, ATT_DH), lambda i: (0, i, 0)),
            pl.BlockSpec((ATT_KV, 1, V_ROWS, tm), lambda i: (0, i // per_tk, 0, i % per_tk)),
        ],
        out_shape=[
            act_bf, act_f32, act_bf, act_f32, act_bf, act_bf, act_bf,
            jax.ShapeDtypeStruct((ATT_KV, ATT_DH, ATT_GROUP * rows), BF16),
            jax.ShapeDtypeStruct((ATT_KV, rows, ATT_DH), BF16),
            jax.ShapeDtypeStruct((ATT_KV, rows // TK, V_ROWS, TK), BF16),
        ],
        compiler_params=pltpu.CompilerParams(
            dimension_semantics=("arbitrary",), vmem_limit_bytes=VMEM_LIMIT),
        name="inproj",
    )(x2d, n1w, w_bf, lbf, lbb, wq, wk, cos_t, sin_t, mq)


def _hgrn_kernel(qf_ref, gf_ref, kf_ref, vf_ref, qb_ref, gb_ref, kb_ref, vb_ref, tri_ref,
                 of_ref, ob_ref, st_ref):
    @pl.when(pl.program_id(1) == 0)
    def _():
        st_ref[...] = jnp.zeros_like(st_ref)

    n_chunks = qf_ref.shape[0] // CHUNK
    row = lax.broadcasted_iota(jnp.int32, (CHUNK, CHUNK), 0)
    col = lax.broadcasted_iota(jnp.int32, (CHUNK, CHUNK), 1)
    plans = (
        (qf_ref, gf_ref, kf_ref, vf_ref, of_ref, row >= col, CHUNK // 2 - 1, CHUNK - 1, range(n_chunks)),
        (qb_ref, gb_ref, kb_ref, vb_ref, ob_ref, col >= row, CHUNK // 2, 0, range(n_chunks - 1, -1, -1)),
    )

    def sub(a, c, hh):
        return a[c * CHUNK:(c + 1) * CHUNK, hh * HG_D:(hh + 1) * HG_D]

    decayed = []
    for d, (q_ref, g_ref, k_ref, v_ref, o_ref, keep, ref_row, end_row, order) in enumerate(plans):
        b = _split_dot_lhs(tri_ref[d], g_ref[...])

        def chunk_rows(r, b=b):
            return jnp.concatenate(
                [jnp.broadcast_to(b[c * CHUNK + r:c * CHUNK + r + 1, :], (CHUNK, HG_W))
                 for c in range(n_chunks)], axis=0)

        b_ref = chunk_rows(ref_row)
        b_end = chunk_rows(end_row)
        q = q_ref[...].astype(F32)
        k = k_ref[...].astype(F32)
        decayed.append((
            b,
            (q * jnp.exp2(b - b_ref)).astype(BF16),
            (k * jnp.exp2(b_ref - b)).astype(BF16),
            (q * jnp.exp2(b)).astype(BF16),
            (k * jnp.exp2(b_end - b)).astype(BF16),
        ))
    local = []
    for d, (q_ref, g_ref, k_ref, v_ref, o_ref, keep, ref_row, end_row, order) in enumerate(plans):
        _, q_in, k_in, _, k_dec = decayed[d]
        units = [(c, hh) for c in order for hh in range(HG_HEADS)]
        v = v_ref[...]
        scores = {u: jnp.where(keep, _dot_nt(sub(q_in, *u), sub(k_in, *u)), 0.0).astype(BF16) for u in units}
        o_intra = {u: _dot(scores[u], sub(v, *u)) for u in units}
        contrib = {u: _dot_tn(sub(v, *u), sub(k_dec, *u)) for u in units}
        local.append((units, o_intra, contrib))
    for i in range(n_chunks * HG_HEADS):
        for d, (q_ref, g_ref, k_ref, v_ref, o_ref, keep, ref_row, end_row, order) in enumerate(plans):
            b, _, _, q_dec, _ = decayed[d]
            units, o_intra, contrib = local[d]
            c, hh = units[i]
            st = st_ref[d * HG_HEADS + hh]
            o_ref[c * CHUNK:(c + 1) * CHUNK, hh * HG_D:(hh + 1) * HG_D] = (
                o_intra[(c, hh)] + _dot_nt(sub(q_dec, c, hh), st.astype(BF16)))
            r_end = c * CHUNK + end_row
            decay = jnp.exp2(b[r_end:r_end + 1, hh * HG_D:(hh + 1) * HG_D])
            st_ref[d * HG_HEADS + hh] = st * decay + contrib[(c, hh)]


def _split_dot_lhs(m, a):
    hi = a.astype(BF16)
    lo = (a - hi.astype(F32)).astype(BF16)
    return _dot(m, hi) + _dot(m, lo)


def _hgrn(hq, gf, kf, gb, kb, iv, tri, batch, seq_len):
    rows = hq.shape[0]
    t = T_SCAN
    nb = seq_len // t
    fwd = lambda b, n: (b * nb + n, 0)
    bwd = lambda b, n: (b * nb + nb - 1 - n, 0)
    sf = pl.BlockSpec((t, HG_W), fwd)
    sb = pl.BlockSpec((t, HG_W), bwd)
    out = jax.ShapeDtypeStruct((rows, HG_W), F32)
    return pl.pallas_call(
        _hgrn_kernel,
        grid=(batch, nb),
        in_specs=[sf, sf, sf, sf, sb, sb, sb, sb, pl.BlockSpec(tri.shape, lambda b, n: (0, 0, 0))],
        out_specs=[sf, sb],
        out_shape=[out, out],
        scratch_shapes=[pltpu.VMEM((2 * HG_HEADS, HG_D, HG_D), F32)],
        compiler_params=pltpu.CompilerParams(
            dimension_semantics=("arbitrary", "arbitrary"), vmem_limit_bytes=VMEM_LIMIT),
        name="hgrn_scan",
    )(hq, gf, kf, iv, hq, gb, kb, iv, tri)


def _attn_kernel(q_ref, k_ref, v_ref, o_ref, m_ref, acc_ref, s_ref, bm_ref):
    tq = o_ref.shape[0]
    tk = v_ref.shape[3]
    n_kb = v_ref.shape[1]
    m_ref[...] = jnp.full(m_ref.shape, -1e30, F32)
    acc_ref[...] = jnp.zeros_like(acc_ref)

    def scores(j, slot):
        kblk = k_ref[0, pl.ds(pl.multiple_of(j * tk, tk), tk), :]
        for hh in range(ATT_GROUP):
            s = _dot(kblk, q_ref[0, :, hh * tq:(hh + 1) * tq])
            s_ref[slot, hh] = s
            bm_ref[slot, hh] = jnp.max(s, axis=0, keepdims=True)

    def accumulate(j, slot):
        vblk = v_ref[0, j]
        for hh in range(ATT_GROUP):
            m_old = m_ref[hh]
            m_new = jnp.maximum(m_old, bm_ref[slot, hh])
            p = jnp.exp2(s_ref[slot, hh] - m_new).astype(BF16)
            acc_ref[hh] = jnp.exp2(m_old - m_new) * acc_ref[hh] + _dot(vblk, p)
            m_ref[hh] = m_new

    scores(0, 0)

    def two_blocks(i, carry):
        j = 2 * i
        scores(j + 1, 1)
        accumulate(j, 0)
        scores(j + 2, 0)
        accumulate(j + 1, 1)
        return carry

    lax.fori_loop(0, n_kb // 2 - 1, two_blocks, 0)
    scores(n_kb - 1, 1)
    accumulate(n_kb - 2, 0)
    accumulate(n_kb - 1, 1)
    for pair in range(ATT_GROUP // 2):
        ot = jnp.concatenate(
            [acc_ref[2 * pair + e, 0:ATT_DH, :] / acc_ref[2 * pair + e, ATT_DH:ATT_DH + 1, :]
             for e in range(2)], axis=0)
        o_ref[:, pair * LANES:(pair + 1) * LANES] = ot.T


def _attention(qt, k, vt, batch, seq_len):
    rows = k.shape[1]
    nq = seq_len // TQ
    n_kb = seq_len // TK
    return pl.pallas_call(
        _attn_kernel,
        grid=(batch, ATT_KV, nq),
        in_specs=[
            pl.BlockSpec((1, ATT_DH, ATT_GROUP * TQ), lambda b, g, i: (g, 0, b * nq + i)),
            pl.BlockSpec((1, seq_len, ATT_DH), lambda b, g, i: (g, b, 0)),
            pl.BlockSpec((1, n_kb, V_ROWS, TK), lambda b, g, i: (g, b, 0, 0)),
        ],
        out_specs=pl.BlockSpec((TQ, ATT_GROUP * ATT_DH), lambda b, g, i: (b * nq + i, g)),
        out_shape=jax.ShapeDtypeStruct((rows, ATT_QW), F32),
        scratch_shapes=[
            pltpu.VMEM((ATT_GROUP, 1, TQ), F32),
            pltpu.VMEM((ATT_GROUP, V_ROWS, TQ), F32),
            pltpu.VMEM((2, ATT_GROUP, TK, TQ), F32),
            pltpu.VMEM((2, ATT_GROUP, 1, TQ), F32),
        ],
        compiler_params=pltpu.CompilerParams(
            dimension_semantics=("arbitrary", "arbitrary", "arbitrary"), vmem_limit_bytes=VMEM_LIMIT),
        name="attention",
    )(qt, k, vt)


def _rms(y, w):
    ms = jnp.mean(y * y, axis=-1, keepdims=True)
    return y * lax.rsqrt(ms + EPS) * w


def _outproj_kernel(of_ref, ob_ref, sg_ref, att_ref, x_ref, w_ref, hgw_ref, attw_ref, n2w_ref,
                    x1_ref, h2_ref):
    o = of_ref[...] + ob_ref[...]
    hgw = hgw_ref[...]
    o_hg = jnp.concatenate(
        [_rms(o[:, hh * HG_D:(hh + 1) * HG_D], hgw) for hh in range(HG_HEADS)], axis=1)
    o_hg = (o_hg * sg_ref[...].astype(F32)).astype(BF16)
    o_att = _rms(att_ref[...], attw_ref[...]).astype(BF16)
    y = x_ref[...] + _dot(o_hg, w_ref[0:HG_W, :]) + _dot(o_att, w_ref[HG_W:HG_W + ATT_QW, :])
    x1_ref[...] = y
    h2_ref[...] = _rms(y, n2w_ref[...]).astype(BF16)


def _outproj(o_f, o_b, sg, att, x2d, w_bf, hgw, attw, n2w):
    rows, d = x2d.shape
    tm = TM_OUT
    row_blk = lambda i: (i, 0)
    const = lambda i: (0, 0)
    half = pl.BlockSpec((tm, HG_W), row_blk)
    wide = pl.BlockSpec((tm, d), row_blk)
    return pl.pallas_call(
        _outproj_kernel,
        grid=(rows // tm,),
        in_specs=[half, half, half, half, wide,
                  pl.BlockSpec(w_bf.shape, const), pl.BlockSpec(hgw.shape, const),
                  pl.BlockSpec(attw.shape, const), pl.BlockSpec(n2w.shape, const)],
        out_specs=[wide, wide],
        out_shape=[jax.ShapeDtypeStruct((rows, d), F32), jax.ShapeDtypeStruct((rows, d), BF16)],
        compiler_params=pltpu.CompilerParams(
            dimension_semantics=("arbitrary",), vmem_limit_bytes=VMEM_LIMIT),
        name="outproj",
    )(o_f, o_b, sg, att, x2d, w_bf, hgw, attw, n2w)


def _ffn_kernel(h_ref, x1_ref, wgu_ref, wd_ref, fw_ref, o_ref, *, d_ff, final_norm):
    h = h_ref[...]
    acc = x1_ref[...]
    for c in range(d_ff // FF_CHUNK):
        lo = c * FF_CHUNK
        gate = _dot(h, wgu_ref[:, lo:lo + FF_CHUNK])
        up = _dot(h, wgu_ref[:, d_ff + lo:d_ff + lo + FF_CHUNK])
        act = (gate * _sigmoid(gate) * up).astype(BF16)
        acc = acc + _dot(act, wd_ref[lo:lo + FF_CHUNK, :])
    o_ref[...] = _rms(acc, fw_ref[...]) if final_norm else acc


def _ffn(h2, x1, wgu_bf, wd_bf, fw, final_norm):
    rows, d = x1.shape
    d_ff = wd_bf.shape[0]
    tm = TM_FFN
    row_blk = lambda i: (i, 0)
    const = lambda i: (0, 0)
    wide = pl.BlockSpec((tm, d), row_blk)
    resident = lambda shape: pl.BlockSpec(shape, const, pipeline_mode=pl.Buffered(1))
    return pl.pallas_call(
        functools.partial(_ffn_kernel, d_ff=d_ff, final_norm=final_norm),
        grid=(rows // tm,),
        in_specs=[wide, wide, resident(wgu_bf.shape), resident(wd_bf.shape), pl.BlockSpec(fw.shape, const)],
        out_specs=wide,
        out_shape=jax.ShapeDtypeStruct((rows, d), F32),
        compiler_params=pltpu.CompilerParams(
            dimension_semantics=("arbitrary",), vmem_limit_bytes=VMEM_LIMIT),
        name="ffn",
    )(h2, x1, wgu_bf, wd_bf, fw)


def _rope_tables(seq_len):
    rows = seq_len // GRID_W
    row = jnp.repeat(jnp.arange(rows), GRID_W).astype(F32)
    col = jnp.tile(jnp.arange(GRID_W), rows).astype(F32)
    axis_dim = ATT_DH // 2
    freqs = ROPE_THETA ** (-jnp.arange(0, axis_dim, 2, dtype=F32) / axis_dim)
    ang = jnp.concatenate([row[:, None] * freqs, col[:, None] * freqs], axis=-1)
    cos = jnp.repeat(jnp.cos(ang), 2, axis=1)
    sin = jnp.repeat(jnp.sin(ang), 2, axis=1) * jnp.tile(jnp.array([-1.0, 1.0], F32), ATT_DH // 2)
    reps = LANES // ATT_DH
    return jnp.tile(cos, (1, reps)), jnp.tile(sin, (1, reps))


def kernel(x, norm1_w, w_in, lb_logits, hg_norm_w, q_norm_w, k_norm_w, att_norm_w, w_out, norm2_w,
           w_gate_up, w_down, final_norm_w):
    batch, seq_len, d_model = x.shape
    depth = norm1_w.shape[0]
    rows = batch * seq_len
    assert seq_len % (2 * TK) == 0 and seq_len // TK >= 4 and seq_len % TQ == 0
    assert seq_len % T_SCAN == 0 and TK % TM_IN == 0
    assert TM_IN == TQ and rows % TM_FFN == 0 and w_in.shape[2] == D_IN

    lb_all = jnp.cumsum(jax.nn.softmax(lb_logits.astype(F32), axis=1), axis=1)
    cos_t, sin_t = _rope_tables(seq_len)
    blk = jnp.arange(ATT_QW) // ATT_DH
    mq = jnp.where(blk[:, None] == blk[None, :], 1.0 / ATT_DH, 0.0).astype(BF16)
    t_idx = jnp.arange(T_SCAN)
    same_chunk = (t_idx[:, None] // CHUNK) == (t_idx[None, :] // CHUNK)
    tri = jnp.stack([same_chunk & (t_idx[:, None] >= t_idx[None, :]),
                     same_chunk & (t_idx[None, :] >= t_idx[:, None])]).astype(BF16)
    q_scale = ATT_DH ** -0.5 * math.log2(math.e)

    x2d = x.reshape(rows, d_model)
    row = lambda v: v.astype(F32).reshape(1, -1)
    for l in range(depth):
        hq, gf, kf, gb, kb, iv, sg, qt, k, vt = _inproj(
            x2d, row(norm1_w[l]), w_in[l].astype(BF16), row(lb_all[0, l]), row(lb_all[1, l]),
            row(jnp.tile(q_norm_w[l], ATT_HEADS)) * q_scale, row(jnp.tile(k_norm_w[l], ATT_KV)),
            cos_t, sin_t, mq, seq_len)
        o_f, o_b = _hgrn(hq, gf, kf, gb, kb, iv, tri, batch, seq_len)
        att = _attention(qt, k, vt, batch, seq_len)
        x1, h2 = _outproj(o_f, o_b, sg, att, x2d, w_out[l].astype(BF16),
                          row(hg_norm_w[l]), row(att_norm_w[l]), row(norm2_w[l]))
        last = l == depth - 1
        x2d = _ffn(h2, x1, w_gate_up[l].astype(BF16), w_down[l].astype(BF16), row(final_norm_w), last)
    return x2d.reshape(batch, seq_len, d_model)
```

```python
import functools
import math

import jax
import jax.numpy as jnp
from jax import lax
from jax.experimental import pallas as pl
from jax.experimental.pallas import tpu as pltpu

F32 = jnp.float32
BF16 = jnp.bfloat16

EPS = 1e-6
GRID_W = 64
ROPE_THETA = 10000.0

HG_HEADS = 4
HG_D = 128
HG_W = HG_HEADS * HG_D
CHUNK = 64

ATT_HEADS = 8
ATT_KV = 2
ATT_GROUP = ATT_HEADS // ATT_KV
ATT_DH = 64
ATT_QW = ATT_HEADS * ATT_DH
ATT_KVW = ATT_KV * ATT_DH
V_ROWS = ATT_DH + 16

C_Q, C_FF, C_FB, C_I, C_G = 0, HG_W, 2 * HG_W, 3 * HG_W, 4 * HG_W
C_AQ = 5 * HG_W
C_AK = C_AQ + ATT_QW
C_AV = C_AK + ATT_KVW
D_IN = C_AV + ATT_KVW

LANES = 128
VMEM_LIMIT = 56 * 1024 * 1024

TM_IN = 256
TQ = 256
TK = 512
T_SCAN = 256
TM_OUT = 256
TM_FFN = 512
FF_CHUNK = 256


def _sigmoid(u):
    return 1.0 / (1.0 + jnp.exp(-u))


def _dot(a, b):
    return jnp.dot(a, b, preferred_element_type=F32)


def _dot_nt(a, b):
    return lax.dot_general(a, b, (((1,), (1,)), ((), ())), preferred_element_type=F32)


def _dot_tn(a, b):
    return lax.dot_general(a, b, (((0,), (0,)), ((), ())), preferred_element_type=F32)


def _split_dot(a, m):
    hi = a.astype(BF16)
    lo = (a - hi.astype(F32)).astype(BF16)
    return _dot(hi, m) + _dot(lo, m)


def _inproj_kernel(x_ref, n1w_ref, w_ref, lbf_ref, lbb_ref, wq_ref, wk_ref, cos_ref, sin_ref, mq_ref,
                   hq_ref, gf_ref, kf_ref, gb_ref, kb_ref, iv_ref, sg_ref, qt_ref, k_ref, vt_ref):
    tm = x_ref.shape[0]
    x = x_ref[...]
    ms = jnp.mean(x * x, axis=-1, keepdims=True)
    h = (x * lax.rsqrt(ms + EPS) * n1w_ref[...]).astype(BF16)

    def proj(lo, hi):
        return _dot(h, w_ref[:, lo:hi])

    u = proj(C_Q, C_FF)
    hq_ref[...] = (u * _sigmoid(u)).astype(BF16)

    def gates(z, lb, g_ref, k_out_ref):
        e = jnp.exp(-jnp.abs(z))
        r = 1.0 / (1.0 + e)
        er = e * r
        pos = z >= 0.0
        s = jnp.where(pos, r, er)
        sn = jnp.where(pos, er, r)
        g_ref[...] = jnp.log2(lb + (1.0 - lb) * s)
        k_out_ref[...] = ((1.0 - lb) * sn).astype(BF16)

    gates(proj(C_FF, C_FB), lbf_ref[...], gf_ref, kf_ref)
    gates(proj(C_FB, C_I), lbb_ref[...], gb_ref, kb_ref)
    iv_ref[...] = proj(C_I, C_G).astype(BF16)
    u = proj(C_G, C_AQ)
    sg_ref[...] = (u * _sigmoid(u)).astype(BF16)

    cos = cos_ref[...]
    sin = sin_ref[...]
    lane = lax.broadcasted_iota(jnp.int32, (tm, LANES), 1)
    even = (lane & 1) == 0

    def rope(xc):
        partner = jnp.where(even, pltpu.roll(xc, LANES - 1, 1), pltpu.roll(xc, 1, 1))
        return xc * cos + partner * sin

    aq = proj(C_AQ, C_AK)
    yq = aq * lax.rsqrt(_split_dot(aq * aq, mq_ref[...]) + EPS) * wq_ref[...]
    for c in range(ATT_QW // LANES):
        qct = rope(yq[:, c * LANES:(c + 1) * LANES]).T.astype(BF16)
        grp, pair = divmod(c, 2)
        for e in range(2):
            hl = 2 * pair + e
            qt_ref[grp, :, hl * tm:(hl + 1) * tm] = qct[e * ATT_DH:(e + 1) * ATT_DH, :]

    ak = proj(C_AK, C_AV)
    yk = ak * lax.rsqrt(_split_dot(ak * ak, mq_ref[0:LANES, 0:LANES]) + EPS) * wk_ref[...]
    yk = rope(yk).astype(BF16)
    k_ref[0] = yk[:, 0:ATT_DH]
    k_ref[1] = yk[:, ATT_DH:2 * ATT_DH]

    avt = proj(C_AV, D_IN).T.astype(BF16)
    ones = jnp.ones((V_ROWS - ATT_DH, tm), BF16)
    for g in range(ATT_KV):
        vt_ref[g, 0, 0:ATT_DH, :] = avt[g * ATT_DH:(g + 1) * ATT_DH, :]
        vt_ref[g, 0, ATT_DH:V_ROWS, :] = ones


def _inproj(x2d, n1w, w_bf, lbf, lbb, wq, wk, cos_t, sin_t, mq, seq_len):
    rows = x2d.shape[0]
    tm = TM_IN
    n_tiles = rows // tm
    tiles_per_seq = seq_len // tm
    per_tk = TK // tm
    row_blk = lambda i: (i, 0)
    const = lambda i: (0, 0)
    full = lambda shape: pl.BlockSpec(shape, const)
    act_bf = jax.ShapeDtypeStruct((rows, HG_W), BF16)
    act_f32 = jax.ShapeDtypeStruct((rows, HG_W), F32)
    act_spec = pl.BlockSpec((tm, HG_W), row_blk)
    return pl.pallas_call(
        _inproj_kernel,
        grid=(n_tiles,),
        in_specs=[
            pl.BlockSpec((tm, x2d.shape[1]), row_blk),
            full(n1w.shape), full(w_bf.shape), full(lbf.shape), full(lbb.shape),
            full(wq.shape), full(wk.shape),
            pl.BlockSpec((tm, LANES), lambda i: (i % tiles_per_seq, 0)),
            pl.BlockSpec((tm, LANES), lambda i: (i % tiles_per_seq, 0)),
            full(mq.shape),
        ],
        out_specs=[
            act_spec, act_spec, act_spec, act_spec, act_spec, act_spec, act_spec,
            pl.BlockSpec((ATT_KV, ATT_DH, ATT_GROUP * tm), lambda i: (0, 0, i)),
            pl.BlockSpec((ATT_KV, tm, ATT_DH), lambda i: (0, i, 0)),
            pl.BlockSpec((ATT_KV, 1, V_ROWS, tm), lambda i: (0, i // per_tk, 0, i % per_tk)),
        ],
        out_shape=[
            act_bf, act_f32, act_bf, act_f32, act_bf, act_bf, act_bf,
            jax.ShapeDtypeStruct((ATT_KV, ATT_DH, ATT_GROUP * rows), BF16),
            jax.ShapeDtypeStruct((ATT_KV, rows, ATT_DH), BF16),
            jax.ShapeDtypeStruct((ATT_KV, rows // TK, V_ROWS, TK), BF16),
        ],
        compiler_params=pltpu.CompilerParams(
            dimension_semantics=("arbitrary",), vmem_limit_bytes=VMEM_LIMIT),
        name="inproj",
    )(x2d, n1w, w_bf, lbf, lbb, wq, wk, cos_t, sin_t, mq)


def _hgrn_kernel(qf_ref, gf_ref, kf_ref, vf_ref, qb_ref, gb_ref, kb_ref, vb_ref, tri_ref,
                 of_ref, ob_ref, st_ref):
    @pl.when(pl.program_id(1) == 0)
    def _():
        st_ref[...] = jnp.zeros_like(st_ref)

    n_chunks = qf_ref.shape[0] // CHUNK
    row = lax.broadcasted_iota(jnp.int32, (CHUNK, CHUNK), 0)
    col = lax.broadcasted_iota(jnp.int32, (CHUNK, CHUNK), 1)
    plans = (
        (qf_ref, gf_ref, kf_ref, vf_ref, of_ref, row >= col, CHUNK // 2 - 1, CHUNK - 1, range(n_chunks)),
        (qb_ref, gb_ref, kb_ref, vb_ref, ob_ref, col >= row, CHUNK // 2, 0, range(n_chunks - 1, -1, -1)),
    )

    def sub(a, c, hh):
        return a[c * CHUNK:(c + 1) * CHUNK, hh * HG_D:(hh + 1) * HG_D]

    decayed = []
    for d, (q_ref, g_ref, k_ref, v_ref, o_ref, keep, ref_row, end_row, order) in enumerate(plans):
        b = _split_dot_lhs(tri_ref[d], g_ref[...])

        def chunk_rows(r, b=b):
            return jnp.concatenate(
                [jnp.broadcast_to(b[c * CHUNK + r:c * CHUNK + r + 1, :], (CHUNK, HG_W))
                 for c in range(n_chunks)], axis=0)

        b_ref = chunk_rows(ref_row)
        b_end = chunk_rows(end_row)
        q = q_ref[...].astype(F32)
        k = k_ref[...].astype(F32)
        decayed.append((
            b,
            (q * jnp.exp2(b - b_ref)).astype(BF16),
            (k * jnp.exp2(b_ref - b)).astype(BF16),
            (q * jnp.exp2(b)).astype(BF16),
            (k * jnp.exp2(b_end - b)).astype(BF16),
        ))
    local = []
    for d, (q_ref, g_ref, k_ref, v_ref, o_ref, keep, ref_row, end_row, order) in enumerate(plans):
        _, q_in, k_in, _, k_dec = decayed[d]
        units = [(c, hh) for c in order for hh in range(HG_HEADS)]
        v = v_ref[...]
        scores = {u: jnp.where(keep, _dot_nt(sub(q_in, *u), sub(k_in, *u)), 0.0).astype(BF16) for u in units}
        o_intra = {u: _dot(scores[u], sub(v, *u)) for u in units}
        contrib = {u: _dot_tn(sub(v, *u), sub(k_dec, *u)) for u in units}
        local.append((units, o_intra, contrib))
    for i in range(n_chunks * HG_HEADS):
        for d, (q_ref, g_ref, k_ref, v_ref, o_ref, keep, ref_row, end_row, order) in enumerate(plans):
            b, _, _, q_dec, _ = decayed[d]
            units, o_intra, contrib = local[d]
            c, hh = units[i]
            st = st_ref[d * HG_HEADS + hh]
            o_ref[c * CHUNK:(c + 1) * CHUNK, hh * HG_D:(hh + 1) * HG_D] = (
                o_intra[(c, hh)] + _dot_nt(sub(q_dec, c, hh), st.astype(BF16)))
            r_end = c * CHUNK + end_row
            decay = jnp.exp2(b[r_end:r_end + 1, hh * HG_D:(hh + 1) * HG_D])
            st_ref[d * HG_HEADS + hh] = st * decay + contrib[(c, hh)]


def _split_dot_lhs(m, a):
    hi = a.astype(BF16)
    lo = (a - hi.astype(F32)).astype(BF16)
    return _dot(m, hi) + _dot(m, lo)


def _hgrn(hq, gf, kf, gb, kb, iv, tri, batch, seq_len):
    rows = hq.shape[0]
    t = T_SCAN
    nb = seq_len // t
    fwd = lambda b, n: (b * nb + n, 0)
    bwd = lambda b, n: (b * nb + nb - 1 - n, 0)
    sf = pl.BlockSpec((t, HG_W), fwd)
    sb = pl.BlockSpec((t, HG_W), bwd)
    out = jax.ShapeDtypeStruct((rows, HG_W), F32)
    return pl.pallas_call(
        _hgrn_kernel,
        grid=(batch, nb),
        in_specs=[sf, sf, sf, sf, sb, sb, sb, sb, pl.BlockSpec(tri.shape, lambda b, n: (0, 0, 0))],
        out_specs=[sf, sb],
        out_shape=[out, out],
        scratch_shapes=[pltpu.VMEM((2 * HG_HEADS, HG_D, HG_D), F32)],
        compiler_params=pltpu.CompilerParams(
            dimension_semantics=("arbitrary", "arbitrary"), vmem_limit_bytes=VMEM_LIMIT),
        name="hgrn_scan",
    )(hq, gf, kf, iv, hq, gb, kb, iv, tri)


def _attn_kernel(q_ref, qn_ref, k_ref, v_ref, o_ref, m_ref, acc_ref, s_ref, bm_ref):
    tq = o_ref.shape[0]
    tk = v_ref.shape[3]
    n_kb = v_ref.shape[1]

    def scores(qsrc_ref, j, slot):
        kblk = k_ref[0, pl.ds(pl.multiple_of(j * tk, tk), tk), :]
        for hh in range(ATT_GROUP):
            s = _dot(kblk, qsrc_ref[0, :, hh * tq:(hh + 1) * tq])
            s_ref[slot, hh] = s
            bm_ref[slot, hh] = jnp.max(s, axis=0, keepdims=True)

    def accumulate(j, slot):
        vblk = v_ref[0, j]
        for hh in range(ATT_GROUP):
            m_old = m_ref[hh]
            m_new = jnp.maximum(m_old, bm_ref[slot, hh])
            p = jnp.exp2(s_ref[slot, hh] - m_new).astype(BF16)
            acc_ref[hh] = jnp.exp2(m_old - m_new) * acc_ref[hh] + _dot(vblk, p)
            m_ref[hh] = m_new

    @pl.when(pl.program_id(2) == 0)
    def _():
        scores(q_ref, 0, 0)

    m_ref[...] = jnp.full(m_ref.shape, -1e30, F32)
    acc_ref[...] = jnp.zeros_like(acc_ref)

    def four_blocks(i, carry):
        j = 4 * i
        for u in range(4):
            scores(q_ref, j + u + 1, (u + 1) % 2)
            accumulate(j + u, u % 2)
        return carry

    lax.fori_loop(0, n_kb // 4 - 1, four_blocks, 0)
    j = n_kb - 4
    for u in range(3):
        scores(q_ref, j + u + 1, (u + 1) % 2)
        accumulate(j + u, u % 2)
    scores(qn_ref, 0, 0)
    accumulate(n_kb - 1, 1)
    for pair in range(ATT_GROUP // 2):
        ot = jnp.concatenate(
            [acc_ref[2 * pair + e, 0:ATT_DH, :] / acc_ref[2 * pair + e, ATT_DH:ATT_DH + 1, :]
             for e in range(2)], axis=0)
        o_ref[:, pair * LANES:(pair + 1) * LANES] = ot.T


def _attention(qt, k, vt, batch, seq_len):
    rows = k.shape[1]
    nq = seq_len // TQ
    n_kb = seq_len // TK
    return pl.pallas_call(
        _attn_kernel,
        grid=(batch, ATT_KV, nq),
        in_specs=[
            pl.BlockSpec((1, ATT_DH, ATT_GROUP * TQ), lambda b, g, i: (g, 0, b * nq + i)),
            pl.BlockSpec((1, ATT_DH, ATT_GROUP * TQ), lambda b, g, i: (g, 0, b * nq + jnp.minimum(i + 1, nq - 1))),
            pl.BlockSpec((1, seq_len, ATT_DH), lambda b, g, i: (g, b, 0)),
            pl.BlockSpec((1, n_kb, V_ROWS, TK), lambda b, g, i: (g, b, 0, 0)),
        ],
        out_specs=pl.BlockSpec((TQ, ATT_GROUP * ATT_DH), lambda b, g, i: (b * nq + i, g)),
        out_shape=jax.ShapeDtypeStruct((rows, ATT_QW), F32),
        scratch_shapes=[
            pltpu.VMEM((ATT_GROUP, 1, TQ), F32),
            pltpu.VMEM((ATT_GROUP, V_ROWS, TQ), F32),
            pltpu.VMEM((2, ATT_GROUP, TK, TQ), F32),
            pltpu.VMEM((2, ATT_GROUP, 1, TQ), F32),
        ],
        compiler_params=pltpu.CompilerParams(
            dimension_semantics=("arbitrary", "arbitrary", "arbitrary"), vmem_limit_bytes=VMEM_LIMIT),
        name="attention",
    )(qt, qt, k, vt)


def _rms(y, w):
    ms = jnp.mean(y * y, axis=-1, keepdims=True)
    return y * lax.rsqrt(ms + EPS) * w


def _outproj_kernel(of_ref, ob_ref, sg_ref, att_ref, x_ref, w_ref, hgw_ref, attw_ref, n2w_ref,
                    x1_ref, h2_ref):
    o = of_ref[...] + ob_ref[...]
    hgw = hgw_ref[...]
    o_hg = jnp.concatenate(
        [_rms(o[:, hh * HG_D:(hh + 1) * HG_D], hgw) for hh in range(HG_HEADS)], axis=1)
    o_hg = (o_hg * sg_ref[...].astype(F32)).astype(BF16)
    o_att = _rms(att_ref[...], attw_ref[...]).astype(BF16)
    y = x_ref[...] + _dot(o_hg, w_ref[0:HG_W, :]) + _dot(o_att, w_ref[HG_W:HG_W + ATT_QW, :])
    x1_ref[...] = y
    h2_ref[...] = _rms(y, n2w_ref[...]).astype(BF16)


def _outproj(o_f, o_b, sg, att, x2d, w_bf, hgw, attw, n2w):
    rows, d = x2d.shape
    tm = TM_OUT
    row_blk = lambda i: (i, 0)
    const = lambda i: (0, 0)
    half = pl.BlockSpec((tm, HG_W), row_blk)
    wide = pl.BlockSpec((tm, d), row_blk)
    return pl.pallas_call(
        _outproj_kernel,
        grid=(rows // tm,),
        in_specs=[half, half, half, half, wide,
                  pl.BlockSpec(w_bf.shape, const), pl.BlockSpec(hgw.shape, const),
                  pl.BlockSpec(attw.shape, const), pl.BlockSpec(n2w.shape, const)],
        out_specs=[wide, wide],
        out_shape=[jax.ShapeDtypeStruct((rows, d), F32), jax.ShapeDtypeStruct((rows, d), BF16)],
        compiler_params=pltpu.CompilerParams(
            dimension_semantics=("arbitrary",), vmem_limit_bytes=VMEM_LIMIT),
        name="outproj",
    )(o_f, o_b, sg, att, x2d, w_bf, hgw, attw, n2w)


def _ffn_kernel(h_ref, x1_ref, wgu_ref, wd_ref, fw_ref, o_ref, *, d_ff, final_norm):
    h = h_ref[...]
    acc = x1_ref[...]
    for c in range(d_ff // FF_CHUNK):
        lo = c * FF_CHUNK
        gate = _dot(h, wgu_ref[:, lo:lo + FF_CHUNK])
        up = _dot(h, wgu_ref[:, d_ff + lo:d_ff + lo + FF_CHUNK])
        act = (gate * _sigmoid(gate) * up).astype(BF16)
        acc = acc + _dot(act, wd_ref[lo:lo + FF_CHUNK, :])
    o_ref[...] = _rms(acc, fw_ref[...]) if final_norm else acc


def _ffn(h2, x1, wgu_bf, wd_bf, fw, final_norm):
    rows, d = x1.shape
    d_ff = wd_bf.shape[0]
    tm = TM_FFN
    row_blk = lambda i: (i, 0)
    const = lambda i: (0, 0)
    wide = pl.BlockSpec((tm, d), row_blk)
    resident = lambda shape: pl.BlockSpec(shape, const, pipeline_mode=pl.Buffered(1))
    return pl.pallas_call(
        functools.partial(_ffn_kernel, d_ff=d_ff, final_norm=final_norm),
        grid=(rows // tm,),
        in_specs=[wide, wide, resident(wgu_bf.shape), resident(wd_bf.shape), pl.BlockSpec(fw.shape, const)],
        out_specs=wide,
        out_shape=jax.ShapeDtypeStruct((rows, d), F32),
        compiler_params=pltpu.CompilerParams(
            dimension_semantics=("arbitrary",), vmem_limit_bytes=VMEM_LIMIT),
        name="ffn",
    )(h2, x1, wgu_bf, wd_bf, fw)


def _rope_tables(seq_len):
    rows = seq_len // GRID_W
    row = jnp.repeat(jnp.arange(rows), GRID_W).astype(F32)
    col = jnp.tile(jnp.arange(GRID_W), rows).astype(F32)
    axis_dim = ATT_DH // 2
    freqs = ROPE_THETA ** (-jnp.arange(0, axis_dim, 2, dtype=F32) / axis_dim)
    ang = jnp.concatenate([row[:, None] * freqs, col[:, None] * freqs], axis=-1)
    cos = jnp.repeat(jnp.cos(ang), 2, axis=1)
    sin = jnp.repeat(jnp.sin(ang), 2, axis=1) * jnp.tile(jnp.array([-1.0, 1.0], F32), ATT_DH // 2)
    reps = LANES // ATT_DH
    return jnp.tile(cos, (1, reps)), jnp.tile(sin, (1, reps))


def kernel(x, norm1_w, w_in, lb_logits, hg_norm_w, q_norm_w, k_norm_w, att_norm_w, w_out, norm2_w,
           w_gate_up, w_down, final_norm_w):
    batch, seq_len, d_model = x.shape
    depth = norm1_w.shape[0]
    rows = batch * seq_len
    assert seq_len % (4 * TK) == 0 and seq_len // TK >= 8 and seq_len % TQ == 0
    assert seq_len % T_SCAN == 0 and TK % TM_IN == 0
    assert TM_IN == TQ and rows % TM_FFN == 0 and w_in.shape[2] == D_IN

    lb_all = jnp.cumsum(jax.nn.softmax(lb_logits.astype(F32), axis=1), axis=1)
    cos_t, sin_t = _rope_tables(seq_len)
    blk = jnp.arange(ATT_QW) // ATT_DH
    mq = jnp.where(blk[:, None] == blk[None, :], 1.0 / ATT_DH, 0.0).astype(BF16)
    t_idx = jnp.arange(T_SCAN)
    same_chunk = (t_idx[:, None] // CHUNK) == (t_idx[None, :] // CHUNK)
    tri = jnp.stack([same_chunk & (t_idx[:, None] >= t_idx[None, :]),
                     same_chunk & (t_idx[None, :] >= t_idx[:, None])]).astype(BF16)
    q_scale = ATT_DH ** -0.5 * math.log2(math.e)

    x2d = x.reshape(rows, d_model)
    row = lambda v: v.astype(F32).reshape(1, -1)
    for l in range(depth):
        hq, gf, kf, gb, kb, iv, sg, qt, k, vt = _inproj(
            x2d, row(norm1_w[l]), w_in[l].astype(BF16), row(lb_all[0, l]), row(lb_all[1, l]),
            row(jnp.tile(q_norm_w[l], ATT_HEADS)) * q_scale, row(jnp.tile(k_norm_w[l], ATT_KV)),
            cos_t, sin_t, mq, seq_len)
        o_f, o_b = _hgrn(hq, gf, kf, gb, kb, iv, tri, batch, seq_len)
        att = _attention(qt, k, vt, batch, seq_len)
        x1, h2 = _outproj(o_f, o_b, sg, att, x2d, w_out[l].astype(BF16),
                          row(hg_norm_w[l]), row(att_norm_w[l]), row(norm2_w[l]))
        last = l == depth - 1
        x2d = _ffn(h2, x1, w_gate_up[l].astype(BF16), w_down[l].astype(BF16), row(final_norm_w), last)
    return x2d.reshape(batch, seq_len, d_model)
```

```python
import functools
import math

import jax
import jax.numpy as jnp
from jax import lax
from jax.experimental import pallas as pl
from jax.experimental.pallas import tpu as pltpu

F32 = jnp.float32
BF16 = jnp.bfloat16

EPS = 1e-6
GRID_W = 64
ROPE_THETA = 10000.0

HG_HEADS = 4
HG_D = 128
HG_W = HG_HEADS * HG_D
CHUNK = 64

ATT_HEADS = 8
ATT_KV = 2
ATT_GROUP = ATT_HEADS // ATT_KV
ATT_DH = 64
ATT_QW = ATT_HEADS * ATT_DH
ATT_KVW = ATT_KV * ATT_DH
V_ROWS = ATT_DH + 16

C_Q, C_FF, C_FB, C_I, C_G = 0, HG_W, 2 * HG_W, 3 * HG_W, 4 * HG_W
C_AQ = 5 * HG_W
C_AK = C_AQ + ATT_QW
C_AV = C_AK + ATT_KVW
D_IN = C_AV + ATT_KVW

LANES = 128
VMEM_LIMIT = 56 * 1024 * 1024

TM_IN = 512
TQ = 256
TK = 512
T_SCAN = 256
TM_FFN = 512
FF_CHUNK = 256


def _sigmoid(u):
    return 1.0 / (1.0 + jnp.exp(-u))


def _dot(a, b):
    return jnp.dot(a, b, preferred_element_type=F32)


def _dot_nt(a, b):
    return lax.dot_general(a, b, (((1,), (1,)), ((), ())), preferred_element_type=F32)


def _dot_tn(a, b):
    return lax.dot_general(a, b, (((0,), (0,)), ((), ())), preferred_element_type=F32)


def _inproj_kernel(x_ref, n1w_ref, w_ref, lbf_ref, lbb_ref, wq_ref, wk_ref, cos_ref, sin_ref, mq_ref,
                   hq_ref, gf_ref, kf_ref, gb_ref, kb_ref, iv_ref, sg_ref, qt_ref, k_ref, vt_ref):
    tm = x_ref.shape[0]
    x = x_ref[...]
    ms = jnp.mean(x * x, axis=-1, keepdims=True)
    h = (x * lax.rsqrt(ms + EPS) * n1w_ref[...]).astype(BF16)

    def proj(lo, hi):
        return _dot(h, w_ref[:, lo:hi])

    def gates(z, lb, g_ref, k_out_ref):
        e = jnp.exp(-jnp.abs(z))
        r = 1.0 / (1.0 + e)
        er = e * r
        pos = z >= 0.0
        s = jnp.where(pos, r, er)
        sn = jnp.where(pos, er, r)
        g_ref[...] = jnp.log2(lb + (1.0 - lb) * s)
        k_out_ref[...] = ((1.0 - lb) * sn).astype(BF16)

    cos = cos_ref[...]
    sin = sin_ref[...]
    lane = lax.broadcasted_iota(jnp.int32, (tm, LANES), 1)
    even = (lane & 1) == 0

    def rope(xc):
        partner = jnp.where(even, pltpu.roll(xc, LANES - 1, 1), pltpu.roll(xc, 1, 1))
        return xc * cos + partner * sin

    def head_rms(a, m, w):
        return a * lax.rsqrt(_dot((a * a).astype(BF16), m) + EPS) * w

    def ep_aq(aq):
        yq = head_rms(aq, mq_ref[...], wq_ref[...])
        for c in range(ATT_QW // LANES):
            qct = rope(yq[:, c * LANES:(c + 1) * LANES]).T.astype(BF16)
            grp, pair = divmod(c, 2)
            for e in range(2):
                hl = 2 * pair + e
                for blk in range(tm // TQ):
                    col = (blk * ATT_GROUP + hl) * TQ
                    qt_ref[grp, :, col:col + TQ] = qct[e * ATT_DH:(e + 1) * ATT_DH, blk * TQ:(blk + 1) * TQ]

    def ep_ak(ak):
        yk = rope(head_rms(ak, mq_ref[0:LANES, 0:LANES], wk_ref[...])).astype(BF16)
        k_ref[0] = yk[:, 0:ATT_DH]
        k_ref[1] = yk[:, ATT_DH:2 * ATT_DH]

    def ep_av(av):
        avt = av.T.astype(BF16)
        ones = jnp.ones((V_ROWS - ATT_DH, tm), BF16)
        for g in range(ATT_KV):
            vt_ref[g, 0, 0:ATT_DH, :] = avt[g * ATT_DH:(g + 1) * ATT_DH, :]
            vt_ref[g, 0, ATT_DH:V_ROWS, :] = ones

    def ep_ff(u):
        gates(u, lbf_ref[...], gf_ref, kf_ref)

    def ep_fb(u):
        gates(u, lbb_ref[...], gb_ref, kb_ref)

    def ep_q(u):
        hq_ref[...] = (u * _sigmoid(u)).astype(BF16)

    def ep_g(u):
        sg_ref[...] = (u * _sigmoid(u)).astype(BF16)

    def ep_i(u):
        iv_ref[...] = u.astype(BF16)

    stages = ((C_AQ, C_AK, ep_aq), (C_AK, C_AV, ep_ak), (C_AV, D_IN, ep_av), (C_FF, C_FB, ep_ff),
              (C_FB, C_I, ep_fb), (C_Q, C_FF, ep_q), (C_G, C_AQ, ep_g), (C_I, C_G, ep_i))
    pending = None
    for lo, hi, epilogue in stages:
        u = proj(lo, hi)
        if pending is not None:
            pending[0](pending[1])
        pending = (epilogue, u)
    pending[0](pending[1])


def _inproj(x2d, n1w, w_bf, lbf, lbb, wq, wk, cos_t, sin_t, mq, seq_len):
    rows = x2d.shape[0]
    tm = TM_IN
    n_tiles = rows // tm
    tiles_per_seq = seq_len // tm
    per_tk = TK // tm
    row_blk = lambda i: (i, 0)
    const = lambda i: (0, 0)
    full = lambda shape: pl.BlockSpec(shape, const)
    act_bf = jax.ShapeDtypeStruct((rows, HG_W), BF16)
    act_f32 = jax.ShapeDtypeStruct((rows, HG_W), F32)
    act_spec = pl.BlockSpec((tm, HG_W), row_blk)
    return pl.pallas_call(
        _inproj_kernel,
        grid=(n_tiles,),
        in_specs=[
            pl.BlockSpec((tm, x2d.shape[1]), row_blk),
            full(n1w.shape), full(w_bf.shape), full(lbf.shape), full(lbb.shape),
            full(wq.shape), full(wk.shape),
            pl.BlockSpec((tm, LANES), lambda i: (i % tiles_per_seq, 0)),
            pl.BlockSpec((tm, LANES), lambda i: (i % tiles_per_seq, 0)),
            full(mq.shape),
        ],
        out_specs=[
            act_spec, act_spec, act_spec, act_spec, act_spec, act_spec, act_spec,
            pl.BlockSpec((ATT_KV, ATT_DH, ATT_GROUP * tm), lambda i: (0, 0, i)),
            pl.BlockSpec((ATT_KV, tm, ATT_DH), lambda i: (0, i, 0)),
            pl.BlockSpec((ATT_KV, 1, V_ROWS, tm), lambda i: (0, i // per_tk, 0, i % per_tk)),
        ],
        out_shape=[
            act_bf, act_f32, act_bf, act_f32, act_bf, act_bf, act_bf,
            jax.ShapeDtypeStruct((ATT_KV, ATT_DH, ATT_GROUP * rows), BF16),
            jax.ShapeDtypeStruct((ATT_KV, rows, ATT_DH), BF16),
            jax.ShapeDtypeStruct((ATT_KV, rows // TK, V_ROWS, TK), BF16),
        ],
        compiler_params=pltpu.CompilerParams(
            dimension_semantics=("arbitrary",), vmem_limit_bytes=VMEM_LIMIT),
        name="inproj",
    )(x2d, n1w, w_bf, lbf, lbb, wq, wk, cos_t, sin_t, mq)


def _hgrn_kernel(qf_ref, gf_ref, kf_ref, vf_ref, qb_ref, gb_ref, kb_ref, vb_ref, tri_ref,
                 of_ref, ob_ref, st_ref):
    @pl.when(pl.program_id(1) == 0)
    def _():
        st_ref[...] = jnp.zeros_like(st_ref)

    n_chunks = qf_ref.shape[0] // CHUNK
    row = lax.broadcasted_iota(jnp.int32, (CHUNK, CHUNK), 0)
    col = lax.broadcasted_iota(jnp.int32, (CHUNK, CHUNK), 1)
    plans = (
        (qf_ref, gf_ref, kf_ref, vf_ref, of_ref, row >= col, CHUNK // 2 - 1, CHUNK - 1, range(n_chunks)),
        (qb_ref, gb_ref, kb_ref, vb_ref, ob_ref, col >= row, CHUNK // 2, 0, range(n_chunks - 1, -1, -1)),
    )

    def sub(a, c, hh):
        return a[c * CHUNK:(c + 1) * CHUNK, hh * HG_D:(hh + 1) * HG_D]

    decayed = []
    for d, (q_ref, g_ref, k_ref, v_ref, o_ref, keep, ref_row, end_row, order) in enumerate(plans):
        b = _split_dot_lhs(tri_ref[d], g_ref[...])

        def chunk_rows(r, b=b):
            return jnp.concatenate(
                [jnp.broadcast_to(b[c * CHUNK + r:c * CHUNK + r + 1, :], (CHUNK, HG_W))
                 for c in range(n_chunks)], axis=0)

        b_ref = chunk_rows(ref_row)
        b_end = chunk_rows(end_row)
        q = q_ref[...].astype(F32)
        k = k_ref[...].astype(F32)
        decayed.append((
            b,
            (q * jnp.exp2(b - b_ref)).astype(BF16),
            (k * jnp.exp2(b_ref - b)).astype(BF16),
            (q * jnp.exp2(b)).astype(BF16),
            (k * jnp.exp2(b_end - b)).astype(BF16),
        ))
    local = []
    for d, (q_ref, g_ref, k_ref, v_ref, o_ref, keep, ref_row, end_row, order) in enumerate(plans):
        _, q_in, k_in, _, k_dec = decayed[d]
        units = [(c, hh) for c in order for hh in range(HG_HEADS)]
        v = v_ref[...]
        scores = {u: jnp.where(keep, _dot_nt(sub(q_in, *u), sub(k_in, *u)), 0.0).astype(BF16) for u in units}
        o_intra = {u: _dot(scores[u], sub(v, *u)) for u in units}
        contrib = {u: _dot_tn(sub(v, *u), sub(k_dec, *u)) for u in units}
        local.append((units, o_intra, contrib))
    for i in range(n_chunks * HG_HEADS):
        for d, (q_ref, g_ref, k_ref, v_ref, o_ref, keep, ref_row, end_row, order) in enumerate(plans):
            b, _, _, q_dec, _ = decayed[d]
            units, o_intra, contrib = local[d]
            c, hh = units[i]
            st = st_ref[d * HG_HEADS + hh]
            o_ref[c * CHUNK:(c + 1) * CHUNK, hh * HG_D:(hh + 1) * HG_D] = (
                o_intra[(c, hh)] + _dot_nt(sub(q_dec, c, hh), st.astype(BF16)))
            r_end = c * CHUNK + end_row
            decay = jnp.exp2(b[r_end:r_end + 1, hh * HG_D:(hh + 1) * HG_D])
            st_ref[d * HG_HEADS + hh] = st * decay + contrib[(c, hh)]


def _split_dot_lhs(m, a):
    hi = a.astype(BF16)
    lo = (a - hi.astype(F32)).astype(BF16)
    return _dot(m, hi) + _dot(m, lo)


def _hgrn(hq, gf, kf, gb, kb, iv, tri, batch, seq_len):
    rows = hq.shape[0]
    t = T_SCAN
    nb = seq_len // t
    fwd = lambda b, n: (b * nb + n, 0)
    bwd = lambda b, n: (b * nb + nb - 1 - n, 0)
    sf = pl.BlockSpec((t, HG_W), fwd)
    sb = pl.BlockSpec((t, HG_W), bwd)
    out = jax.ShapeDtypeStruct((rows, HG_W), F32)
    return pl.pallas_call(
        _hgrn_kernel,
        grid=(batch, nb),
        in_specs=[sf, sf, sf, sf, sb, sb, sb, sb, pl.BlockSpec(tri.shape, lambda b, n: (0, 0, 0))],
        out_specs=[sf, sb],
        out_shape=[out, out],
        scratch_shapes=[pltpu.VMEM((2 * HG_HEADS, HG_D, HG_D), F32)],
        compiler_params=pltpu.CompilerParams(
            dimension_semantics=("arbitrary", "arbitrary"), vmem_limit_bytes=VMEM_LIMIT),
        name="hgrn_scan",
    )(hq, gf, kf, iv, hq, gb, kb, iv, tri)


def _attn_kernel(q_ref, qn_ref, k_ref, v_ref, o_ref, m_ref, acc_ref, s_ref, bm_ref):
    tq = o_ref.shape[0]
    tk = v_ref.shape[3]
    n_kb = v_ref.shape[1]

    def scores(qsrc_ref, j, slot):
        kblk = k_ref[0, pl.ds(pl.multiple_of(j * tk, tk), tk), :]
        for hh in range(ATT_GROUP):
            s = _dot(kblk, qsrc_ref[0, :, hh * tq:(hh + 1) * tq])
            s_ref[slot, hh] = s
            bm_ref[slot, hh] = jnp.max(s, axis=0, keepdims=True)

    def accumulate(j, slot):
        vblk = v_ref[0, j]
        for hh in range(ATT_GROUP):
            m_old = m_ref[hh]
            m_new = jnp.maximum(m_old, bm_ref[slot, hh])
            p = jnp.exp2(s_ref[slot, hh] - m_new).astype(BF16)
            acc_ref[hh] = jnp.exp2(m_old - m_new) * acc_ref[hh] + _dot(vblk, p)
            m_ref[hh] = m_new

    @pl.when(pl.program_id(2) == 0)
    def _():
        scores(q_ref, 0, 0)

    m_ref[...] = jnp.full(m_ref.shape, -1e30, F32)
    acc_ref[...] = jnp.zeros_like(acc_ref)

    def four_blocks(i, carry):
        j = 4 * i
        for u in range(4):
            scores(q_ref, j + u + 1, (u + 1) % 2)
            accumulate(j + u, u % 2)
        return carry

    lax.fori_loop(0, n_kb // 4 - 1, four_blocks, 0)
    j = n_kb - 4
    for u in range(3):
        scores(q_ref, j + u + 1, (u + 1) % 2)
        accumulate(j + u, u % 2)
    scores(qn_ref, 0, 0)
    accumulate(n_kb - 1, 1)
    for pair in range(ATT_GROUP // 2):
        ot = jnp.concatenate(
            [acc_ref[2 * pair + e, 0:ATT_DH, :] / acc_ref[2 * pair + e, ATT_DH:ATT_DH + 1, :]
             for e in range(2)], axis=0)
        o_ref[:, pair * LANES:(pair + 1) * LANES] = ot.T


def _attention(qt, k, vt, batch, seq_len):
    rows = k.shape[1]
    nq = seq_len // TQ
    n_kb = seq_len // TK
    return pl.pallas_call(
        _attn_kernel,
        grid=(batch, ATT_KV, nq),
        in_specs=[
            pl.BlockSpec((1, ATT_DH, ATT_GROUP * TQ), lambda b, g, i: (g, 0, b * nq + i)),
            pl.BlockSpec((1, ATT_DH, ATT_GROUP * TQ), lambda b, g, i: (g, 0, b * nq + jnp.minimum(i + 1, nq - 1))),
            pl.BlockSpec((1, seq_len, ATT_DH), lambda b, g, i: (g, b, 0)),
            pl.BlockSpec((1, n_kb, V_ROWS, TK), lambda b, g, i: (g, b, 0, 0)),
        ],
        out_specs=pl.BlockSpec((TQ, ATT_GROUP * ATT_DH), lambda b, g, i: (b * nq + i, g)),
        out_shape=jax.ShapeDtypeStruct((rows, ATT_QW), F32),
        scratch_shapes=[
            pltpu.VMEM((ATT_GROUP, 1, TQ), F32),
            pltpu.VMEM((ATT_GROUP, V_ROWS, TQ), F32),
            pltpu.VMEM((2, ATT_GROUP, TK, TQ), F32),
            pltpu.VMEM((2, ATT_GROUP, 1, TQ), F32),
        ],
        compiler_params=pltpu.CompilerParams(
            dimension_semantics=("arbitrary", "arbitrary", "arbitrary"), vmem_limit_bytes=VMEM_LIMIT),
        name="attention",
    )(qt, qt, k, vt)


def _rms(y, w):
    ms = jnp.mean(y * y, axis=-1, keepdims=True)
    return y * lax.rsqrt(ms + EPS) * w


def _mix_ffn_kernel(of_ref, ob_ref, sg_ref, att_ref, x_ref, wo_ref, hgw_ref, attw_ref, n2w_ref,
                    wgu_ref, wd_ref, fw_ref, o_ref, *, d_ff, final_norm):
    o = of_ref[...] + ob_ref[...]
    hgw = hgw_ref[...]
    o_hg = jnp.concatenate(
        [_rms(o[:, hh * HG_D:(hh + 1) * HG_D], hgw) for hh in range(HG_HEADS)], axis=1)
    o_hg = (o_hg * sg_ref[...].astype(F32)).astype(BF16)
    o_att = _rms(att_ref[...], attw_ref[...]).astype(BF16)
    y = x_ref[...] + _dot(o_hg, wo_ref[0:HG_W, :]) + _dot(o_att, wo_ref[HG_W:HG_W + ATT_QW, :])
    h = _rms(y, n2w_ref[...]).astype(BF16)
    ffn = None
    for c in range(d_ff // FF_CHUNK):
        lo = c * FF_CHUNK
        gate = _dot(h, wgu_ref[:, lo:lo + FF_CHUNK])
        up = _dot(h, wgu_ref[:, d_ff + lo:d_ff + lo + FF_CHUNK])
        act = (gate * _sigmoid(gate) * up).astype(BF16)
        part = _dot(act, wd_ref[lo:lo + FF_CHUNK, :])
        ffn = part if ffn is None else ffn + part
    z = y + ffn
    o_ref[...] = _rms(z, fw_ref[...]) if final_norm else z


def _mix_ffn(o_f, o_b, sg, att, x2d, wo_bf, hgw, attw, n2w, wgu_bf, wd_bf, fw, final_norm):
    rows, d = x2d.shape
    d_ff = wd_bf.shape[0]
    tm = TM_FFN
    row_blk = lambda i: (i, 0)
    const = lambda i: (0, 0)
    half = pl.BlockSpec((tm, HG_W), row_blk)
    wide = pl.BlockSpec((tm, d), row_blk)
    small = lambda a: pl.BlockSpec(a.shape, const)
    resident = lambda a: pl.BlockSpec(a.shape, const, pipeline_mode=pl.Buffered(1))
    return pl.pallas_call(
        functools.partial(_mix_ffn_kernel, d_ff=d_ff, final_norm=final_norm),
        grid=(rows // tm,),
        in_specs=[half, half, half, half, wide, resident(wo_bf), small(hgw), small(attw), small(n2w),
                  resident(wgu_bf), resident(wd_bf), small(fw)],
        out_specs=wide,
        out_shape=jax.ShapeDtypeStruct((rows, d), F32),
        compiler_params=pltpu.CompilerParams(
            dimension_semantics=("arbitrary",), vmem_limit_bytes=VMEM_LIMIT),
        name="mix_ffn",
    )(o_f, o_b, sg, att, x2d, wo_bf, hgw, attw, n2w, wgu_bf, wd_bf, fw)


def _rope_tables(seq_len):
    rows = seq_len // GRID_W
    row = jnp.repeat(jnp.arange(rows), GRID_W).astype(F32)
    col = jnp.tile(jnp.arange(GRID_W), rows).astype(F32)
    axis_dim = ATT_DH // 2
    freqs = ROPE_THETA ** (-jnp.arange(0, axis_dim, 2, dtype=F32) / axis_dim)
    ang = jnp.concatenate([row[:, None] * freqs, col[:, None] * freqs], axis=-1)
    cos = jnp.repeat(jnp.cos(ang), 2, axis=1)
    sin = jnp.repeat(jnp.sin(ang), 2, axis=1) * jnp.tile(jnp.array([-1.0, 1.0], F32), ATT_DH // 2)
    reps = LANES // ATT_DH
    return jnp.tile(cos, (1, reps)), jnp.tile(sin, (1, reps))


def kernel(x, norm1_w, w_in, lb_logits, hg_norm_w, q_norm_w, k_norm_w, att_norm_w, w_out, norm2_w,
           w_gate_up, w_down, final_norm_w):
    batch, seq_len, d_model = x.shape
    depth = norm1_w.shape[0]
    rows = batch * seq_len
    assert seq_len % (4 * TK) == 0 and seq_len // TK >= 8 and seq_len % TQ == 0
    assert seq_len % T_SCAN == 0 and TK % TM_IN == 0
    assert TM_IN % TQ == 0 and rows % TM_FFN == 0 and w_in.shape[2] == D_IN

    lb_all = jnp.cumsum(jax.nn.softmax(lb_logits.astype(F32), axis=1), axis=1)
    cos_t, sin_t = _rope_tables(seq_len)
    blk = jnp.arange(ATT_QW) // ATT_DH
    mq = jnp.where(blk[:, None] == blk[None, :], 1.0 / ATT_DH, 0.0).astype(BF16)
    t_idx = jnp.arange(T_SCAN)
    same_chunk = (t_idx[:, None] // CHUNK) == (t_idx[None, :] // CHUNK)
    tri = jnp.stack([same_chunk & (t_idx[:, None] >= t_idx[None, :]),
                     same_chunk & (t_idx[None, :] >= t_idx[:, None])]).astype(BF16)
    q_scale = ATT_DH ** -0.5 * math.log2(math.e)

    x2d = x.reshape(rows, d_model)
    row = lambda v: v.astype(F32).reshape(1, -1)
    for l in range(depth):
        hq, gf, kf, gb, kb, iv, sg, qt, k, vt = _inproj(
            x2d, row(norm1_w[l]), w_in[l].astype(BF16), row(lb_all[0, l]), row(lb_all[1, l]),
            row(jnp.tile(q_norm_w[l], ATT_HEADS)) * q_scale, row(jnp.tile(k_norm_w[l], ATT_KV)),
            cos_t, sin_t, mq, seq_len)
        o_f, o_b = _hgrn(hq, gf, kf, gb, kb, iv, tri, batch, seq_len)
        att = _attention(qt, k, vt, batch, seq_len)
        x2d = _mix_ffn(o_f, o_b, sg, att, x2d, w_out[l].astype(BF16),
                       row(hg_norm_w[l]), row(att_norm_w[l]), row(norm2_w[l]),
                       w_gate_up[l].astype(BF16), w_down[l].astype(BF16), row(final_norm_w), l == depth - 1)
    return x2d.reshape(batch, seq_len, d_model)
```

```python
import functools
import math

import jax
import jax.numpy as jnp
from jax import lax
from jax.experimental import pallas as pl
from jax.experimental.pallas import tpu as pltpu

F32 = jnp.float32
BF16 = jnp.bfloat16

EPS = 1e-6
GRID_W = 64
ROPE_THETA = 10000.0

HG_HEADS = 4
HG_D = 128
HG_W = HG_HEADS * HG_D
CHUNK = 64

ATT_HEADS = 8
ATT_KV = 2
ATT_GROUP = ATT_HEADS // ATT_KV
ATT_DH = 64
ATT_QW = ATT_HEADS * ATT_DH
ATT_KVW = ATT_KV * ATT_DH
V_ROWS = ATT_DH + 16

C_Q, C_FF, C_FB, C_I, C_G = 0, HG_W, 2 * HG_W, 3 * HG_W, 4 * HG_W
C_AQ = 5 * HG_W
C_AK = C_AQ + ATT_QW
C_AV = C_AK + ATT_KVW
D_IN = C_AV + ATT_KVW

LANES = 128
VMEM_LIMIT = 56 * 1024 * 1024

TM_IN = 512
TQ = 256
TK = 512
T_SCAN = 256
TM_FFN = 512
FF_CHUNK = 256


def _sigmoid(u):
    return 1.0 / (1.0 + jnp.exp(-u))


def _dot(a, b):
    return jnp.dot(a, b, preferred_element_type=F32)


def _dot_nt(a, b):
    return lax.dot_general(a, b, (((1,), (1,)), ((), ())), preferred_element_type=F32)


def _dot_tn(a, b):
    return lax.dot_general(a, b, (((0,), (0,)), ((), ())), preferred_element_type=F32)


def _inproj_kernel(x_ref, n1w_ref, w_ref, lbf_ref, lbb_ref, wq_ref, wk_ref, cos_ref, sin_ref, mq_ref,
                   hq_ref, gf_ref, kf_ref, gb_ref, kb_ref, iv_ref, sg_ref, qt_ref, k_ref, vt_ref):
    tm = x_ref.shape[0]
    x = x_ref[...]
    ms = jnp.mean(x * x, axis=-1, keepdims=True)
    h = (x * lax.rsqrt(ms + EPS) * n1w_ref[...]).astype(BF16)

    def proj(lo, hi):
        return _dot(h, w_ref[:, lo:hi])

    def gates(z, lb, g_ref, k_out_ref):
        e = jnp.exp(-jnp.abs(z))
        r = 1.0 / (1.0 + e)
        er = e * r
        pos = z >= 0.0
        s = jnp.where(pos, r, er)
        sn = jnp.where(pos, er, r)
        g_ref[...] = jnp.log2(lb + (1.0 - lb) * s)
        k_out_ref[...] = ((1.0 - lb) * sn).astype(BF16)

    cos = cos_ref[...]
    sin = sin_ref[...]
    lane = lax.broadcasted_iota(jnp.int32, (tm, LANES), 1)
    even = (lane & 1) == 0

    def rope(xc):
        partner = jnp.where(even, pltpu.roll(xc, LANES - 1, 1), pltpu.roll(xc, 1, 1))
        return xc * cos + partner * sin

    def head_rms(a, m, w):
        return a * lax.rsqrt(_dot((a * a).astype(BF16), m) + EPS) * w

    def ep_aq(aq):
        yq = head_rms(aq, mq_ref[...], wq_ref[...])
        for c in range(ATT_QW // LANES):
            qct = rope(yq[:, c * LANES:(c + 1) * LANES]).T.astype(BF16)
            grp, pair = divmod(c, 2)
            for e in range(2):
                hl = 2 * pair + e
                for blk in range(tm // TQ):
                    col = (blk * ATT_GROUP + hl) * TQ
                    qt_ref[grp, :, col:col + TQ] = qct[e * ATT_DH:(e + 1) * ATT_DH, blk * TQ:(blk + 1) * TQ]

    def ep_ak(ak):
        yk = rope(head_rms(ak, mq_ref[0:LANES, 0:LANES], wk_ref[...])).astype(BF16)
        k_ref[0] = yk[:, 0:ATT_DH]
        k_ref[1] = yk[:, ATT_DH:2 * ATT_DH]

    def ep_av(av):
        avt = av.T.astype(BF16)
        ones = jnp.ones((V_ROWS - ATT_DH, tm), BF16)
        for g in range(ATT_KV):
            vt_ref[g, 0, 0:ATT_DH, :] = avt[g * ATT_DH:(g + 1) * ATT_DH, :]
            vt_ref[g, 0, ATT_DH:V_ROWS, :] = ones

    def ep_ff(u):
        gates(u, lbf_ref[...], gf_ref, kf_ref)

    def ep_fb(u):
        gates(u, lbb_ref[...], gb_ref, kb_ref)

    def ep_q(u):
        hq_ref[...] = (u * _sigmoid(u)).astype(BF16)

    def ep_g(u):
        sg_ref[...] = (u * _sigmoid(u)).astype(BF16)

    def ep_i(u):
        iv_ref[...] = u.astype(BF16)

    stages = ((C_AQ, C_AK, ep_aq), (C_AK, C_AV, ep_ak), (C_AV, D_IN, ep_av), (C_FF, C_FB, ep_ff),
              (C_FB, C_I, ep_fb), (C_Q, C_FF, ep_q), (C_G, C_AQ, ep_g), (C_I, C_G, ep_i))
    pending = None
    for lo, hi, epilogue in stages:
        u = proj(lo, hi)
        if pending is not None:
            pending[0](pending[1])
        pending = (epilogue, u)
    pending[0](pending[1])


def _inproj(x2d, n1w, w_bf, lbf, lbb, wq, wk, cos_t, sin_t, mq, seq_len):
    rows = x2d.shape[0]
    tm = TM_IN
    n_tiles = rows // tm
    tiles_per_seq = seq_len // tm
    per_tk = TK // tm
    row_blk = lambda i: (i, 0)
    const = lambda i: (0, 0)
    full = lambda shape: pl.BlockSpec(shape, const)
    act_bf = jax.ShapeDtypeStruct((rows, HG_W), BF16)
    act_f32 = jax.ShapeDtypeStruct((rows, HG_W), F32)
    act_spec = pl.BlockSpec((tm, HG_W), row_blk)
    return pl.pallas_call(
        _inproj_kernel,
        grid=(n_tiles,),
        in_specs=[
            pl.BlockSpec((tm, x2d.shape[1]), row_blk),
            full(n1w.shape), full(w_bf.shape), full(lbf.shape), full(lbb.shape),
            full(wq.shape), full(wk.shape),
            pl.BlockSpec((tm, LANES), lambda i: (i % tiles_per_seq, 0)),
            pl.BlockSpec((tm, LANES), lambda i: (i % tiles_per_seq, 0)),
            full(mq.shape),
        ],
        out_specs=[
            act_spec, act_spec, act_spec, act_spec, act_spec, act_spec, act_spec,
            pl.BlockSpec((ATT_KV, ATT_DH, ATT_GROUP * tm), lambda i: (0, 0, i)),
            pl.BlockSpec((ATT_KV, tm, ATT_DH), lambda i: (0, i, 0)),
            pl.BlockSpec((ATT_KV, 1, V_ROWS, tm), lambda i: (0, i // per_tk, 0, i % per_tk)),
        ],
        out_shape=[
            act_bf, act_f32, act_bf, act_f32, act_bf, act_bf, act_bf,
            jax.ShapeDtypeStruct((ATT_KV, ATT_DH, ATT_GROUP * rows), BF16),
            jax.ShapeDtypeStruct((ATT_KV, rows, ATT_DH), BF16),
            jax.ShapeDtypeStruct((ATT_KV, rows // TK, V_ROWS, TK), BF16),
        ],
        compiler_params=pltpu.CompilerParams(
            dimension_semantics=("arbitrary",), vmem_limit_bytes=VMEM_LIMIT),
        name="inproj",
    )(x2d, n1w, w_bf, lbf, lbb, wq, wk, cos_t, sin_t, mq)


def _hgrn_kernel(qf_ref, gf_ref, kf_ref, vf_ref, qb_ref, gb_ref, kb_ref, vb_ref, tri_ref,
                 of_ref, ob_ref, st_ref):
    @pl.when(pl.program_id(1) == 0)
    def _():
        st_ref[...] = jnp.zeros_like(st_ref)

    n_chunks = qf_ref.shape[0] // CHUNK
    row = lax.broadcasted_iota(jnp.int32, (CHUNK, CHUNK), 0)
    col = lax.broadcasted_iota(jnp.int32, (CHUNK, CHUNK), 1)
    plans = (
        (qf_ref, gf_ref, kf_ref, vf_ref, of_ref, row >= col, CHUNK // 2 - 1, CHUNK - 1, range(n_chunks)),
        (qb_ref, gb_ref, kb_ref, vb_ref, ob_ref, col >= row, CHUNK // 2, 0, range(n_chunks - 1, -1, -1)),
    )

    def sub(a, c, hh):
        return a[c * CHUNK:(c + 1) * CHUNK, hh * HG_D:(hh + 1) * HG_D]

    decayed = []
    for d, (q_ref, g_ref, k_ref, v_ref, o_ref, keep, ref_row, end_row, order) in enumerate(plans):
        b = _split_dot_lhs(tri_ref[d], g_ref[...])

        def chunk_rows(r, b=b):
            return jnp.concatenate(
                [jnp.broadcast_to(b[c * CHUNK + r:c * CHUNK + r + 1, :], (CHUNK, HG_W))
                 for c in range(n_chunks)], axis=0)

        b_ref = chunk_rows(ref_row)
        b_end = chunk_rows(end_row)
        q = q_ref[...].astype(F32)
        k = k_ref[...].astype(F32)
        decayed.append((
            b,
            (q * jnp.exp2(b - b_ref)).astype(BF16),
            (k * jnp.exp2(b_ref - b)).astype(BF16),
            (q * jnp.exp2(b)).astype(BF16),
            (k * jnp.exp2(b_end - b)).astype(BF16),
        ))
    local = []
    for d, (q_ref, g_ref, k_ref, v_ref, o_ref, keep, ref_row, end_row, order) in enumerate(plans):
        _, q_in, k_in, _, k_dec = decayed[d]
        units = [(c, hh) for c in order for hh in range(HG_HEADS)]
        v = v_ref[...]
        scores = {u: jnp.where(keep, _dot_nt(sub(q_in, *u), sub(k_in, *u)), 0.0).astype(BF16) for u in units}
        o_intra = {u: _dot(scores[u], sub(v, *u)) for u in units}
        contrib = {u: _dot_tn(sub(v, *u), sub(k_dec, *u)) for u in units}
        local.append((units, o_intra, contrib))
    for i in range(n_chunks * HG_HEADS):
        for d, (q_ref, g_ref, k_ref, v_ref, o_ref, keep, ref_row, end_row, order) in enumerate(plans):
            b, _, _, q_dec, _ = decayed[d]
            units, o_intra, contrib = local[d]
            c, hh = units[i]
            st = st_ref[d * HG_HEADS + hh]
            o_ref[c * CHUNK:(c + 1) * CHUNK, hh * HG_D:(hh + 1) * HG_D] = (
                o_intra[(c, hh)] + _dot_nt(sub(q_dec, c, hh), st.astype(BF16)))
            r_end = c * CHUNK + end_row
            decay = jnp.exp2(b[r_end:r_end + 1, hh * HG_D:(hh + 1) * HG_D])
            st_ref[d * HG_HEADS + hh] = st * decay + contrib[(c, hh)]


def _split_dot_lhs(m, a):
    hi = a.astype(BF16)
    lo = (a - hi.astype(F32)).astype(BF16)
    return _dot(m, hi) + _dot(m, lo)


def _hgrn(hq, gf, kf, gb, kb, iv, tri, batch, seq_len):
    rows = hq.shape[0]
    t = T_SCAN
    nb = seq_len // t
    fwd = lambda b, n: (b * nb + n, 0)
    bwd = lambda b, n: (b * nb + nb - 1 - n, 0)
    sf = pl.BlockSpec((t, HG_W), fwd)
    sb = pl.BlockSpec((t, HG_W), bwd)
    out = jax.ShapeDtypeStruct((rows, HG_W), F32)
    return pl.pallas_call(
        _hgrn_kernel,
        grid=(batch, nb),
        in_specs=[sf, sf, sf, sf, sb, sb, sb, sb, pl.BlockSpec(tri.shape, lambda b, n: (0, 0, 0))],
        out_specs=[sf, sb],
        out_shape=[out, out],
        scratch_shapes=[pltpu.VMEM((2 * HG_HEADS, HG_D, HG_D), F32)],
        compiler_params=pltpu.CompilerParams(
            dimension_semantics=("arbitrary", "arbitrary"), vmem_limit_bytes=VMEM_LIMIT),
        name="hgrn_scan",
    )(hq, gf, kf, iv, hq, gb, kb, iv, tri)


def _attn_kernel(q_ref, qn_ref, k_ref, v_ref, o_ref, m_ref, acc_ref, s_ref, bm_ref):
    tq = o_ref.shape[0]
    tk = v_ref.shape[3]
    n_kb = v_ref.shape[1]

    def scores(qsrc_ref, j, hh):
        s = _dot(k_ref[0, j * tk:(j + 1) * tk, :],
                 qsrc_ref[0, :, hh * tq:(hh + 1) * tq])
        s_ref[j % 2, hh] = s
        bm_ref[j % 2, hh] = jnp.max(s, axis=0, keepdims=True)

    def accumulate(j, hh):
        m_old = m_ref[hh]
        m_new = jnp.maximum(m_old, bm_ref[j % 2, hh])
        p = jnp.exp2(s_ref[j % 2, hh] - m_new).astype(BF16)
        acc_ref[hh] = jnp.exp2(m_old - m_new) * acc_ref[hh] + _dot(v_ref[0, j], p)
        m_ref[hh] = m_new

    @pl.when(pl.program_id(2) == 0)
    def _():
        for hh in range(ATT_GROUP):
            scores(q_ref, 0, hh)

    m_ref[...] = jnp.full(m_ref.shape, -1e30, F32)
    acc_ref[...] = jnp.zeros_like(acc_ref)
    for j in range(n_kb):
        for hh in range(ATT_GROUP):
            if j + 1 < n_kb:
                scores(q_ref, j + 1, hh)
            else:
                scores(qn_ref, 0, hh)
            accumulate(j, hh)
    for pair in range(ATT_GROUP // 2):
        ot = jnp.concatenate(
            [acc_ref[2 * pair + e, 0:ATT_DH, :] / acc_ref[2 * pair + e, ATT_DH:ATT_DH + 1, :]
             for e in range(2)], axis=0)
        o_ref[:, pair * LANES:(pair + 1) * LANES] = ot.T


def _attention(qt, k, vt, batch, seq_len):
    rows = k.shape[1]
    nq = seq_len // TQ
    n_kb = seq_len // TK
    return pl.pallas_call(
        _attn_kernel,
        grid=(batch, ATT_KV, nq),
        in_specs=[
            pl.BlockSpec((1, ATT_DH, ATT_GROUP * TQ), lambda b, g, i: (g, 0, b * nq + i)),
            pl.BlockSpec((1, ATT_DH, ATT_GROUP * TQ), lambda b, g, i: (g, 0, b * nq + jnp.minimum(i + 1, nq - 1))),
            pl.BlockSpec((1, seq_len, ATT_DH), lambda b, g, i: (g, b, 0)),
            pl.BlockSpec((1, n_kb, V_ROWS, TK), lambda b, g, i: (g, b, 0, 0)),
        ],
        out_specs=pl.BlockSpec((TQ, ATT_GROUP * ATT_DH), lambda b, g, i: (b * nq + i, g)),
        out_shape=jax.ShapeDtypeStruct((rows, ATT_QW), F32),
        scratch_shapes=[
            pltpu.VMEM((ATT_GROUP, 1, TQ), F32),
            pltpu.VMEM((ATT_GROUP, V_ROWS, TQ), F32),
            pltpu.VMEM((2, ATT_GROUP, TK, TQ), F32),
            pltpu.VMEM((2, ATT_GROUP, 1, TQ), F32),
        ],
        compiler_params=pltpu.CompilerParams(
            dimension_semantics=("arbitrary", "arbitrary", "arbitrary"), vmem_limit_bytes=VMEM_LIMIT),
        name="attention",
    )(qt, qt, k, vt)


def _rms(y, w):
    ms = jnp.mean(y * y, axis=-1, keepdims=True)
    return y * lax.rsqrt(ms + EPS) * w


def _mix_ffn_kernel(of_ref, ob_ref, sg_ref, att_ref, x_ref, wo_ref, hgw_ref, attw_ref, n2w_ref,
                    wgu_ref, wd_ref, fw_ref, o_ref, *, d_ff, final_norm):
    o = of_ref[...] + ob_ref[...]
    hgw = hgw_ref[...]
    o_hg = jnp.concatenate(
        [_rms(o[:, hh * HG_D:(hh + 1) * HG_D], hgw) for hh in range(HG_HEADS)], axis=1)
    o_hg = (o_hg * sg_ref[...].astype(F32)).astype(BF16)
    o_att = _rms(att_ref[...], attw_ref[...]).astype(BF16)
    y = x_ref[...] + _dot(o_hg, wo_ref[0:HG_W, :]) + _dot(o_att, wo_ref[HG_W:HG_W + ATT_QW, :])
    h = _rms(y, n2w_ref[...]).astype(BF16)
    ffn = None
    for c in range(d_ff // FF_CHUNK):
        lo = c * FF_CHUNK
        gate = _dot(h, wgu_ref[:, lo:lo + FF_CHUNK])
        up = _dot(h, wgu_ref[:, d_ff + lo:d_ff + lo + FF_CHUNK])
        act = (gate * _sigmoid(gate) * up).astype(BF16)
        part = _dot(act, wd_ref[lo:lo + FF_CHUNK, :])
        ffn = part if ffn is None else ffn + part
    z = y + ffn
    o_ref[...] = _rms(z, fw_ref[...]) if final_norm else z


def _mix_ffn(o_f, o_b, sg, att, x2d, wo_bf, hgw, attw, n2w, wgu_bf, wd_bf, fw, final_norm):
    rows, d = x2d.shape
    d_ff = wd_bf.shape[0]
    tm = TM_FFN
    row_blk = lambda i: (i, 0)
    const = lambda i: (0, 0)
    half = pl.BlockSpec((tm, HG_W), row_blk)
    wide = pl.BlockSpec((tm, d), row_blk)
    small = lambda a: pl.BlockSpec(a.shape, const)
    resident = lambda a: pl.BlockSpec(a.shape, const, pipeline_mode=pl.Buffered(1))
    return pl.pallas_call(
        functools.partial(_mix_ffn_kernel, d_ff=d_ff, final_norm=final_norm),
        grid=(rows // tm,),
        in_specs=[half, half, half, half, wide, resident(wo_bf), small(hgw), small(attw), small(n2w),
                  resident(wgu_bf), resident(wd_bf), small(fw)],
        out_specs=wide,
        out_shape=jax.ShapeDtypeStruct((rows, d), F32),
        compiler_params=pltpu.CompilerParams(
            dimension_semantics=("arbitrary",), vmem_limit_bytes=VMEM_LIMIT),
        name="mix_ffn",
    )(o_f, o_b, sg, att, x2d, wo_bf, hgw, attw, n2w, wgu_bf, wd_bf, fw)


def _rope_tables(seq_len):
    rows = seq_len // GRID_W
    row = jnp.repeat(jnp.arange(rows), GRID_W).astype(F32)
    col = jnp.tile(jnp.arange(GRID_W), rows).astype(F32)
    axis_dim = ATT_DH // 2
    freqs = ROPE_THETA ** (-jnp.arange(0, axis_dim, 2, dtype=F32) / axis_dim)
    ang = jnp.concatenate([row[:, None] * freqs, col[:, None] * freqs], axis=-1)
    cos = jnp.repeat(jnp.cos(ang), 2, axis=1)
    sin = jnp.repeat(jnp.sin(ang), 2, axis=1) * jnp.tile(jnp.array([-1.0, 1.0], F32), ATT_DH // 2)
    reps = LANES // ATT_DH
    return jnp.tile(cos, (1, reps)), jnp.tile(sin, (1, reps))


def kernel(x, norm1_w, w_in, lb_logits, hg_norm_w, q_norm_w, k_norm_w, att_norm_w, w_out, norm2_w,
           w_gate_up, w_down, final_norm_w):
    batch, seq_len, d_model = x.shape
    depth = norm1_w.shape[0]
    rows = batch * seq_len
    assert seq_len % (2 * TK) == 0 and seq_len % TQ == 0
    assert seq_len % T_SCAN == 0 and TK % TM_IN == 0
    assert TM_IN % TQ == 0 and rows % TM_FFN == 0 and w_in.shape[2] == D_IN

    lb_all = jnp.cumsum(jax.nn.softmax(lb_logits.astype(F32), axis=1), axis=1)
    cos_t, sin_t = _rope_tables(seq_len)
    blk = jnp.arange(ATT_QW) // ATT_DH
    mq = jnp.where(blk[:, None] == blk[None, :], 1.0 / ATT_DH, 0.0).astype(BF16)
    t_idx = jnp.arange(T_SCAN)
    same_chunk = (t_idx[:, None] // CHUNK) == (t_idx[None, :] // CHUNK)
    tri = jnp.stack([same_chunk & (t_idx[:, None] >= t_idx[None, :]),
                     same_chunk & (t_idx[None, :] >= t_idx[:, None])]).astype(BF16)
    q_scale = ATT_DH ** -0.5 * math.log2(math.e)

    x2d = x.reshape(rows, d_model)
    row = lambda v: v.astype(F32).reshape(1, -1)
    for l in range(depth):
        hq, gf, kf, gb, kb, iv, sg, qt, k, vt = _inproj(
            x2d, row(norm1_w[l]), w_in[l].astype(BF16), row(lb_all[0, l]), row(lb_all[1, l]),
            row(jnp.tile(q_norm_w[l], ATT_HEADS)) * q_scale, row(jnp.tile(k_norm_w[l], ATT_KV)),
            cos_t, sin_t, mq, seq_len)
        o_f, o_b = _hgrn(hq, gf, kf, gb, kb, iv, tri, batch, seq_len)
        att = _attention(qt, k, vt, batch, seq_len)
        x2d = _mix_ffn(o_f, o_b, sg, att, x2d, w_out[l].astype(BF16),
                       row(hg_norm_w[l]), row(att_norm_w[l]), row(norm2_w[l]),
                       w_gate_up[l].astype(BF16), w_down[l].astype(BF16), row(final_norm_w), l == depth - 1)
    return x2d.reshape(batch, seq_len, d_model)
```

```python
import functools
import math

import jax
import jax.numpy as jnp
from jax import lax
from jax.experimental import pallas as pl
from jax.experimental.pallas import tpu as pltpu

F32 = jnp.float32
BF16 = jnp.bfloat16

EPS = 1e-6
GRID_W = 64
ROPE_THETA = 10000.0

HG_HEADS = 4
HG_D = 128
HG_W = HG_HEADS * HG_D
CHUNK = 64

ATT_HEADS = 8
ATT_KV = 2
ATT_GROUP = ATT_HEADS // ATT_KV
ATT_DH = 64
ATT_QW = ATT_HEADS * ATT_DH
ATT_KVW = ATT_KV * ATT_DH
V_ROWS = ATT_DH + 16

C_Q, C_FF, C_FB, C_I, C_G = 0, HG_W, 2 * HG_W, 3 * HG_W, 4 * HG_W
C_AQ = 5 * HG_W
C_AK = C_AQ + ATT_QW
C_AV = C_AK + ATT_KVW
D_IN = C_AV + ATT_KVW

LANES = 128
VMEM_LIMIT = 56 * 1024 * 1024

TM_IN = 512
TQ = 256
TK = 512
T_SCAN = 256
TM_FFN = 512
FF_CHUNK = 256
PIECE = 256


def _silu(u):
    hu = 0.5 * u
    return hu + hu * jnp.tanh(hu)


def _dot(a, b):
    return jnp.dot(a, b, preferred_element_type=F32)


def _dot_nt(a, b):
    return lax.dot_general(a, b, (((1,), (1,)), ((), ())), preferred_element_type=F32)


def _dot_tn(a, b):
    return lax.dot_general(a, b, (((0,), (0,)), ((), ())), preferred_element_type=F32)


def _inproj_kernel(x_ref, n1w_ref, w_ref, lbf_ref, lbb_ref, wq_ref, wk_ref, cos_ref, sin_ref, mq_ref,
                   hq_ref, gf_ref, kf_ref, gb_ref, kb_ref, iv_ref, sg_ref, qt_ref, k_ref, vt_ref):
    tm = x_ref.shape[0]
    x = x_ref[...]
    ms = jnp.mean(x * x, axis=-1, keepdims=True)
    h = (x * lax.rsqrt(ms + EPS) * n1w_ref[...]).astype(BF16)

    def proj(lo, hi):
        return _dot(h, w_ref[:, lo:hi])

    def gates(z, lb, g_ref, k_out_ref, cs):
        lb = lb[:, cs]
        hs = 0.5 * (1.0 - lb)
        ht = hs * jnp.tanh(0.5 * z)
        g_ref[:, cs] = jnp.log2((lb + hs) + ht)
        k_out_ref[:, cs] = (hs - ht).astype(BF16)

    cos = cos_ref[...]
    sin = sin_ref[...]
    lane = lax.broadcasted_iota(jnp.int32, (tm, LANES), 1)
    even = (lane & 1) == 0

    def rope(xc):
        partner = jnp.where(even, pltpu.roll(xc, LANES - 1, 1), pltpu.roll(xc, 1, 1))
        return xc * cos + partner * sin

    def head_rms(a, m, w):
        return a * lax.rsqrt(_dot((a * a).astype(BF16), m) + EPS) * w

    def ep_aq(aq, piece, cs):
        yq = head_rms(aq, mq_ref[0:PIECE, 0:PIECE], wq_ref[:, cs])
        for pair in range(PIECE // LANES):
            qct = rope(yq[:, pair * LANES:(pair + 1) * LANES]).T.astype(BF16)
            for e in range(2):
                hl = 2 * pair + e
                for blk in range(tm // TQ):
                    col = (blk * ATT_GROUP + hl) * TQ
                    qt_ref[piece, :, col:col + TQ] = qct[e * ATT_DH:(e + 1) * ATT_DH, blk * TQ:(blk + 1) * TQ]

    def ep_ak(ak, piece, cs):
        yk = rope(head_rms(ak, mq_ref[0:LANES, 0:LANES], wk_ref[...])).astype(BF16)
        k_ref[0] = yk[:, 0:ATT_DH]
        k_ref[1] = yk[:, ATT_DH:2 * ATT_DH]

    def ep_av(av, piece, cs):
        avt = av.T.astype(BF16)
        ones = jnp.ones((V_ROWS - ATT_DH, tm), BF16)
        for g in range(ATT_KV):
            vt_ref[g, 0, 0:ATT_DH, :] = avt[g * ATT_DH:(g + 1) * ATT_DH, :]
            vt_ref[g, 0, ATT_DH:V_ROWS, :] = ones

    def ep_ff(u, piece, cs):
        gates(u, lbf_ref[...], gf_ref, kf_ref, cs)

    def ep_fb(u, piece, cs):
        gates(u, lbb_ref[...], gb_ref, kb_ref, cs)

    def ep_q(u, piece, cs):
        hq_ref[:, cs] = _silu(u).astype(BF16)

    def ep_g(u, piece, cs):
        sg_ref[:, cs] = _silu(u).astype(BF16)

    def ep_i(u, piece, cs):
        iv_ref[:, cs] = u.astype(BF16)

    halves = lambda base, ep: [(base + p * PIECE, PIECE, ep, p) for p in range(HG_W // PIECE)]
    stages = (halves(C_AQ, ep_aq) + [(C_AK, ATT_KVW, ep_ak, 0), (C_AV, ATT_KVW, ep_av, 0)]
              + halves(C_FF, ep_ff) + halves(C_FB, ep_fb) + halves(C_Q, ep_q) + halves(C_G, ep_g)
              + halves(C_I, ep_i))
    pending = None
    for lo, width, epilogue, piece in stages:
        u = proj(lo, lo + width)
        if pending is not None:
            pending()
        pending = functools.partial(epilogue, u, piece, slice(piece * PIECE, (piece + 1) * PIECE))
    pending()


def _inproj(x2d, n1w, w_bf, lbf, lbb, wq, wk, cos_t, sin_t, mq, seq_len):
    rows = x2d.shape[0]
    tm = TM_IN
    n_tiles = rows // tm
    tiles_per_seq = seq_len // tm
    per_tk = TK // tm
    row_blk = lambda i: (i, 0)
    const = lambda i: (0, 0)
    full = lambda shape: pl.BlockSpec(shape, const)
    act_bf = jax.ShapeDtypeStruct((rows, HG_W), BF16)
    act_f32 = jax.ShapeDtypeStruct((rows, HG_W), F32)
    act_spec = pl.BlockSpec((tm, HG_W), row_blk)
    return pl.pallas_call(
        _inproj_kernel,
        grid=(n_tiles,),
        in_specs=[
            pl.BlockSpec((tm, x2d.shape[1]), row_blk),
            full(n1w.shape), full(w_bf.shape), full(lbf.shape), full(lbb.shape),
            full(wq.shape), full(wk.shape),
            pl.BlockSpec((tm, LANES), lambda i: (i % tiles_per_seq, 0)),
            pl.BlockSpec((tm, LANES), lambda i: (i % tiles_per_seq, 0)),
            full(mq.shape),
        ],
        out_specs=[
            act_spec, act_spec, act_spec, act_spec, act_spec, act_spec, act_spec,
            pl.BlockSpec((ATT_KV, ATT_DH, ATT_GROUP * tm), lambda i: (0, 0, i)),
            pl.BlockSpec((ATT_KV, tm, ATT_DH), lambda i: (0, i, 0)),
            pl.BlockSpec((ATT_KV, 1, V_ROWS, tm), lambda i: (0, i // per_tk, 0, i % per_tk)),
        ],
        out_shape=[
            act_bf, act_f32, act_bf, act_f32, act_bf, act_bf, act_bf,
            jax.ShapeDtypeStruct((ATT_KV, ATT_DH, ATT_GROUP * rows), BF16),
            jax.ShapeDtypeStruct((ATT_KV, rows, ATT_DH), BF16),
            jax.ShapeDtypeStruct((ATT_KV, rows // TK, V_ROWS, TK), BF16),
        ],
        compiler_params=pltpu.CompilerParams(
            dimension_semantics=("arbitrary",), vmem_limit_bytes=VMEM_LIMIT),
        name="inproj",
    )(x2d, n1w, w_bf, lbf, lbb, wq, wk, cos_t, sin_t, mq)


def _hgrn_kernel(qf_ref, gf_ref, kf_ref, vf_ref, qb_ref, gb_ref, kb_ref, vb_ref, tri_ref,
                 of_ref, ob_ref, st_ref):
    @pl.when(pl.program_id(1) == 0)
    def _():
        st_ref[...] = jnp.zeros_like(st_ref)

    n_chunks = qf_ref.shape[0] // CHUNK
    row = lax.broadcasted_iota(jnp.int32, (CHUNK, CHUNK), 0)
    col = lax.broadcasted_iota(jnp.int32, (CHUNK, CHUNK), 1)
    plans = (
        (qf_ref, gf_ref, kf_ref, vf_ref, of_ref, row >= col, CHUNK // 2 - 1, CHUNK - 1, range(n_chunks)),
        (qb_ref, gb_ref, kb_ref, vb_ref, ob_ref, col >= row, CHUNK // 2, 0, range(n_chunks - 1, -1, -1)),
    )

    def sub(a, c, hh):
        return a[c * CHUNK:(c + 1) * CHUNK, hh * HG_D:(hh + 1) * HG_D]

    decayed = []
    for d, (q_ref, g_ref, k_ref, v_ref, o_ref, keep, ref_row, end_row, order) in enumerate(plans):
        b = _split_dot_lhs(tri_ref[d], g_ref[...])

        def chunk_rows(r, b=b):
            return jnp.concatenate(
                [jnp.broadcast_to(b[c * CHUNK + r:c * CHUNK + r + 1, :], (CHUNK, HG_W))
                 for c in range(n_chunks)], axis=0)

        b_ref = chunk_rows(ref_row)
        b_end = chunk_rows(end_row)
        q = q_ref[...].astype(F32)
        k = k_ref[...].astype(F32)
        decayed.append((
            b,
            (q * jnp.exp2(b - b_ref)).astype(BF16),
            (k * jnp.exp2(b_ref - b)).astype(BF16),
            (q * jnp.exp2(b)).astype(BF16),
            (k * jnp.exp2(b_end - b)).astype(BF16),
        ))
    local = []
    for d, (q_ref, g_ref, k_ref, v_ref, o_ref, keep, ref_row, end_row, order) in enumerate(plans):
        _, q_in, k_in, _, k_dec = decayed[d]
        units = [(c, hh) for c in order for hh in range(HG_HEADS)]
        v = v_ref[...]
        scores = {u: jnp.where(keep, _dot_nt(sub(q_in, *u), sub(k_in, *u)), 0.0).astype(BF16) for u in units}
        o_intra = {u: _dot(scores[u], sub(v, *u)) for u in units}
        contrib = {u: _dot_tn(sub(v, *u), sub(k_dec, *u)) for u in units}
        local.append((units, o_intra, contrib))
    for i in range(n_chunks * HG_HEADS):
        for d, (q_ref, g_ref, k_ref, v_ref, o_ref, keep, ref_row, end_row, order) in enumerate(plans):
            b, _, _, q_dec, _ = decayed[d]
            units, o_intra, contrib = local[d]
            c, hh = units[i]
            st = st_ref[d * HG_HEADS + hh]
            o_ref[c * CHUNK:(c + 1) * CHUNK, hh * HG_D:(hh + 1) * HG_D] = (
                o_intra[(c, hh)] + _dot_nt(sub(q_dec, c, hh), st.astype(BF16)))
            r_end = c * CHUNK + end_row
            decay = jnp.exp2(b[r_end:r_end + 1, hh * HG_D:(hh + 1) * HG_D])
            st_ref[d * HG_HEADS + hh] = st * decay + contrib[(c, hh)]


def _split_dot_lhs(m, a):
    hi = a.astype(BF16)
    lo = (a - hi.astype(F32)).astype(BF16)
    return _dot(m, hi) + _dot(m, lo)


def _hgrn(hq, gf, kf, gb, kb, iv, tri, batch, seq_len):
    rows = hq.shape[0]
    t = T_SCAN
    nb = seq_len // t
    fwd = lambda b, n: (b * nb + n, 0)
    bwd = lambda b, n: (b * nb + nb - 1 - n, 0)
    sf = pl.BlockSpec((t, HG_W), fwd)
    sb = pl.BlockSpec((t, HG_W), bwd)
    out = jax.ShapeDtypeStruct((rows, HG_W), F32)
    return pl.pallas_call(
        _hgrn_kernel,
        grid=(batch, nb),
        in_specs=[sf, sf, sf, sf, sb, sb, sb, sb, pl.BlockSpec(tri.shape, lambda b, n: (0, 0, 0))],
        out_specs=[sf, sb],
        out_shape=[out, out],
        scratch_shapes=[pltpu.VMEM((2 * HG_HEADS, HG_D, HG_D), F32)],
        compiler_params=pltpu.CompilerParams(
            dimension_semantics=("arbitrary", "arbitrary"), vmem_limit_bytes=VMEM_LIMIT),
        name="hgrn_scan",
    )(hq, gf, kf, iv, hq, gb, kb, iv, tri)


def _attn_kernel(q_ref, qn_ref, k_ref, v_ref, o_ref, m_ref, acc_ref, s_ref, bm_ref):
    tq = o_ref.shape[0]
    tk = v_ref.shape[3]
    n_kb = v_ref.shape[1]

    def scores(qsrc_ref, j, hh):
        s = _dot(k_ref[0, j * tk:(j + 1) * tk, :],
                 qsrc_ref[0, :, hh * tq:(hh + 1) * tq])
        s_ref[j % 2, hh] = s
        bm_ref[j % 2, hh] = jnp.max(s, axis=0, keepdims=True)

    def accumulate(j, hh):
        m_old = m_ref[hh]
        m_new = jnp.maximum(m_old, bm_ref[j % 2, hh])
        p = jnp.exp2(s_ref[j % 2, hh] - m_new).astype(BF16)
        acc_ref[hh] = jnp.exp2(m_old - m_new) * acc_ref[hh] + _dot(v_ref[0, j], p)
        m_ref[hh] = m_new

    @pl.when(pl.program_id(2) == 0)
    def _():
        for hh in range(ATT_GROUP):
            scores(q_ref, 0, hh)

    m_ref[...] = jnp.full(m_ref.shape, -1e30, F32)
    acc_ref[...] = jnp.zeros_like(acc_ref)
    for j in range(n_kb):
        for hh in range(ATT_GROUP):
            if j + 1 < n_kb:
                scores(q_ref, j + 1, hh)
            else:
                scores(qn_ref, 0, hh)
            accumulate(j, hh)
    for pair in range(ATT_GROUP // 2):
        ot = jnp.concatenate(
            [acc_ref[2 * pair + e, 0:ATT_DH, :] / acc_ref[2 * pair + e, ATT_DH:ATT_DH + 1, :]
             for e in range(2)], axis=0)
        o_ref[:, pair * LANES:(pair + 1) * LANES] = ot.T


def _attention(qt, k, vt, batch, seq_len):
    rows = k.shape[1]
    nq = seq_len // TQ
    n_kb = seq_len // TK
    return pl.pallas_call(
        _attn_kernel,
        grid=(batch, ATT_KV, nq),
        in_specs=[
            pl.BlockSpec((1, ATT_DH, ATT_GROUP * TQ), lambda b, g, i: (g, 0, b * nq + i)),
            pl.BlockSpec((1, ATT_DH, ATT_GROUP * TQ), lambda b, g, i: (g, 0, b * nq + jnp.minimum(i + 1, nq - 1))),
            pl.BlockSpec((1, seq_len, ATT_DH), lambda b, g, i: (g, b, 0)),
            pl.BlockSpec((1, n_kb, V_ROWS, TK), lambda b, g, i: (g, b, 0, 0)),
        ],
        out_specs=pl.BlockSpec((TQ, ATT_GROUP * ATT_DH), lambda b, g, i: (b * nq + i, g)),
        out_shape=jax.ShapeDtypeStruct((rows, ATT_QW), F32),
        scratch_shapes=[
            pltpu.VMEM((ATT_GROUP, 1, TQ), F32),
            pltpu.VMEM((ATT_GROUP, V_ROWS, TQ), F32),
            pltpu.VMEM((2, ATT_GROUP, TK, TQ), F32),
            pltpu.VMEM((2, ATT_GROUP, 1, TQ), F32),
        ],
        compiler_params=pltpu.CompilerParams(
            dimension_semantics=("arbitrary", "arbitrary", "arbitrary"), vmem_limit_bytes=VMEM_LIMIT),
        name="attention",
    )(qt, qt, k, vt)


def _rms(y, w):
    ms = jnp.mean(y * y, axis=-1, keepdims=True)
    return y * lax.rsqrt(ms + EPS) * w


def _mix_ffn_kernel(of_ref, ob_ref, sg_ref, att_ref, x_ref, wo_ref, hgw_ref, attw_ref, n2w_ref,
                    wgu_ref, wd_ref, fw_ref, o_ref, *, d_ff, final_norm):
    o = of_ref[...] + ob_ref[...]
    hgw = hgw_ref[...]
    o_hg = jnp.concatenate(
        [_rms(o[:, hh * HG_D:(hh + 1) * HG_D], hgw) for hh in range(HG_HEADS)], axis=1)
    o_hg = (o_hg * sg_ref[...].astype(F32)).astype(BF16)
    o_att = _rms(att_ref[...], attw_ref[...]).astype(BF16)
    y = x_ref[...] + _dot(o_hg, wo_ref[0:HG_W, :]) + _dot(o_att, wo_ref[HG_W:HG_W + ATT_QW, :])
    h = _rms(y, n2w_ref[...]).astype(BF16)
    ffn = None
    for c in range(d_ff // FF_CHUNK):
        lo = c * FF_CHUNK
        gate = _dot(h, wgu_ref[:, lo:lo + FF_CHUNK])
        up = _dot(h, wgu_ref[:, d_ff + lo:d_ff + lo + FF_CHUNK])
        act = (_silu(gate) * up).astype(BF16)
        part = _dot(act, wd_ref[lo:lo + FF_CHUNK, :])
        ffn = part if ffn is None else ffn + part
    z = y + ffn
    o_ref[...] = _rms(z, fw_ref[...]) if final_norm else z


def _mix_ffn(o_f, o_b, sg, att, x2d, wo_bf, hgw, attw, n2w, wgu_bf, wd_bf, fw, final_norm):
    rows, d = x2d.shape
    d_ff = wd_bf.shape[0]
    tm = TM_FFN
    row_blk = lambda i: (i, 0)
    const = lambda i: (0, 0)
    half = pl.BlockSpec((tm, HG_W), row_blk)
    wide = pl.BlockSpec((tm, d), row_blk)
    small = lambda a: pl.BlockSpec(a.shape, const)
    resident = lambda a: pl.BlockSpec(a.shape, const, pipeline_mode=pl.Buffered(1))
    return pl.pallas_call(
        functools.partial(_mix_ffn_kernel, d_ff=d_ff, final_norm=final_norm),
        grid=(rows // tm,),
        in_specs=[half, half, half, half, wide, resident(wo_bf), small(hgw), small(attw), small(n2w),
                  resident(wgu_bf), resident(wd_bf), small(fw)],
        out_specs=wide,
        out_shape=jax.ShapeDtypeStruct((rows, d), F32),
        compiler_params=pltpu.CompilerParams(
            dimension_semantics=("arbitrary",), vmem_limit_bytes=VMEM_LIMIT),
        name="mix_ffn",
    )(o_f, o_b, sg, att, x2d, wo_bf, hgw, attw, n2w, wgu_bf, wd_bf, fw)


def _rope_tables(seq_len):
    rows = seq_len // GRID_W
    row = jnp.repeat(jnp.arange(rows), GRID_W).astype(F32)
    col = jnp.tile(jnp.arange(GRID_W), rows).astype(F32)
    axis_dim = ATT_DH // 2
    freqs = ROPE_THETA ** (-jnp.arange(0, axis_dim, 2, dtype=F32) / axis_dim)
    ang = jnp.concatenate([row[:, None] * freqs, col[:, None] * freqs], axis=-1)
    cos = jnp.repeat(jnp.cos(ang), 2, axis=1)
    sin = jnp.repeat(jnp.sin(ang), 2, axis=1) * jnp.tile(jnp.array([-1.0, 1.0], F32), ATT_DH // 2)
    reps = LANES // ATT_DH
    return jnp.tile(cos, (1, reps)), jnp.tile(sin, (1, reps))


def kernel(x, norm1_w, w_in, lb_logits, hg_norm_w, q_norm_w, k_norm_w, att_norm_w, w_out, norm2_w,
           w_gate_up, w_down, final_norm_w):
    batch, seq_len, d_model = x.shape
    depth = norm1_w.shape[0]
    rows = batch * seq_len
    assert seq_len % (2 * TK) == 0 and seq_len % TQ == 0
    assert seq_len % T_SCAN == 0 and TK % TM_IN == 0
    assert TM_IN % TQ == 0 and rows % TM_FFN == 0 and w_in.shape[2] == D_IN

    lb_all = jnp.cumsum(jax.nn.softmax(lb_logits.astype(F32), axis=1), axis=1)
    cos_t, sin_t = _rope_tables(seq_len)
    blk = jnp.arange(ATT_QW) // ATT_DH
    mq = jnp.where(blk[:, None] == blk[None, :], 1.0 / ATT_DH, 0.0).astype(BF16)
    t_idx = jnp.arange(T_SCAN)
    same_chunk = (t_idx[:, None] // CHUNK) == (t_idx[None, :] // CHUNK)
    tri = jnp.stack([same_chunk & (t_idx[:, None] >= t_idx[None, :]),
                     same_chunk & (t_idx[None, :] >= t_idx[:, None])]).astype(BF16)
    q_scale = ATT_DH ** -0.5 * math.log2(math.e)

    x2d = x.reshape(rows, d_model)
    row = lambda v: v.astype(F32).reshape(1, -1)
    for l in range(depth):
        hq, gf, kf, gb, kb, iv, sg, qt, k, vt = _inproj(
            x2d, row(norm1_w[l]), w_in[l].astype(BF16), row(lb_all[0, l]), row(lb_all[1, l]),
            row(jnp.tile(q_norm_w[l], ATT_HEADS)) * q_scale, row(jnp.tile(k_norm_w[l], ATT_KV)),
            cos_t, sin_t, mq, seq_len)
        o_f, o_b = _hgrn(hq, gf, kf, gb, kb, iv, tri, batch, seq_len)
        att = _attention(qt, k, vt, batch, seq_len)
        x2d = _mix_ffn(o_f, o_b, sg, att, x2d, w_out[l].astype(BF16),
                       row(hg_norm_w[l]), row(att_norm_w[l]), row(norm2_w[l]),
                       w_gate_up[l].astype(BF16), w_down[l].astype(BF16), row(final_norm_w), l == depth - 1)
    return x2d.reshape(batch, seq_len, d_model)
```

```python
import functools
import math

import jax
import jax.numpy as jnp
from jax import lax
from jax.experimental import pallas as pl
from jax.experimental.pallas import tpu as pltpu

F32 = jnp.float32
BF16 = jnp.bfloat16

EPS = 1e-6
GRID_W = 64
ROPE_THETA = 10000.0

HG_HEADS = 4
HG_D = 128
HG_W = HG_HEADS * HG_D
CHUNK = 64

ATT_HEADS = 8
ATT_KV = 2
ATT_GROUP = ATT_HEADS // ATT_KV
ATT_DH = 64
ATT_QW = ATT_HEADS * ATT_DH
ATT_KVW = ATT_KV * ATT_DH
V_ROWS = ATT_DH + 16

C_Q, C_FF, C_FB, C_I, C_G = 0, HG_W, 2 * HG_W, 3 * HG_W, 4 * HG_W
C_AQ = 5 * HG_W
C_AK = C_AQ + ATT_QW
C_AV = C_AK + ATT_KVW
D_IN = C_AV + ATT_KVW

LANES = 128
VMEM_LIMIT = 56 * 1024 * 1024

TM_IN = 512
TQ = 256
TK = 512
T_SCAN = 256
TM_FFN = 512
FF_CHUNK = 256
PIECE = 256


def _silu(u):
    hu = 0.5 * u
    return hu + hu * jnp.tanh(hu)


def _dot(a, b):
    return jnp.dot(a, b, preferred_element_type=F32)


def _dot_nt(a, b):
    return lax.dot_general(a, b, (((1,), (1,)), ((), ())), preferred_element_type=F32)


def _dot_tn(a, b):
    return lax.dot_general(a, b, (((0,), (0,)), ((), ())), preferred_element_type=F32)


def _inproj_kernel(x_ref, n1w_ref, w_ref, lbf_ref, lbb_ref, wq_ref, wk_ref, cos_ref, sin_ref, mq_ref,
                   hq_ref, gf_ref, kf_ref, gb_ref, kb_ref, iv_ref, sg_ref, qt_ref, k_ref, vt_ref):
    tm = x_ref.shape[0]
    x = x_ref[...]
    ms = jnp.mean(x * x, axis=-1, keepdims=True)
    h = (x * lax.rsqrt(ms + EPS) * n1w_ref[...]).astype(BF16)

    def proj(lo, hi):
        return _dot(h, w_ref[:, lo:hi])

    def gates(z, lb, g_ref, k_out_ref, cs):
        lb = lb[:, cs]
        hs = 0.5 * (1.0 - lb)
        ht = hs * jnp.tanh(0.5 * z)
        g_ref[:, cs] = jnp.log2((lb + hs) + ht)
        k_out_ref[:, cs] = (hs - ht).astype(BF16)

    cos = cos_ref[...]
    sin = sin_ref[...]
    lane = lax.broadcasted_iota(jnp.int32, (tm, LANES), 1)
    even = (lane & 1) == 0

    def rope(xc):
        partner = jnp.where(even, pltpu.roll(xc, LANES - 1, 1), pltpu.roll(xc, 1, 1))
        return xc * cos + partner * sin

    def head_rms(a, m, w):
        return a * lax.rsqrt(_dot((a * a).astype(BF16), m) + EPS) * w

    def ep_aq(aq, piece, cs):
        yq = head_rms(aq, mq_ref[0:PIECE, 0:PIECE], wq_ref[:, cs])
        for pair in range(PIECE // LANES):
            qct = rope(yq[:, pair * LANES:(pair + 1) * LANES]).T.astype(BF16)
            for e in range(2):
                hl = 2 * pair + e
                for blk in range(tm // TQ):
                    col = (blk * ATT_GROUP + hl) * TQ
                    qt_ref[piece, :, col:col + TQ] = qct[e * ATT_DH:(e + 1) * ATT_DH, blk * TQ:(blk + 1) * TQ]

    def ep_ak(ak, piece, cs):
        yk = rope(head_rms(ak, mq_ref[0:LANES, 0:LANES], wk_ref[...])).astype(BF16)
        k_ref[0] = yk[:, 0:ATT_DH]
        k_ref[1] = yk[:, ATT_DH:2 * ATT_DH]

    def ep_av(av, piece, cs):
        avt = av.T.astype(BF16)
        ones = jnp.ones((V_ROWS - ATT_DH, tm), BF16)
        for g in range(ATT_KV):
            vt_ref[g, 0, 0:ATT_DH, :] = avt[g * ATT_DH:(g + 1) * ATT_DH, :]
            vt_ref[g, 0, ATT_DH:V_ROWS, :] = ones

    def ep_ff(u, piece, cs):
        gates(u, lbf_ref[...], gf_ref, kf_ref, cs)

    def ep_fb(u, piece, cs):
        gates(u, lbb_ref[...], gb_ref, kb_ref, cs)

    def ep_q(u, piece, cs):
        hq_ref[:, cs] = _silu(u).astype(BF16)

    def ep_g(u, piece, cs):
        sg_ref[:, cs] = _silu(u).astype(BF16)

    def ep_i(u, piece, cs):
        iv_ref[:, cs] = u.astype(BF16)

    halves = lambda base, ep: [(base + p * PIECE, PIECE, ep, p) for p in range(HG_W // PIECE)]
    stages = (halves(C_AQ, ep_aq) + [(C_AK, ATT_KVW, ep_ak, 0), (C_AV, ATT_KVW, ep_av, 0)]
              + halves(C_FF, ep_ff) + halves(C_FB, ep_fb) + halves(C_Q, ep_q) + halves(C_G, ep_g)
              + halves(C_I, ep_i))
    pending = None
    for lo, width, epilogue, piece in stages:
        u = proj(lo, lo + width)
        if pending is not None:
            pending()
        pending = functools.partial(epilogue, u, piece, slice(piece * PIECE, (piece + 1) * PIECE))
    pending()


def _inproj(x2d, n1w, w_bf, lbf, lbb, wq, wk, cos_t, sin_t, mq, seq_len):
    rows = x2d.shape[0]
    tm = TM_IN
    n_tiles = rows // tm
    tiles_per_seq = seq_len // tm
    per_tk = TK // tm
    row_blk = lambda i: (i, 0)
    const = lambda i: (0, 0)
    full = lambda shape: pl.BlockSpec(shape, const)
    act_bf = jax.ShapeDtypeStruct((rows, HG_W), BF16)
    act_f32 = jax.ShapeDtypeStruct((rows, HG_W), F32)
    act_spec = pl.BlockSpec((tm, HG_W), row_blk)
    return pl.pallas_call(
        _inproj_kernel,
        grid=(n_tiles,),
        in_specs=[
            pl.BlockSpec((tm, x2d.shape[1]), row_blk),
            full(n1w.shape), full(w_bf.shape), full(lbf.shape), full(lbb.shape),
            full(wq.shape), full(wk.shape),
            pl.BlockSpec((tm, LANES), lambda i: (i % tiles_per_seq, 0)),
            pl.BlockSpec((tm, LANES), lambda i: (i % tiles_per_seq, 0)),
            full(mq.shape),
        ],
        out_specs=[
            act_spec, act_spec, act_spec, act_spec, act_spec, act_spec, act_spec,
            pl.BlockSpec((ATT_KV, ATT_DH, ATT_GROUP * tm), lambda i: (0, 0, i)),
            pl.BlockSpec((ATT_KV, tm, ATT_DH), lambda i: (0, i, 0)),
            pl.BlockSpec((ATT_KV, 1, V_ROWS, tm), lambda i: (0, i // per_tk, 0, i % per_tk)),
        ],
        out_shape=[
            act_bf, act_f32, act_bf, act_f32, act_bf, act_bf, act_bf,
            jax.ShapeDtypeStruct((ATT_KV, ATT_DH, ATT_GROUP * rows), BF16),
            jax.ShapeDtypeStruct((ATT_KV, rows, ATT_DH), BF16),
            jax.ShapeDtypeStruct((ATT_KV, rows // TK, V_ROWS, TK), BF16),
        ],
        compiler_params=pltpu.CompilerParams(
            dimension_semantics=("arbitrary",), vmem_limit_bytes=VMEM_LIMIT),
        name="inproj",
    )(x2d, n1w, w_bf, lbf, lbb, wq, wk, cos_t, sin_t, mq)


def _hgrn_kernel(qf_ref, gf_ref, kf_ref, vf_ref, qb_ref, gb_ref, kb_ref, vb_ref, tri_ref,
                 of_ref, ob_ref, st_ref):
    @pl.when(pl.program_id(1) == 0)
    def _():
        st_ref[...] = jnp.zeros_like(st_ref)

    n_chunks = qf_ref.shape[0] // CHUNK
    row = lax.broadcasted_iota(jnp.int32, (CHUNK, CHUNK), 0)
    col = lax.broadcasted_iota(jnp.int32, (CHUNK, CHUNK), 1)
    plans = (
        (qf_ref, gf_ref, kf_ref, vf_ref, of_ref, row >= col, CHUNK // 2 - 1, CHUNK - 1, range(n_chunks)),
        (qb_ref, gb_ref, kb_ref, vb_ref, ob_ref, col >= row, CHUNK // 2, 0, range(n_chunks - 1, -1, -1)),
    )

    def sub(a, c, hh):
        return a[c * CHUNK:(c + 1) * CHUNK, hh * HG_D:(hh + 1) * HG_D]

    cum, staged = [], []
    for d, (q_ref, g_ref, k_ref, v_ref, o_ref, keep, ref_row, end_row, order) in enumerate(plans):
        b = _split_dot_lhs(tri_ref[d], g_ref[...])
        b_r = jnp.concatenate(
            [jnp.broadcast_to(b[c * CHUNK + ref_row:c * CHUNK + ref_row + 1, :], (CHUNK, HG_W))
             for c in range(n_chunks)], axis=0)
        rel = b - b_r
        cum.append(b)
        staged.append(((q_ref[...].astype(F32) * jnp.exp2(rel)).astype(BF16),
                       (k_ref[...].astype(F32) * jnp.exp2(-rel)).astype(BF16)))

    def local_stage(ci):
        work, scores, contrib = [], [], []
        for d, (q_ref, g_ref, k_ref, v_ref, o_ref, keep, ref_row, end_row, order) in enumerate(plans):
            c = order[ci]
            for hh in range(HG_HEADS):
                q_in, k_in = sub(staged[d][0], c, hh), sub(staged[d][1], c, hh)
                v = v_ref[c * CHUNK:(c + 1) * CHUNK, hh * HG_D:(hh + 1) * HG_D]
                work.append((d, c, hh, q_in, v))
                scores.append(jnp.where(keep, _dot_nt(q_in, k_in), 0.0).astype(BF16))
                contrib.append(_dot_tn(v, k_in))
        return work, scores, contrib

    def state_stage(work, scores, contrib):
        for (d, c, hh, q_in, v), sc, ct in zip(work, scores, contrib):
            o_ref, ref_row, end_row = plans[d][4], plans[d][6], plans[d][7]
            cs = slice(hh * HG_D, (hh + 1) * HG_D)
            b_r = cum[d][c * CHUNK + ref_row:c * CHUNK + ref_row + 1, cs]
            b_end = cum[d][c * CHUNK + end_row:c * CHUNK + end_row + 1, cs]
            st = st_ref[d * HG_HEADS + hh]
            lhs = jnp.concatenate([q_in, sc], axis=1)
            rhs = jnp.concatenate([(st * jnp.exp2(b_r)).astype(BF16), v.T], axis=1)
            o_ref[c * CHUNK:(c + 1) * CHUNK, cs] = _dot_nt(lhs, rhs)
            st_ref[d * HG_HEADS + hh] = st * jnp.exp2(b_end) + ct * jnp.exp2(b_end - b_r)

    ahead = local_stage(0)
    for ci in range(n_chunks):
        current, ahead = ahead, (local_stage(ci + 1) if ci + 1 < n_chunks else None)
        state_stage(*current)


def _split_dot_lhs(m, a):
    hi = a.astype(BF16)
    lo = (a - hi.astype(F32)).astype(BF16)
    return _dot(m, hi) + _dot(m, lo)


def _hgrn(hq, gf, kf, gb, kb, iv, tri, batch, seq_len):
    rows = hq.shape[0]
    t = T_SCAN
    nb = seq_len // t
    fwd = lambda b, n: (b * nb + n, 0)
    bwd = lambda b, n: (b * nb + nb - 1 - n, 0)
    sf = pl.BlockSpec((t, HG_W), fwd)
    sb = pl.BlockSpec((t, HG_W), bwd)
    out = jax.ShapeDtypeStruct((rows, HG_W), F32)
    return pl.pallas_call(
        _hgrn_kernel,
        grid=(batch, nb),
        in_specs=[sf, sf, sf, sf, sb, sb, sb, sb, pl.BlockSpec(tri.shape, lambda b, n: (0, 0, 0))],
        out_specs=[sf, sb],
        out_shape=[out, out],
        scratch_shapes=[pltpu.VMEM((2 * HG_HEADS, HG_D, HG_D), F32)],
        compiler_params=pltpu.CompilerParams(
            dimension_semantics=("arbitrary", "arbitrary"), vmem_limit_bytes=VMEM_LIMIT),
        name="hgrn_scan",
    )(hq, gf, kf, iv, hq, gb, kb, iv, tri)


def _attn_kernel(q_ref, qn_ref, k_ref, v_ref, o_ref, m_ref, acc_ref, s_ref, bm_ref):
    tq = o_ref.shape[0]
    tk = v_ref.shape[3]
    n_kb = v_ref.shape[1]

    def scores(qsrc_ref, j, hh):
        s = _dot(k_ref[0, j * tk:(j + 1) * tk, :],
                 qsrc_ref[0, :, hh * tq:(hh + 1) * tq])
        s_ref[j % 2, hh] = s
        bm_ref[j % 2, hh] = jnp.max(s, axis=0, keepdims=True)

    def accumulate(j, hh):
        m_old = m_ref[hh]
        m_new = jnp.maximum(m_old, bm_ref[j % 2, hh])
        p = jnp.exp2(s_ref[j % 2, hh] - m_new).astype(BF16)
        acc_ref[hh] = jnp.exp2(m_old - m_new) * acc_ref[hh] + _dot(v_ref[0, j], p)
        m_ref[hh] = m_new

    @pl.when(pl.program_id(2) == 0)
    def _():
        for hh in range(ATT_GROUP):
            scores(q_ref, 0, hh)

    m_ref[...] = jnp.full(m_ref.shape, -1e30, F32)
    acc_ref[...] = jnp.zeros_like(acc_ref)
    for j in range(n_kb):
        for hh in range(ATT_GROUP):
            if j + 1 < n_kb:
                scores(q_ref, j + 1, hh)
            else:
                scores(qn_ref, 0, hh)
            accumulate(j, hh)
    for pair in range(ATT_GROUP // 2):
        ot = jnp.concatenate(
            [acc_ref[2 * pair + e, 0:ATT_DH, :] / acc_ref[2 * pair + e, ATT_DH:ATT_DH + 1, :]
             for e in range(2)], axis=0)
        o_ref[:, pair * LANES:(pair + 1) * LANES] = ot.T


def _attention(qt, k, vt, batch, seq_len):
    rows = k.shape[1]
    nq = seq_len // TQ
    n_kb = seq_len // TK
    return pl.pallas_call(
        _attn_kernel,
        grid=(batch, ATT_KV, nq),
        in_specs=[
            pl.BlockSpec((1, ATT_DH, ATT_GROUP * TQ), lambda b, g, i: (g, 0, b * nq + i)),
            pl.BlockSpec((1, ATT_DH, ATT_GROUP * TQ), lambda b, g, i: (g, 0, b * nq + jnp.minimum(i + 1, nq - 1))),
            pl.BlockSpec((1, seq_len, ATT_DH), lambda b, g, i: (g, b, 0)),
            pl.BlockSpec((1, n_kb, V_ROWS, TK), lambda b, g, i: (g, b, 0, 0)),
        ],
        out_specs=pl.BlockSpec((TQ, ATT_GROUP * ATT_DH), lambda b, g, i: (b * nq + i, g)),
        out_shape=jax.ShapeDtypeStruct((rows, ATT_QW), F32),
        scratch_shapes=[
            pltpu.VMEM((ATT_GROUP, 1, TQ), F32),
            pltpu.VMEM((ATT_GROUP, V_ROWS, TQ), F32),
            pltpu.VMEM((2, ATT_GROUP, TK, TQ), F32),
            pltpu.VMEM((2, ATT_GROUP, 1, TQ), F32),
        ],
        compiler_params=pltpu.CompilerParams(
            dimension_semantics=("arbitrary", "arbitrary", "arbitrary"), vmem_limit_bytes=VMEM_LIMIT),
        name="attention",
    )(qt, qt, k, vt)


def _rms(y, w):
    ms = jnp.mean(y * y, axis=-1, keepdims=True)
    return y * lax.rsqrt(ms + EPS) * w


def _mix_ffn_kernel(of_ref, ob_ref, sg_ref, att_ref, x_ref, wo_ref, hgw_ref, attw_ref, n2w_ref,
                    wgu_ref, wd_ref, fw_ref, o_ref, *, d_ff, final_norm):
    o = of_ref[...] + ob_ref[...]
    hgw = hgw_ref[...]
    o_hg = jnp.concatenate(
        [_rms(o[:, hh * HG_D:(hh + 1) * HG_D], hgw) for hh in range(HG_HEADS)], axis=1)
    o_hg = (o_hg * sg_ref[...].astype(F32)).astype(BF16)
    o_att = _rms(att_ref[...], attw_ref[...]).astype(BF16)
    y = x_ref[...] + _dot(o_hg, wo_ref[0:HG_W, :]) + _dot(o_att, wo_ref[HG_W:HG_W + ATT_QW, :])
    h = _rms(y, n2w_ref[...]).astype(BF16)
    ffn = None
    for c in range(d_ff // FF_CHUNK):
        lo = c * FF_CHUNK
        gate = _dot(h, wgu_ref[:, lo:lo + FF_CHUNK])
        up = _dot(h, wgu_ref[:, d_ff + lo:d_ff + lo + FF_CHUNK])
        act = (_silu(gate) * up).astype(BF16)
        part = _dot(act, wd_ref[lo:lo + FF_CHUNK, :])
        ffn = part if ffn is None else ffn + part
    z = y + ffn
    o_ref[...] = _rms(z, fw_ref[...]) if final_norm else z


def _mix_ffn(o_f, o_b, sg, att, x2d, wo_bf, hgw, attw, n2w, wgu_bf, wd_bf, fw, final_norm):
    rows, d = x2d.shape
    d_ff = wd_bf.shape[0]
    tm = TM_FFN
    row_blk = lambda i: (i, 0)
    const = lambda i: (0, 0)
    half = pl.BlockSpec((tm, HG_W), row_blk)
    wide = pl.BlockSpec((tm, d), row_blk)
    small = lambda a: pl.BlockSpec(a.shape, const)
    resident = lambda a: pl.BlockSpec(a.shape, const, pipeline_mode=pl.Buffered(1))
    return pl.pallas_call(
        functools.partial(_mix_ffn_kernel, d_ff=d_ff, final_norm=final_norm),
        grid=(rows // tm,),
        in_specs=[half, half, half, half, wide, resident(wo_bf), small(hgw), small(attw), small(n2w),
                  resident(wgu_bf), resident(wd_bf), small(fw)],
        out_specs=wide,
        out_shape=jax.ShapeDtypeStruct((rows, d), F32),
        compiler_params=pltpu.CompilerParams(
            dimension_semantics=("arbitrary",), vmem_limit_bytes=VMEM_LIMIT),
        name="mix_ffn",
    )(o_f, o_b, sg, att, x2d, wo_bf, hgw, attw, n2w, wgu_bf, wd_bf, fw)


def _rope_tables(seq_len):
    rows = seq_len // GRID_W
    row = jnp.repeat(jnp.arange(rows), GRID_W).astype(F32)
    col = jnp.tile(jnp.arange(GRID_W), rows).astype(F32)
    axis_dim = ATT_DH // 2
    freqs = ROPE_THETA ** (-jnp.arange(0, axis_dim, 2, dtype=F32) / axis_dim)
    ang = jnp.concatenate([row[:, None] * freqs, col[:, None] * freqs], axis=-1)
    cos = jnp.repeat(jnp.cos(ang), 2, axis=1)
    sin = jnp.repeat(jnp.sin(ang), 2, axis=1) * jnp.tile(jnp.array([-1.0, 1.0], F32), ATT_DH // 2)
    reps = LANES // ATT_DH
    return jnp.tile(cos, (1, reps)), jnp.tile(sin, (1, reps))


def kernel(x, norm1_w, w_in, lb_logits, hg_norm_w, q_norm_w, k_norm_w, att_norm_w, w_out, norm2_w,
           w_gate_up, w_down, final_norm_w):
    batch, seq_len, d_model = x.shape
    depth = norm1_w.shape[0]
    rows = batch * seq_len
    assert seq_len % (2 * TK) == 0 and seq_len % TQ == 0
    assert seq_len % T_SCAN == 0 and TK % TM_IN == 0
    assert TM_IN % TQ == 0 and rows % TM_FFN == 0 and w_in.shape[2] == D_IN

    lb_all = jnp.cumsum(jax.nn.softmax(lb_logits.astype(F32), axis=1), axis=1)
    cos_t, sin_t = _rope_tables(seq_len)
    blk = jnp.arange(ATT_QW) // ATT_DH
    mq = jnp.where(blk[:, None] == blk[None, :], 1.0 / ATT_DH, 0.0).astype(BF16)
    t_idx = jnp.arange(T_SCAN)
    same_chunk = (t_idx[:, None] // CHUNK) == (t_idx[None, :] // CHUNK)
    tri = jnp.stack([same_chunk & (t_idx[:, None] >= t_idx[None, :]),
                     same_chunk & (t_idx[None, :] >= t_idx[:, None])]).astype(BF16)
    q_scale = ATT_DH ** -0.5 * math.log2(math.e)

    x2d = x.reshape(rows, d_model)
    row = lambda v: v.astype(F32).reshape(1, -1)
    for l in range(depth):
        hq, gf, kf, gb, kb, iv, sg, qt, k, vt = _inproj(
            x2d, row(norm1_w[l]), w_in[l].astype(BF16), row(lb_all[0, l]), row(lb_all[1, l]),
            row(jnp.tile(q_norm_w[l], ATT_HEADS)) * q_scale, row(jnp.tile(k_norm_w[l], ATT_KV)),
            cos_t, sin_t, mq, seq_len)
        o_f, o_b = _hgrn(hq, gf, kf, gb, kb, iv, tri, batch, seq_len)
        att = _attention(qt, k, vt, batch, seq_len)
        x2d = _mix_ffn(o_f, o_b, sg, att, x2d, w_out[l].astype(BF16),
                       row(hg_norm_w[l]), row(att_norm_w[l]), row(norm2_w[l]),
                       w_gate_up[l].astype(BF16), w_down[l].astype(BF16), row(final_norm_w), l == depth - 1)
    return x2d.reshape(batch, seq_len, d_model)
```

```python
import functools
import math

import jax
import jax.numpy as jnp
import numpy as np
from jax import lax
from jax.experimental import pallas as pl
from jax.experimental.pallas import tpu as pltpu

F32 = jnp.float32
BF16 = jnp.bfloat16

EPS = 1e-6
GRID_W = 64
ROPE_THETA = 10000.0

HG_HEADS = 4
HG_D = 128
HG_W = HG_HEADS * HG_D
CHUNK = 64

ATT_HEADS = 8
ATT_KV = 2
ATT_GROUP = ATT_HEADS // ATT_KV
ATT_DH = 64
ATT_QW = ATT_HEADS * ATT_DH
ATT_KVW = ATT_KV * ATT_DH
V_ROWS = ATT_DH + 16

C_Q, C_FF, C_FB, C_I, C_G = 0, HG_W, 2 * HG_W, 3 * HG_W, 4 * HG_W
C_AQ = 5 * HG_W
C_AK = C_AQ + ATT_QW
C_AV = C_AK + ATT_KVW
D_IN = C_AV + ATT_KVW

LANES = 128
VMEM_LIMIT = 56 * 1024 * 1024

TM_IN = 512
TQ = 256
TK = 512
T_SCAN = 256
TM_FFN = 512
FF_CHUNK = 256
PIECE = 256


def _silu(u):
    hu = 0.5 * u
    return hu + hu * jnp.tanh(hu)


def _dot(a, b):
    return jnp.dot(a, b, preferred_element_type=F32)


def _dot_nt(a, b):
    return lax.dot_general(a, b, (((1,), (1,)), ((), ())), preferred_element_type=F32)


def _dot_tn(a, b):
    return lax.dot_general(a, b, (((0,), (0,)), ((), ())), preferred_element_type=F32)


def _inproj_kernel(x_ref, n1w_ref, w32_ref, lbf_ref, lbb_ref, wq_ref, wk_ref, cos_ref, sin_ref, mq_ref,
                   hq_ref, gf_ref, kf_ref, gb_ref, kb_ref, iv_ref, sg_ref, qt_ref, k_ref, vt_ref, w_ref):
    @pl.when(pl.program_id(0) == 0)
    def _():
        w_ref[...] = w32_ref[...].astype(BF16)

    tm = x_ref.shape[0]
    x = x_ref[...]
    ms = jnp.mean(x * x, axis=-1, keepdims=True)
    h = (x * lax.rsqrt(ms + EPS) * n1w_ref[...]).astype(BF16)

    def proj(lo, hi):
        return _dot(h, w_ref[:, lo:hi])

    def gates(z, lb, g_ref, k_out_ref, cs):
        lb = lb[:, cs]
        hs = 0.5 * (1.0 - lb)
        ht = hs * jnp.tanh(0.5 * z)
        g_ref[:, cs] = jnp.log2((lb + hs) + ht)
        k_out_ref[:, cs] = (hs - ht).astype(BF16)

    cos = cos_ref[...]
    sin = sin_ref[...]
    lane = lax.broadcasted_iota(jnp.int32, (tm, LANES), 1)
    even = (lane & 1) == 0

    def rope(xc):
        partner = jnp.where(even, pltpu.roll(xc, LANES - 1, 1), pltpu.roll(xc, 1, 1))
        return xc * cos + partner * sin

    def head_rms(a, m, w):
        return a * lax.rsqrt(_dot((a * a).astype(BF16), m) + EPS) * w

    def ep_aq(aq, piece, cs):
        yq = head_rms(aq, mq_ref[0:PIECE, 0:PIECE], wq_ref[:, cs])
        for pair in range(PIECE // LANES):
            qct = rope(yq[:, pair * LANES:(pair + 1) * LANES]).T.astype(BF16)
            for e in range(2):
                hl = 2 * pair + e
                for blk in range(tm // TQ):
                    col = (blk * ATT_GROUP + hl) * TQ
                    qt_ref[piece, :, col:col + TQ] = qct[e * ATT_DH:(e + 1) * ATT_DH, blk * TQ:(blk + 1) * TQ]

    def ep_ak(ak, piece, cs):
        yk = rope(head_rms(ak, mq_ref[0:LANES, 0:LANES], wk_ref[...])).astype(BF16)
        k_ref[0] = yk[:, 0:ATT_DH]
        k_ref[1] = yk[:, ATT_DH:2 * ATT_DH]

    def ep_av(av, piece, cs):
        avt = av.T.astype(BF16)
        ones = jnp.ones((V_ROWS - ATT_DH, tm), BF16)
        for g in range(ATT_KV):
            vt_ref[g, 0, 0:ATT_DH, :] = avt[g * ATT_DH:(g + 1) * ATT_DH, :]
            vt_ref[g, 0, ATT_DH:V_ROWS, :] = ones

    def ep_ff(u, piece, cs):
        gates(u, lbf_ref[...], gf_ref, kf_ref, cs)

    def ep_fb(u, piece, cs):
        gates(u, lbb_ref[...], gb_ref, kb_ref, cs)

    def ep_q(u, piece, cs):
        hq_ref[:, cs] = _silu(u).astype(BF16)

    def ep_g(u, piece, cs):
        sg_ref[:, cs] = _silu(u).astype(BF16)

    def ep_i(u, piece, cs):
        iv_ref[:, cs] = u.astype(BF16)

    halves = lambda base, ep: [(base + p * PIECE, PIECE, ep, p) for p in range(HG_W // PIECE)]
    stages = (halves(C_AQ, ep_aq) + [(C_AK, ATT_KVW, ep_ak, 0), (C_AV, ATT_KVW, ep_av, 0)]
              + halves(C_FF, ep_ff) + halves(C_FB, ep_fb) + halves(C_Q, ep_q) + halves(C_G, ep_g)
              + halves(C_I, ep_i))
    pending = None
    for lo, width, epilogue, piece in stages:
        u = proj(lo, lo + width)
        if pending is not None:
            pending()
        pending = functools.partial(epilogue, u, piece, slice(piece * PIECE, (piece + 1) * PIECE))
    pending()


def _inproj(x2d, n1w, w32, lbf, lbb, wq, wk, cos_t, sin_t, mq, seq_len):
    rows = x2d.shape[0]
    tm = TM_IN
    n_tiles = rows // tm
    tiles_per_seq = seq_len // tm
    per_tk = TK // tm
    row_blk = lambda i: (i, 0)
    const = lambda i: (0, 0)
    full = lambda shape: pl.BlockSpec(shape, const)
    act_bf = jax.ShapeDtypeStruct((rows, HG_W), BF16)
    act_f32 = jax.ShapeDtypeStruct((rows, HG_W), F32)
    act_spec = pl.BlockSpec((tm, HG_W), row_blk)
    return pl.pallas_call(
        _inproj_kernel,
        grid=(n_tiles,),
        in_specs=[
            pl.BlockSpec((tm, x2d.shape[1]), row_blk),
            full(n1w.shape), pl.BlockSpec(w32.shape, const, pipeline_mode=pl.Buffered(1)),
            full(lbf.shape), full(lbb.shape),
            full(wq.shape), full(wk.shape),
            pl.BlockSpec((tm, LANES), lambda i: (i % tiles_per_seq, 0)),
            pl.BlockSpec((tm, LANES), lambda i: (i % tiles_per_seq, 0)),
            full(mq.shape),
        ],
        out_specs=[
            act_spec, act_spec, act_spec, act_spec, act_spec, act_spec, act_spec,
            pl.BlockSpec((ATT_KV, ATT_DH, ATT_GROUP * tm), lambda i: (0, 0, i)),
            pl.BlockSpec((ATT_KV, tm, ATT_DH), lambda i: (0, i, 0)),
            pl.BlockSpec((ATT_KV, 1, V_ROWS, tm), lambda i: (0, i // per_tk, 0, i % per_tk)),
        ],
        out_shape=[
            act_bf, act_f32, act_bf, act_f32, act_bf, act_bf, act_bf,
            jax.ShapeDtypeStruct((ATT_KV, ATT_DH, ATT_GROUP * rows), BF16),
            jax.ShapeDtypeStruct((ATT_KV, rows, ATT_DH), BF16),
            jax.ShapeDtypeStruct((ATT_KV, rows // TK, V_ROWS, TK), BF16),
        ],
        scratch_shapes=[pltpu.VMEM(w32.shape, BF16)],
        compiler_params=pltpu.CompilerParams(
            dimension_semantics=("arbitrary",), vmem_limit_bytes=VMEM_LIMIT),
        name="inproj",
    )(x2d, n1w, w32, lbf, lbb, wq, wk, cos_t, sin_t, mq)


def _hgrn_kernel(qf_ref, gf_ref, kf_ref, vf_ref, qb_ref, gb_ref, kb_ref, vb_ref, tri_ref,
                 of_ref, ob_ref, st_ref):
    @pl.when(pl.program_id(1) == 0)
    def _():
        st_ref[...] = jnp.zeros_like(st_ref)

    n_chunks = qf_ref.shape[0] // CHUNK
    row = lax.broadcasted_iota(jnp.int32, (CHUNK, CHUNK), 0)
    col = lax.broadcasted_iota(jnp.int32, (CHUNK, CHUNK), 1)
    plans = (
        (qf_ref, gf_ref, kf_ref, vf_ref, of_ref, row >= col, CHUNK // 2 - 1, CHUNK - 1, range(n_chunks)),
        (qb_ref, gb_ref, kb_ref, vb_ref, ob_ref, col >= row, CHUNK // 2, 0, range(n_chunks - 1, -1, -1)),
    )

    def sub(a, c, hh):
        return a[c * CHUNK:(c + 1) * CHUNK, hh * HG_D:(hh + 1) * HG_D]

    cum, staged = [], []
    for d, (q_ref, g_ref, k_ref, v_ref, o_ref, keep, ref_row, end_row, order) in enumerate(plans):
        b = _split_dot_lhs(tri_ref[d], g_ref[...])
        b_r = jnp.concatenate(
            [jnp.broadcast_to(b[c * CHUNK + ref_row:c * CHUNK + ref_row + 1, :], (CHUNK, HG_W))
             for c in range(n_chunks)], axis=0)
        rel = b - b_r
        cum.append(b)
        staged.append(((q_ref[...].astype(F32) * jnp.exp2(rel)).astype(BF16),
                       (k_ref[...].astype(F32) * jnp.exp2(-rel)).astype(BF16)))

    def local_stage(ci):
        work, scores, contrib = [], [], []
        for d, (q_ref, g_ref, k_ref, v_ref, o_ref, keep, ref_row, end_row, order) in enumerate(plans):
            c = order[ci]
            for hh in range(HG_HEADS):
                q_in, k_in = sub(staged[d][0], c, hh), sub(staged[d][1], c, hh)
                v = v_ref[c * CHUNK:(c + 1) * CHUNK, hh * HG_D:(hh + 1) * HG_D]
                work.append((d, c, hh, q_in, v))
                scores.append(jnp.where(keep, _dot_nt(q_in, k_in), 0.0).astype(BF16))
                contrib.append(_dot_tn(v, k_in))
        return work, scores, contrib

    def state_stage(work, scores, contrib):
        for (d, c, hh, q_in, v), sc, ct in zip(work, scores, contrib):
            o_ref, ref_row, end_row = plans[d][4], plans[d][6], plans[d][7]
            cs = slice(hh * HG_D, (hh + 1) * HG_D)
            b_r = cum[d][c * CHUNK + ref_row:c * CHUNK + ref_row + 1, cs]
            b_end = cum[d][c * CHUNK + end_row:c * CHUNK + end_row + 1, cs]
            st = st_ref[d * HG_HEADS + hh]
            lhs = jnp.concatenate([q_in, sc], axis=1)
            rhs = jnp.concatenate([(st * jnp.exp2(b_r)).astype(BF16), v.T], axis=1)
            o_ref[c * CHUNK:(c + 1) * CHUNK, cs] = _dot_nt(lhs, rhs)
            st_ref[d * HG_HEADS + hh] = st * jnp.exp2(b_end) + ct * jnp.exp2(b_end - b_r)

    ahead = local_stage(0)
    for ci in range(n_chunks):
        current, ahead = ahead, (local_stage(ci + 1) if ci + 1 < n_chunks else None)
        state_stage(*current)


def _split_dot_lhs(m, a):
    hi = a.astype(BF16)
    lo = (a - hi.astype(F32)).astype(BF16)
    return _dot(m, hi) + _dot(m, lo)


def _hgrn(hq, gf, kf, gb, kb, iv, tri, batch, seq_len):
    rows = hq.shape[0]
    t = T_SCAN
    nb = seq_len // t
    fwd = lambda b, n: (b * nb + n, 0)
    bwd = lambda b, n: (b * nb + nb - 1 - n, 0)
    sf = pl.BlockSpec((t, HG_W), fwd)
    sb = pl.BlockSpec((t, HG_W), bwd)
    out = jax.ShapeDtypeStruct((rows, HG_W), F32)
    return pl.pallas_call(
        _hgrn_kernel,
        grid=(batch, nb),
        in_specs=[sf, sf, sf, sf, sb, sb, sb, sb, pl.BlockSpec(tri.shape, lambda b, n: (0, 0, 0))],
        out_specs=[sf, sb],
        out_shape=[out, out],
        scratch_shapes=[pltpu.VMEM((2 * HG_HEADS, HG_D, HG_D), F32)],
        compiler_params=pltpu.CompilerParams(
            dimension_semantics=("arbitrary", "arbitrary"), vmem_limit_bytes=VMEM_LIMIT),
        name="hgrn_scan",
    )(hq, gf, kf, iv, hq, gb, kb, iv, tri)


def _attn_kernel(q_ref, qn_ref, k_ref, v_ref, o_ref, m_ref, acc_ref, s_ref, bm_ref):
    tq = o_ref.shape[0]
    tk = v_ref.shape[3]
    n_kb = v_ref.shape[1]

    def scores(qsrc_ref, j, hh):
        s = _dot(k_ref[0, j * tk:(j + 1) * tk, :],
                 qsrc_ref[0, :, hh * tq:(hh + 1) * tq])
        s_ref[j % 2, hh] = s
        bm_ref[j % 2, hh] = jnp.max(s, axis=0, keepdims=True)

    def accumulate(j, hh):
        m_old = m_ref[hh]
        m_new = jnp.maximum(m_old, bm_ref[j % 2, hh])
        p = jnp.exp2(s_ref[j % 2, hh] - m_new).astype(BF16)
        acc_ref[hh] = jnp.exp2(m_old - m_new) * acc_ref[hh] + _dot(v_ref[0, j], p)
        m_ref[hh] = m_new

    @pl.when(pl.program_id(2) == 0)
    def _():
        for hh in range(ATT_GROUP):
            scores(q_ref, 0, hh)

    m_ref[...] = jnp.full(m_ref.shape, -1e30, F32)
    acc_ref[...] = jnp.zeros_like(acc_ref)
    for j in range(n_kb):
        for hh in range(ATT_GROUP):
            if j + 1 < n_kb:
                scores(q_ref, j + 1, hh)
            else:
                scores(qn_ref, 0, hh)
            accumulate(j, hh)
    for pair in range(ATT_GROUP // 2):
        ot = jnp.concatenate(
            [acc_ref[2 * pair + e, 0:ATT_DH, :] / acc_ref[2 * pair + e, ATT_DH:ATT_DH + 1, :]
             for e in range(2)], axis=0)
        o_ref[:, pair * LANES:(pair + 1) * LANES] = ot.T


def _attention(qt, k, vt, batch, seq_len):
    rows = k.shape[1]
    nq = seq_len // TQ
    n_kb = seq_len // TK
    return pl.pallas_call(
        _attn_kernel,
        grid=(batch, ATT_KV, nq),
        in_specs=[
            pl.BlockSpec((1, ATT_DH, ATT_GROUP * TQ), lambda b, g, i: (g, 0, b * nq + i)),
            pl.BlockSpec((1, ATT_DH, ATT_GROUP * TQ), lambda b, g, i: (g, 0, b * nq + jnp.minimum(i + 1, nq - 1))),
            pl.BlockSpec((1, seq_len, ATT_DH), lambda b, g, i: (g, b, 0)),
            pl.BlockSpec((1, n_kb, V_ROWS, TK), lambda b, g, i: (g, b, 0, 0)),
        ],
        out_specs=pl.BlockSpec((TQ, ATT_GROUP * ATT_DH), lambda b, g, i: (b * nq + i, g)),
        out_shape=jax.ShapeDtypeStruct((rows, ATT_QW), F32),
        scratch_shapes=[
            pltpu.VMEM((ATT_GROUP, 1, TQ), F32),
            pltpu.VMEM((ATT_GROUP, V_ROWS, TQ), F32),
            pltpu.VMEM((2, ATT_GROUP, TK, TQ), F32),
            pltpu.VMEM((2, ATT_GROUP, 1, TQ), F32),
        ],
        compiler_params=pltpu.CompilerParams(
            dimension_semantics=("arbitrary", "arbitrary", "arbitrary"), vmem_limit_bytes=VMEM_LIMIT),
        name="attention",
    )(qt, qt, k, vt)


def _rms(y, w):
    ms = jnp.mean(y * y, axis=-1, keepdims=True)
    return y * lax.rsqrt(ms + EPS) * w


def _mix_ffn_kernel(of_ref, ob_ref, sg_ref, att_ref, x_ref, wo_ref, hgw_ref, attw_ref, n2w_ref,
                    wgu_ref, wd_ref, fw_ref, o_ref, *, d_ff, final_norm):
    o = of_ref[...] + ob_ref[...]
    hgw = hgw_ref[...]
    o_hg = jnp.concatenate(
        [_rms(o[:, hh * HG_D:(hh + 1) * HG_D], hgw) for hh in range(HG_HEADS)], axis=1)
    o_hg = (o_hg * sg_ref[...].astype(F32)).astype(BF16)
    o_att = _rms(att_ref[...], attw_ref[...]).astype(BF16)
    y = x_ref[...] + _dot(o_hg, wo_ref[0:HG_W, :]) + _dot(o_att, wo_ref[HG_W:HG_W + ATT_QW, :])
    h = _rms(y, n2w_ref[...]).astype(BF16)
    ffn = None
    for c in range(d_ff // FF_CHUNK):
        lo = c * FF_CHUNK
        gate = _dot(h, wgu_ref[:, lo:lo + FF_CHUNK])
        up = _dot(h, wgu_ref[:, d_ff + lo:d_ff + lo + FF_CHUNK])
        act = (_silu(gate) * up).astype(BF16)
        part = _dot(act, wd_ref[lo:lo + FF_CHUNK, :])
        ffn = part if ffn is None else ffn + part
    z = y + ffn
    o_ref[...] = _rms(z, fw_ref[...]) if final_norm else z


def _mix_ffn(o_f, o_b, sg, att, x2d, wo_bf, hgw, attw, n2w, wgu_bf, wd_bf, fw, final_norm):
    rows, d = x2d.shape
    d_ff = wd_bf.shape[0]
    tm = TM_FFN
    row_blk = lambda i: (i, 0)
    const = lambda i: (0, 0)
    half = pl.BlockSpec((tm, HG_W), row_blk)
    wide = pl.BlockSpec((tm, d), row_blk)
    small = lambda a: pl.BlockSpec(a.shape, const)
    resident = lambda a: pl.BlockSpec(a.shape, const, pipeline_mode=pl.Buffered(1))
    return pl.pallas_call(
        functools.partial(_mix_ffn_kernel, d_ff=d_ff, final_norm=final_norm),
        grid=(rows // tm,),
        in_specs=[half, half, half, half, wide, resident(wo_bf), small(hgw), small(attw), small(n2w),
                  resident(wgu_bf), resident(wd_bf), small(fw)],
        out_specs=wide,
        out_shape=jax.ShapeDtypeStruct((rows, d), F32),
        compiler_params=pltpu.CompilerParams(
            dimension_semantics=("arbitrary",), vmem_limit_bytes=VMEM_LIMIT),
        name="mix_ffn",
    )(o_f, o_b, sg, att, x2d, wo_bf, hgw, attw, n2w, wgu_bf, wd_bf, fw)


def _rope_tables(seq_len):
    f32 = np.float32
    rows = seq_len // GRID_W
    row = np.repeat(np.arange(rows), GRID_W).astype(f32)
    col = np.tile(np.arange(GRID_W), rows).astype(f32)
    axis_dim = ATT_DH // 2
    freqs = (f32(ROPE_THETA) ** (-np.arange(0, axis_dim, 2, dtype=f32) / f32(axis_dim))).astype(f32)
    ang = np.concatenate([row[:, None] * freqs, col[:, None] * freqs], axis=-1)
    cos = np.repeat(np.cos(ang), 2, axis=1)
    sin = np.repeat(np.sin(ang), 2, axis=1) * np.tile(np.array([-1.0, 1.0], f32), ATT_DH // 2)
    reps = LANES // ATT_DH
    return jnp.asarray(np.tile(cos, (1, reps)), F32), jnp.asarray(np.tile(sin, (1, reps)), F32)


def kernel(x, norm1_w, w_in, lb_logits, hg_norm_w, q_norm_w, k_norm_w, att_norm_w, w_out, norm2_w,
           w_gate_up, w_down, final_norm_w):
    batch, seq_len, d_model = x.shape
    depth = norm1_w.shape[0]
    rows = batch * seq_len
    assert seq_len % (2 * TK) == 0 and seq_len % TQ == 0
    assert seq_len % T_SCAN == 0 and TK % TM_IN == 0
    assert TM_IN % TQ == 0 and rows % TM_FFN == 0 and w_in.shape[2] == D_IN

    lb_all = jnp.cumsum(jax.nn.softmax(lb_logits.astype(F32), axis=1), axis=1)
    cos_t, sin_t = _rope_tables(seq_len)
    blk = np.arange(ATT_QW) // ATT_DH
    mq = jnp.asarray(np.where(blk[:, None] == blk[None, :], 1.0 / ATT_DH, 0.0), BF16)
    t_idx = np.arange(T_SCAN)
    same_chunk = (t_idx[:, None] // CHUNK) == (t_idx[None, :] // CHUNK)
    tri = jnp.asarray(np.stack([same_chunk & (t_idx[:, None] >= t_idx[None, :]),
                                same_chunk & (t_idx[None, :] >= t_idx[:, None])]).astype(np.float32),
                      BF16)
    q_scale = ATT_DH ** -0.5 * math.log2(math.e)

    x2d = x.reshape(rows, d_model)
    row = lambda v: v.astype(F32).reshape(1, -1)
    for l in range(depth):
        hq, gf, kf, gb, kb, iv, sg, qt, k, vt = _inproj(
            x2d, row(norm1_w[l]), w_in[l].astype(F32), row(lb_all[0, l]), row(lb_all[1, l]),
            row(jnp.tile(q_norm_w[l], ATT_HEADS)) * q_scale, row(jnp.tile(k_norm_w[l], ATT_KV)),
            cos_t, sin_t, mq, seq_len)
        o_f, o_b = _hgrn(hq, gf, kf, gb, kb, iv, tri, batch, seq_len)
        att = _attention(qt, k, vt, batch, seq_len)
        x2d = _mix_ffn(o_f, o_b, sg, att, x2d, w_out[l].astype(BF16),
                       row(hg_norm_w[l]), row(att_norm_w[l]), row(norm2_w[l]),
                       w_gate_up[l].astype(BF16), w_down[l].astype(BF16), row(final_norm_w), l == depth - 1)
    return x2d.reshape(batch, seq_len, d_model)
```

```python
import functools
import math

import jax
import jax.numpy as jnp
import numpy as np
from jax import lax
from jax.experimental import pallas as pl
from jax.experimental.pallas import tpu as pltpu

F32 = jnp.float32
BF16 = jnp.bfloat16

EPS = 1e-6
GRID_W = 64
ROPE_THETA = 10000.0

HG_HEADS = 4
HG_D = 128
HG_W = HG_HEADS * HG_D
CHUNK = 64

ATT_HEADS = 8
ATT_KV = 2
ATT_GROUP = ATT_HEADS // ATT_KV
ATT_DH = 64
ATT_QW = ATT_HEADS * ATT_DH
ATT_KVW = ATT_KV * ATT_DH
V_ROWS = ATT_DH + 16

C_Q, C_FF, C_FB, C_I, C_G = 0, HG_W, 2 * HG_W, 3 * HG_W, 4 * HG_W
C_AQ = 5 * HG_W
C_AK = C_AQ + ATT_QW
C_AV = C_AK + ATT_KVW
D_IN = C_AV + ATT_KVW

LANES = 128
VMEM_LIMIT = 56 * 1024 * 1024

TM_IN = 512
TQ = 256
TK = 512
T_SEG = 256
T_SCAN = 1024
TM_FFN = 512
FF_CHUNK = 256
PIECE = 256


def _silu(u):
    hu = 0.5 * u
    return hu + hu * jnp.tanh(hu)


def _dot(a, b):
    return jnp.dot(a, b, preferred_element_type=F32)


def _dot_nt(a, b):
    return lax.dot_general(a, b, (((1,), (1,)), ((), ())), preferred_element_type=F32)


def _dot_tn(a, b):
    return lax.dot_general(a, b, (((0,), (0,)), ((), ())), preferred_element_type=F32)


def _inproj_kernel(x_ref, n1w_ref, w32_ref, lbf_ref, lbb_ref, wq_ref, wk_ref, cos_ref, sin_ref, mq_ref,
                   hq_ref, gf_ref, kf_ref, gb_ref, kb_ref, iv_ref, sg_ref, qt_ref, k_ref, vt_ref, w_ref):
    @pl.when(pl.program_id(0) == 0)
    def _():
        w_ref[...] = w32_ref[...].astype(BF16)

    tm = x_ref.shape[0]
    x = x_ref[...]
    ms = jnp.mean(x * x, axis=-1, keepdims=True)
    h = (x * lax.rsqrt(ms + EPS) * n1w_ref[...]).astype(BF16)

    def proj(lo, hi):
        return _dot(h, w_ref[:, lo:hi])

    def gates(z, lb, g_ref, k_out_ref, cs):
        lb = lb[:, cs]
        hs = 0.5 * (1.0 - lb)
        ht = hs * jnp.tanh(0.5 * z)
        g_ref[:, cs] = jnp.log2((lb + hs) + ht)
        k_out_ref[:, cs] = (hs - ht).astype(BF16)

    cos = cos_ref[...]
    sin = sin_ref[...]
    lane = lax.broadcasted_iota(jnp.int32, (tm, LANES), 1)
    even = (lane & 1) == 0

    def rope(xc):
        partner = jnp.where(even, pltpu.roll(xc, LANES - 1, 1), pltpu.roll(xc, 1, 1))
        return xc * cos + partner * sin

    def head_rms(a, m, w):
        return a * lax.rsqrt(_dot((a * a).astype(BF16), m) + EPS) * w

    def ep_aq(aq, piece, cs):
        yq = head_rms(aq, mq_ref[0:PIECE, 0:PIECE], wq_ref[:, cs])
        for pair in range(PIECE // LANES):
            qct = rope(yq[:, pair * LANES:(pair + 1) * LANES]).T.astype(BF16)
            for e in range(2):
                hl = 2 * pair + e
                for blk in range(tm // TQ):
                    col = (blk * ATT_GROUP + hl) * TQ
                    qt_ref[piece, :, col:col + TQ] = qct[e * ATT_DH:(e + 1) * ATT_DH, blk * TQ:(blk + 1) * TQ]

    def ep_ak(ak, piece, cs):
        yk = rope(head_rms(ak, mq_ref[0:LANES, 0:LANES], wk_ref[...])).astype(BF16)
        k_ref[0] = yk[:, 0:ATT_DH]
        k_ref[1] = yk[:, ATT_DH:2 * ATT_DH]

    def ep_av(av, piece, cs):
        avt = av.T.astype(BF16)
        ones = jnp.ones((V_ROWS - ATT_DH, tm), BF16)
        for g in range(ATT_KV):
            vt_ref[g, 0, 0:ATT_DH, :] = avt[g * ATT_DH:(g + 1) * ATT_DH, :]
            vt_ref[g, 0, ATT_DH:V_ROWS, :] = ones

    def ep_ff(u, piece, cs):
        gates(u, lbf_ref[...], gf_ref, kf_ref, cs)

    def ep_fb(u, piece, cs):
        gates(u, lbb_ref[...], gb_ref, kb_ref, cs)

    def ep_q(u, piece, cs):
        hq_ref[:, cs] = _silu(u).astype(BF16)

    def ep_g(u, piece, cs):
        sg_ref[:, cs] = _silu(u).astype(BF16)

    def ep_i(u, piece, cs):
        iv_ref[:, cs] = u.astype(BF16)

    halves = lambda base, ep: [(base + p * PIECE, PIECE, ep, p) for p in range(HG_W // PIECE)]
    stages = (halves(C_AQ, ep_aq) + [(C_AK, ATT_KVW, ep_ak, 0), (C_AV, ATT_KVW, ep_av, 0)]
              + halves(C_FF, ep_ff) + halves(C_FB, ep_fb) + halves(C_Q, ep_q) + halves(C_G, ep_g)
              + halves(C_I, ep_i))
    pending = None
    for lo, width, epilogue, piece in stages:
        u = proj(lo, lo + width)
        if pending is not None:
            pending()
        pending = functools.partial(epilogue, u, piece, slice(piece * PIECE, (piece + 1) * PIECE))
    pending()


def _inproj(x2d, n1w, w32, lbf, lbb, wq, wk, cos_t, sin_t, mq, seq_len):
    rows = x2d.shape[0]
    tm = TM_IN
    n_tiles = rows // tm
    tiles_per_seq = seq_len // tm
    per_tk = TK // tm
    row_blk = lambda i: (i, 0)
    const = lambda i: (0, 0)
    full = lambda shape: pl.BlockSpec(shape, const)
    act_bf = jax.ShapeDtypeStruct((rows, HG_W), BF16)
    act_f32 = jax.ShapeDtypeStruct((rows, HG_W), F32)
    act_spec = pl.BlockSpec((tm, HG_W), row_blk)
    return pl.pallas_call(
        _inproj_kernel,
        grid=(n_tiles,),
        in_specs=[
            pl.BlockSpec((tm, x2d.shape[1]), row_blk),
            full(n1w.shape), pl.BlockSpec(w32.shape, const, pipeline_mode=pl.Buffered(1)),
            full(lbf.shape), full(lbb.shape),
            full(wq.shape), full(wk.shape),
            pl.BlockSpec((tm, LANES), lambda i: (i % tiles_per_seq, 0)),
            pl.BlockSpec((tm, LANES), lambda i: (i % tiles_per_seq, 0)),
            full(mq.shape),
        ],
        out_specs=[
            act_spec, act_spec, act_spec, act_spec, act_spec, act_spec, act_spec,
            pl.BlockSpec((ATT_KV, ATT_DH, ATT_GROUP * tm), lambda i: (0, 0, i)),
            pl.BlockSpec((ATT_KV, tm, ATT_DH), lambda i: (0, i, 0)),
            pl.BlockSpec((ATT_KV, 1, V_ROWS, tm), lambda i: (0, i // per_tk, 0, i % per_tk)),
        ],
        out_shape=[
            act_bf, act_f32, act_bf, act_f32, act_bf, act_bf, act_bf,
            jax.ShapeDtypeStruct((ATT_KV, ATT_DH, ATT_GROUP * rows), BF16),
            jax.ShapeDtypeStruct((ATT_KV, rows, ATT_DH), BF16),
            jax.ShapeDtypeStruct((ATT_KV, rows // TK, V_ROWS, TK), BF16),
        ],
        scratch_shapes=[pltpu.VMEM(w32.shape, BF16)],
        compiler_params=pltpu.CompilerParams(
            dimension_semantics=("arbitrary",), vmem_limit_bytes=VMEM_LIMIT),
        name="inproj",
    )(x2d, n1w, w32, lbf, lbb, wq, wk, cos_t, sin_t, mq)


def _hgrn_kernel(qf_ref, gf_ref, kf_ref, vf_ref, qb_ref, gb_ref, kb_ref, vb_ref, tri_ref,
                 of_ref, ob_ref, st_ref):
    @pl.when(pl.program_id(1) == 0)
    def _():
        st_ref[...] = jnp.zeros_like(st_ref)

    t_seg = tri_ref.shape[1]
    n_seg = qf_ref.shape[0] // t_seg
    n_chunks = t_seg // CHUNK
    row = lax.broadcasted_iota(jnp.int32, (CHUNK, CHUNK), 0)
    col = lax.broadcasted_iota(jnp.int32, (CHUNK, CHUNK), 1)
    plans = (
        (qf_ref, gf_ref, kf_ref, vf_ref, of_ref, row >= col, CHUNK // 2 - 1, CHUNK - 1, range(n_chunks)),
        (qb_ref, gb_ref, kb_ref, vb_ref, ob_ref, col >= row, CHUNK // 2, 0, range(n_chunks - 1, -1, -1)),
    )

    def sub(a, c, hh):
        return a[c * CHUNK:(c + 1) * CHUNK, hh * HG_D:(hh + 1) * HG_D]

    def stage(sv):
        out = []
        for d, (q_ref, g_ref, k_ref, v_ref, o_ref, keep, ref_row, end_row, order) in enumerate(plans):
            off = (sv if d == 0 else n_seg - 1 - sv) * t_seg
            rs = slice(off, off + t_seg)
            b = _split_dot_lhs(tri_ref[d], g_ref[rs, :])
            b_r = jnp.concatenate(
                [jnp.broadcast_to(b[c * CHUNK + ref_row:c * CHUNK + ref_row + 1, :], (CHUNK, HG_W))
                 for c in range(n_chunks)], axis=0)
            rel = b - b_r
            out.append((off, b, (q_ref[rs, :].astype(F32) * jnp.exp2(rel)).astype(BF16),
                        (k_ref[rs, :].astype(F32) * jnp.exp2(-rel)).astype(BF16)))
        return out

    def local_stage(seg, ci):
        work, scores, contrib = [], [], []
        for d, (q_ref, g_ref, k_ref, v_ref, o_ref, keep, ref_row, end_row, order) in enumerate(plans):
            off, _, q_seg, k_seg = seg[d]
            c = order[ci]
            for hh in range(HG_HEADS):
                q_in, k_in = sub(q_seg, c, hh), sub(k_seg, c, hh)
                v = v_ref[off + c * CHUNK:off + (c + 1) * CHUNK, hh * HG_D:(hh + 1) * HG_D]
                work.append((d, c, hh, q_in, v))
                scores.append(jnp.where(keep, _dot_nt(q_in, k_in), 0.0).astype(BF16))
                contrib.append(_dot_tn(v, k_in))
        return work, scores, contrib

    def state_stage(seg, work, scores, contrib):
        for (d, c, hh, q_in, v), sc, ct in zip(work, scores, contrib):
            o_ref, ref_row, end_row = plans[d][4], plans[d][6], plans[d][7]
            off, b = seg[d][0], seg[d][1]
            cs = slice(hh * HG_D, (hh + 1) * HG_D)
            b_r = b[c * CHUNK + ref_row:c * CHUNK + ref_row + 1, cs]
            b_end = b[c * CHUNK + end_row:c * CHUNK + end_row + 1, cs]
            st = st_ref[d * HG_HEADS + hh]
            lhs = jnp.concatenate([q_in, sc], axis=1)
            rhs = jnp.concatenate([(st * jnp.exp2(b_r)).astype(BF16), v.T], axis=1)
            o_ref[off + c * CHUNK:off + (c + 1) * CHUNK, cs] = _dot_nt(lhs, rhs)
            st_ref[d * HG_HEADS + hh] = st * jnp.exp2(b_end) + ct * jnp.exp2(b_end - b_r)

    seg = stage(0)
    for sv in range(n_seg):
        ahead = local_stage(seg, 0)
        nxt = stage(sv + 1) if sv + 1 < n_seg else None
        for ci in range(n_chunks):
            current, ahead = ahead, (local_stage(seg, ci + 1) if ci + 1 < n_chunks else None)
            state_stage(seg, *current)
        seg = nxt


def _split_dot_lhs(m, a):
    hi = a.astype(BF16)
    lo = (a - hi.astype(F32)).astype(BF16)
    return _dot(m, hi) + _dot(m, lo)


def _hgrn(hq, gf, kf, gb, kb, iv, tri, batch, seq_len):
    rows = hq.shape[0]
    t = T_SCAN
    nb = seq_len // t
    fwd = lambda b, n: (b * nb + n, 0)
    bwd = lambda b, n: (b * nb + nb - 1 - n, 0)
    sf = pl.BlockSpec((t, HG_W), fwd)
    sb = pl.BlockSpec((t, HG_W), bwd)
    out = jax.ShapeDtypeStruct((rows, HG_W), F32)
    return pl.pallas_call(
        _hgrn_kernel,
        grid=(batch, nb),
        in_specs=[sf, sf, sf, sf, sb, sb, sb, sb, pl.BlockSpec(tri.shape, lambda b, n: (0, 0, 0))],
        out_specs=[sf, sb],
        out_shape=[out, out],
        scratch_shapes=[pltpu.VMEM((2 * HG_HEADS, HG_D, HG_D), F32)],
        compiler_params=pltpu.CompilerParams(
            dimension_semantics=("arbitrary", "arbitrary"), vmem_limit_bytes=VMEM_LIMIT),
        name="hgrn_scan",
    )(hq, gf, kf, iv, hq, gb, kb, iv, tri)


def _attn_kernel(q_ref, qn_ref, k_ref, v_ref, o_ref, m_ref, acc_ref, s_ref, bm_ref):
    tq = o_ref.shape[0]
    tk = v_ref.shape[3]
    n_kb = v_ref.shape[1]

    def scores(qsrc_ref, j, hh):
        s = _dot(k_ref[0, j * tk:(j + 1) * tk, :],
                 qsrc_ref[0, :, hh * tq:(hh + 1) * tq])
        s_ref[j % 2, hh] = s
        bm_ref[j % 2, hh] = jnp.max(s, axis=0, keepdims=True)

    def accumulate(j, hh):
        m_old = m_ref[hh]
        m_new = jnp.maximum(m_old, bm_ref[j % 2, hh])
        p = jnp.exp2(s_ref[j % 2, hh] - m_new).astype(BF16)
        acc_ref[hh] = jnp.exp2(m_old - m_new) * acc_ref[hh] + _dot(v_ref[0, j], p)
        m_ref[hh] = m_new

    @pl.when(pl.program_id(2) == 0)
    def _():
        for hh in range(ATT_GROUP):
            scores(q_ref, 0, hh)

    m_ref[...] = jnp.full(m_ref.shape, -1e30, F32)
    acc_ref[...] = jnp.zeros_like(acc_ref)
    for j in range(n_kb):
        for hh in range(ATT_GROUP):
            if j + 1 < n_kb:
                scores(q_ref, j + 1, hh)
            else:
                scores(qn_ref, 0, hh)
            accumulate(j, hh)
    for pair in range(ATT_GROUP // 2):
        ot = jnp.concatenate(
            [acc_ref[2 * pair + e, 0:ATT_DH, :] / acc_ref[2 * pair + e, ATT_DH:ATT_DH + 1, :]
             for e in range(2)], axis=0)
        o_ref[:, pair * LANES:(pair + 1) * LANES] = ot.T


def _attention(qt, k, vt, batch, seq_len):
    rows = k.shape[1]
    nq = seq_len // TQ
    n_kb = seq_len // TK
    return pl.pallas_call(
        _attn_kernel,
        grid=(batch, ATT_KV, nq),
        in_specs=[
            pl.BlockSpec((1, ATT_DH, ATT_GROUP * TQ), lambda b, g, i: (g, 0, b * nq + i)),
            pl.BlockSpec((1, ATT_DH, ATT_GROUP * TQ), lambda b, g, i: (g, 0, b * nq + jnp.minimum(i + 1, nq - 1))),
            pl.BlockSpec((1, seq_len, ATT_DH), lambda b, g, i: (g, b, 0)),
            pl.BlockSpec((1, n_kb, V_ROWS, TK), lambda b, g, i: (g, b, 0, 0)),
        ],
        out_specs=pl.BlockSpec((TQ, ATT_GROUP * ATT_DH), lambda b, g, i: (b * nq + i, g)),
        out_shape=jax.ShapeDtypeStruct((rows, ATT_QW), F32),
        scratch_shapes=[
            pltpu.VMEM((ATT_GROUP, 1, TQ), F32),
            pltpu.VMEM((ATT_GROUP, V_ROWS, TQ), F32),
            pltpu.VMEM((2, ATT_GROUP, TK, TQ), F32),
            pltpu.VMEM((2, ATT_GROUP, 1, TQ), F32),
        ],
        compiler_params=pltpu.CompilerParams(
            dimension_semantics=("arbitrary", "arbitrary", "arbitrary"), vmem_limit_bytes=VMEM_LIMIT),
        name="attention",
    )(qt, qt, k, vt)


def _rms(y, w):
    ms = jnp.mean(y * y, axis=-1, keepdims=True)
    return y * lax.rsqrt(ms + EPS) * w


def _mix_ffn_kernel(of_ref, ob_ref, sg_ref, att_ref, x_ref, wo_ref, hgw_ref, attw_ref, n2w_ref,
                    wgu_ref, wd_ref, fw_ref, o_ref, *, d_ff, final_norm):
    o = of_ref[...] + ob_ref[...]
    hgw = hgw_ref[...]
    o_hg = jnp.concatenate(
        [_rms(o[:, hh * HG_D:(hh + 1) * HG_D], hgw) for hh in range(HG_HEADS)], axis=1)
    o_hg = (o_hg * sg_ref[...].astype(F32)).astype(BF16)
    o_att = _rms(att_ref[...], attw_ref[...]).astype(BF16)
    y = x_ref[...] + _dot(o_hg, wo_ref[0:HG_W, :]) + _dot(o_att, wo_ref[HG_W:HG_W + ATT_QW, :])
    h = _rms(y, n2w_ref[...]).astype(BF16)
    ffn = None
    for c in range(d_ff // FF_CHUNK):
        lo = c * FF_CHUNK
        gate = _dot(h, wgu_ref[:, lo:lo + FF_CHUNK])
        up = _dot(h, wgu_ref[:, d_ff + lo:d_ff + lo + FF_CHUNK])
        act = (_silu(gate) * up).astype(BF16)
        part = _dot(act, wd_ref[lo:lo + FF_CHUNK, :])
        ffn = part if ffn is None else ffn + part
    z = y + ffn
    o_ref[...] = _rms(z, fw_ref[...]) if final_norm else z


def _mix_ffn(o_f, o_b, sg, att, x2d, wo_bf, hgw, attw, n2w, wgu_bf, wd_bf, fw, final_norm):
    rows, d = x2d.shape
    d_ff = wd_bf.shape[0]
    tm = TM_FFN
    row_blk = lambda i: (i, 0)
    const = lambda i: (0, 0)
    half = pl.BlockSpec((tm, HG_W), row_blk)
    wide = pl.BlockSpec((tm, d), row_blk)
    small = lambda a: pl.BlockSpec(a.shape, const)
    resident = lambda a: pl.BlockSpec(a.shape, const, pipeline_mode=pl.Buffered(1))
    return pl.pallas_call(
        functools.partial(_mix_ffn_kernel, d_ff=d_ff, final_norm=final_norm),
        grid=(rows // tm,),
        in_specs=[half, half, half, half, wide, resident(wo_bf), small(hgw), small(attw), small(n2w),
                  resident(wgu_bf), resident(wd_bf), small(fw)],
        out_specs=wide,
        out_shape=jax.ShapeDtypeStruct((rows, d), F32),
        compiler_params=pltpu.CompilerParams(
            dimension_semantics=("arbitrary",), vmem_limit_bytes=VMEM_LIMIT),
        name="mix_ffn",
    )(o_f, o_b, sg, att, x2d, wo_bf, hgw, attw, n2w, wgu_bf, wd_bf, fw)


def _rope_tables(seq_len):
    f32 = np.float32
    rows = seq_len // GRID_W
    row = np.repeat(np.arange(rows), GRID_W).astype(f32)
    col = np.tile(np.arange(GRID_W), rows).astype(f32)
    axis_dim = ATT_DH // 2
    freqs = (f32(ROPE_THETA) ** (-np.arange(0, axis_dim, 2, dtype=f32) / f32(axis_dim))).astype(f32)
    ang = np.concatenate([row[:, None] * freqs, col[:, None] * freqs], axis=-1)
    cos = np.repeat(np.cos(ang), 2, axis=1)
    sin = np.repeat(np.sin(ang), 2, axis=1) * np.tile(np.array([-1.0, 1.0], f32), ATT_DH // 2)
    reps = LANES // ATT_DH
    return jnp.asarray(np.tile(cos, (1, reps)), F32), jnp.asarray(np.tile(sin, (1, reps)), F32)


def kernel(x, norm1_w, w_in, lb_logits, hg_norm_w, q_norm_w, k_norm_w, att_norm_w, w_out, norm2_w,
           w_gate_up, w_down, final_norm_w):
    batch, seq_len, d_model = x.shape
    depth = norm1_w.shape[0]
    rows = batch * seq_len
    assert seq_len % (2 * TK) == 0 and seq_len % TQ == 0
    assert seq_len % T_SCAN == 0 and T_SCAN % T_SEG == 0 and T_SEG % CHUNK == 0 and TK % TM_IN == 0
    assert TM_IN % TQ == 0 and rows % TM_FFN == 0 and w_in.shape[2] == D_IN

    lb_all = jnp.cumsum(jax.nn.softmax(lb_logits.astype(F32), axis=1), axis=1)
    cos_t, sin_t = _rope_tables(seq_len)
    blk = np.arange(ATT_QW) // ATT_DH
    mq = jnp.asarray(np.where(blk[:, None] == blk[None, :], 1.0 / ATT_DH, 0.0), BF16)
    t_idx = np.arange(T_SEG)
    same_chunk = (t_idx[:, None] // CHUNK) == (t_idx[None, :] // CHUNK)
    tri = jnp.asarray(np.stack([same_chunk & (t_idx[:, None] >= t_idx[None, :]),
                                same_chunk & (t_idx[None, :] >= t_idx[:, None])]).astype(np.float32),
                      BF16)
    q_scale = ATT_DH ** -0.5 * math.log2(math.e)

    x2d = x.reshape(rows, d_model)
    row = lambda v: v.astype(F32).reshape(1, -1)
    for l in range(depth):
        hq, gf, kf, gb, kb, iv, sg, qt, k, vt = _inproj(
            x2d, row(norm1_w[l]), w_in[l].astype(F32), row(lb_all[0, l]), row(lb_all[1, l]),
            row(jnp.tile(q_norm_w[l], ATT_HEADS)) * q_scale, row(jnp.tile(k_norm_w[l], ATT_KV)),
            cos_t, sin_t, mq, seq_len)
        o_f, o_b = _hgrn(hq, gf, kf, gb, kb, iv, tri, batch, seq_len)
        att = _attention(qt, k, vt, batch, seq_len)
        x2d = _mix_ffn(o_f, o_b, sg, att, x2d, w_out[l].astype(BF16),
                       row(hg_norm_w[l]), row(att_norm_w[l]), row(norm2_w[l]),
                       w_gate_up[l].astype(BF16), w_down[l].astype(BF16), row(final_norm_w), l == depth - 1)
    return x2d.reshape(batch, seq_len, d_model)
```

```python
import functools
import math

import jax
import jax.numpy as jnp
import numpy as np
from jax import lax
from jax.experimental import pallas as pl
from jax.experimental.pallas import tpu as pltpu

F32 = jnp.float32
BF16 = jnp.bfloat16

EPS = 1e-6
GRID_W = 64
ROPE_THETA = 10000.0

HG_HEADS = 4
HG_D = 128
HG_W = HG_HEADS * HG_D
CHUNK = 64

ATT_HEADS = 8
ATT_KV = 2
ATT_GROUP = ATT_HEADS // ATT_KV
ATT_DH = 64
ATT_QW = ATT_HEADS * ATT_DH
ATT_KVW = ATT_KV * ATT_DH
V_ROWS = ATT_DH + 16

C_Q, C_FF, C_FB, C_I, C_G = 0, HG_W, 2 * HG_W, 3 * HG_W, 4 * HG_W
C_AQ = 5 * HG_W
C_AK = C_AQ + ATT_QW
C_AV = C_AK + ATT_KVW
D_IN = C_AV + ATT_KVW

LANES = 128
VMEM_LIMIT = 56 * 1024 * 1024

TM_IN = 512
TQ = 256
TK = 512
T_SEG = 256
T_SCAN = 1024
TM_FFN = 512
FF_CHUNK = 256
PIECE = 256


def _silu(u):
    hu = 0.5 * u
    return hu + hu * jnp.tanh(hu)


def _dot(a, b):
    return jnp.dot(a, b, preferred_element_type=F32)


def _dot_nt(a, b):
    return lax.dot_general(a, b, (((1,), (1,)), ((), ())), preferred_element_type=F32)


def _dot_tn(a, b):
    return lax.dot_general(a, b, (((0,), (0,)), ((), ())), preferred_element_type=F32)


def _inproj_kernel(x_ref, n1w_ref, w32_ref, lbf_ref, lbb_ref, wq_ref, wk_ref, cos_ref, sin_ref, mq_ref,
                   hq_ref, gf_ref, kf_ref, gb_ref, kb_ref, iv_ref, sg_ref, qt_ref, k_ref, vt_ref, w_ref):
    @pl.when(pl.program_id(0) == 0)
    def _():
        w_ref[...] = w32_ref[...].astype(BF16)

    tm = x_ref.shape[0]
    x = x_ref[...]
    ms = jnp.mean(x * x, axis=-1, keepdims=True)
    h = (x * lax.rsqrt(ms + EPS) * n1w_ref[...]).astype(BF16)

    def proj(lo, hi):
        return _dot(h, w_ref[:, lo:hi])

    def gates(z, lb, g_ref, k_out_ref, cs):
        lb = lb[:, cs]
        hs = 0.5 * (1.0 - lb)
        ht = hs * jnp.tanh(0.5 * z)
        g_ref[:, cs] = jnp.log2((lb + hs) + ht)
        k_out_ref[:, cs] = (hs - ht).astype(BF16)

    cos = cos_ref[...]
    sin = sin_ref[...]
    lane = lax.broadcasted_iota(jnp.int32, (tm, LANES), 1)
    even = (lane & 1) == 0

    def rope(xc):
        partner = jnp.where(even, pltpu.roll(xc, LANES - 1, 1), pltpu.roll(xc, 1, 1))
        return xc * cos + partner * sin

    def head_rms(a, m, w):
        return a * lax.rsqrt(_dot((a * a).astype(BF16), m) + EPS) * w

    def ep_aq(aq, piece, cs):
        yq = head_rms(aq, mq_ref[0:PIECE, 0:PIECE], wq_ref[:, cs])
        for pair in range(PIECE // LANES):
            qct = rope(yq[:, pair * LANES:(pair + 1) * LANES]).T.astype(BF16)
            for e in range(2):
                hl = 2 * pair + e
                for blk in range(tm // TQ):
                    col = (blk * ATT_GROUP + hl) * TQ
                    qt_ref[piece, :, col:col + TQ] = qct[e * ATT_DH:(e + 1) * ATT_DH, blk * TQ:(blk + 1) * TQ]

    def ep_ak(ak, piece, cs):
        yk = rope(head_rms(ak, mq_ref[0:LANES, 0:LANES], wk_ref[...])).astype(BF16)
        k_ref[0] = yk[:, 0:ATT_DH]
        k_ref[1] = yk[:, ATT_DH:2 * ATT_DH]

    def ep_av(av, piece, cs):
        avt = av.T.astype(BF16)
        ones = jnp.ones((V_ROWS - ATT_DH, tm), BF16)
        for g in range(ATT_KV):
            vt_ref[g, 0, 0:ATT_DH, :] = avt[g * ATT_DH:(g + 1) * ATT_DH, :]
            vt_ref[g, 0, ATT_DH:V_ROWS, :] = ones

    def ep_ff(u, piece, cs):
        gates(u, lbf_ref[...], gf_ref, kf_ref, cs)

    def ep_fb(u, piece, cs):
        gates(u, lbb_ref[...], gb_ref, kb_ref, cs)

    def ep_q(u, piece, cs):
        hq_ref[:, cs] = _silu(u).astype(BF16)

    def ep_g(u, piece, cs):
        sg_ref[:, cs] = _silu(u).astype(BF16)

    def ep_i(u, piece, cs):
        iv_ref[:, cs] = u.astype(BF16)

    halves = lambda base, ep: [(base + p * PIECE, PIECE, ep, p) for p in range(HG_W // PIECE)]
    stages = (halves(C_AQ, ep_aq) + [(C_AK, ATT_KVW, ep_ak, 0), (C_AV, ATT_KVW, ep_av, 0)]
              + halves(C_FF, ep_ff) + halves(C_FB, ep_fb) + halves(C_Q, ep_q) + halves(C_G, ep_g)
              + halves(C_I, ep_i))
    pending = None
    for lo, width, epilogue, piece in stages:
        u = proj(lo, lo + width)
        if pending is not None:
            pending()
        pending = functools.partial(epilogue, u, piece, slice(piece * PIECE, (piece + 1) * PIECE))
    pending()


def _inproj(x2d, n1w, w32, lbf, lbb, wq, wk, cos_t, sin_t, mq, seq_len):
    rows = x2d.shape[0]
    tm = TM_IN
    n_tiles = rows // tm
    tiles_per_seq = seq_len // tm
    per_tk = TK // tm
    row_blk = lambda i: (i, 0)
    const = lambda i: (0, 0)
    full = lambda shape: pl.BlockSpec(shape, const)
    act_bf = jax.ShapeDtypeStruct((rows, HG_W), BF16)
    act_f32 = jax.ShapeDtypeStruct((rows, HG_W), F32)
    act_spec = pl.BlockSpec((tm, HG_W), row_blk)
    return pl.pallas_call(
        _inproj_kernel,
        grid=(n_tiles,),
        in_specs=[
            pl.BlockSpec((tm, x2d.shape[1]), row_blk),
            full(n1w.shape), pl.BlockSpec(w32.shape, const, pipeline_mode=pl.Buffered(1)),
            full(lbf.shape), full(lbb.shape),
            full(wq.shape), full(wk.shape),
            pl.BlockSpec((tm, LANES), lambda i: (i % tiles_per_seq, 0)),
            pl.BlockSpec((tm, LANES), lambda i: (i % tiles_per_seq, 0)),
            full(mq.shape),
        ],
        out_specs=[
            act_spec, act_spec, act_spec, act_spec, act_spec, act_spec, act_spec,
            pl.BlockSpec((ATT_KV, ATT_DH, ATT_GROUP * tm), lambda i: (0, 0, i)),
            pl.BlockSpec((ATT_KV, tm, ATT_DH), lambda i: (0, i, 0)),
            pl.BlockSpec((ATT_KV, 1, V_ROWS, tm), lambda i: (0, i // per_tk, 0, i % per_tk)),
        ],
        out_shape=[
            act_bf, act_f32, act_bf, act_f32, act_bf, act_bf, act_bf,
            jax.ShapeDtypeStruct((ATT_KV, ATT_DH, ATT_GROUP * rows), BF16),
            jax.ShapeDtypeStruct((ATT_KV, rows, ATT_DH), BF16),
            jax.ShapeDtypeStruct((ATT_KV, rows // TK, V_ROWS, TK), BF16),
        ],
        scratch_shapes=[pltpu.VMEM(w32.shape, BF16)],
        compiler_params=pltpu.CompilerParams(
            dimension_semantics=("arbitrary",), vmem_limit_bytes=VMEM_LIMIT),
        name="inproj",
    )(x2d, n1w, w32, lbf, lbb, wq, wk, cos_t, sin_t, mq)


def _hgrn_kernel(qf_ref, gf_ref, kf_ref, vf_ref, qb_ref, gb_ref, kb_ref, vb_ref, tri_ref,
                 of_ref, ob_ref, st_ref):
    @pl.when(pl.program_id(1) == 0)
    def _():
        st_ref[...] = jnp.zeros_like(st_ref)

    t_seg = tri_ref.shape[1]
    n_seg = qf_ref.shape[0] // t_seg
    n_chunks = t_seg // CHUNK
    row = lax.broadcasted_iota(jnp.int32, (CHUNK, CHUNK), 0)
    col = lax.broadcasted_iota(jnp.int32, (CHUNK, CHUNK), 1)
    plans = (
        (qf_ref, gf_ref, kf_ref, vf_ref, of_ref, row >= col, CHUNK // 2 - 1, CHUNK - 1, range(n_chunks)),
        (qb_ref, gb_ref, kb_ref, vb_ref, ob_ref, col >= row, CHUNK // 2, 0, range(n_chunks - 1, -1, -1)),
    )

    def sub(a, c, hh):
        return a[c * CHUNK:(c + 1) * CHUNK, hh * HG_D:(hh + 1) * HG_D]

    def stage(sv):
        out = []
        for d, (q_ref, g_ref, k_ref, v_ref, o_ref, keep, ref_row, end_row, order) in enumerate(plans):
            off = (sv if d == 0 else n_seg - 1 - sv) * t_seg
            rs = slice(off, off + t_seg)
            b = _split_dot_lhs(tri_ref[d], g_ref[rs, :])
            b_r = jnp.concatenate(
                [jnp.broadcast_to(b[c * CHUNK + ref_row:c * CHUNK + ref_row + 1, :], (CHUNK, HG_W))
                 for c in range(n_chunks)], axis=0)
            rel = b - b_r
            out.append((off, b, (q_ref[rs, :].astype(F32) * jnp.exp2(rel)).astype(BF16),
                        (k_ref[rs, :].astype(F32) * jnp.exp2(-rel)).astype(BF16)))
        return out

    def local_stage(seg, ci):
        work, scores, contrib = [], [], []
        for d, (q_ref, g_ref, k_ref, v_ref, o_ref, keep, ref_row, end_row, order) in enumerate(plans):
            off, _, q_seg, k_seg = seg[d]
            c = order[ci]
            for hh in range(HG_HEADS):
                q_in, k_in = sub(q_seg, c, hh), sub(k_seg, c, hh)
                v = v_ref[off + c * CHUNK:off + (c + 1) * CHUNK, hh * HG_D:(hh + 1) * HG_D]
                work.append((d, c, hh, q_in, v))
                scores.append(jnp.where(keep, _dot_nt(q_in, k_in), 0.0).astype(BF16))
                contrib.append(_dot_tn(v, k_in))
        return work, scores, contrib

    def state_stage(seg, work, scores, contrib):
        for (d, c, hh, q_in, v), sc, ct in zip(work, scores, contrib):
            o_ref, ref_row, end_row = plans[d][4], plans[d][6], plans[d][7]
            off, b = seg[d][0], seg[d][1]
            cs = slice(hh * HG_D, (hh + 1) * HG_D)
            b_r = b[c * CHUNK + ref_row:c * CHUNK + ref_row + 1, cs]
            b_end = b[c * CHUNK + end_row:c * CHUNK + end_row + 1, cs]
            st = st_ref[d * HG_HEADS + hh]
            lhs = jnp.concatenate([q_in, sc], axis=1)
            rhs = jnp.concatenate([(st * jnp.exp2(b_r)).astype(BF16), v.T], axis=1)
            o_ref[off + c * CHUNK:off + (c + 1) * CHUNK, cs] = _dot_nt(lhs, rhs)
            st_ref[d * HG_HEADS + hh] = st * jnp.exp2(b_end) + ct * jnp.exp2(b_end - b_r)

    seg = stage(0)
    for sv in range(n_seg):
        ahead = local_stage(seg, 0)
        nxt = stage(sv + 1) if sv + 1 < n_seg else None
        for ci in range(n_chunks):
            current, ahead = ahead, (local_stage(seg, ci + 1) if ci + 1 < n_chunks else None)
            state_stage(seg, *current)
        seg = nxt


def _split_dot_lhs(m, a):
    hi = a.astype(BF16)
    lo = (a - hi.astype(F32)).astype(BF16)
    return _dot(m, hi) + _dot(m, lo)


def _hgrn(hq, gf, kf, gb, kb, iv, tri, batch, seq_len):
    rows = hq.shape[0]
    t = T_SCAN
    nb = seq_len // t
    fwd = lambda b, n: (b * nb + n, 0)
    bwd = lambda b, n: (b * nb + nb - 1 - n, 0)
    sf = pl.BlockSpec((t, HG_W), fwd)
    sb = pl.BlockSpec((t, HG_W), bwd)
    out = jax.ShapeDtypeStruct((rows, HG_W), F32)
    return pl.pallas_call(
        _hgrn_kernel,
        grid=(batch, nb),
        in_specs=[sf, sf, sf, sf, sb, sb, sb, sb, pl.BlockSpec(tri.shape, lambda b, n: (0, 0, 0))],
        out_specs=[sf, sb],
        out_shape=[out, out],
        scratch_shapes=[pltpu.VMEM((2 * HG_HEADS, HG_D, HG_D), F32)],
        compiler_params=pltpu.CompilerParams(
            dimension_semantics=("arbitrary", "arbitrary"), vmem_limit_bytes=VMEM_LIMIT),
        name="hgrn_scan",
    )(hq, gf, kf, iv, hq, gb, kb, iv, tri)


def _attn_kernel(q_ref, qn_ref, k_ref, v_ref, o_ref, m_ref, acc_ref, s_ref, bm_ref):
    tq = o_ref.shape[0]
    tk = v_ref.shape[3]
    n_kb = v_ref.shape[1]
    heads = [(g, hh) for g in range(ATT_KV) for hh in range(ATT_GROUP)]

    def scores(qsrc_ref, j, g, hh):
        u = g * ATT_GROUP + hh
        s = _dot(k_ref[g, j * tk:(j + 1) * tk, :],
                 qsrc_ref[g, :, hh * tq:(hh + 1) * tq])
        s_ref[j % 2, u] = s
        bm_ref[j % 2, u] = jnp.max(s, axis=0, keepdims=True)

    def accumulate(j, g, hh):
        u = g * ATT_GROUP + hh
        m_old = m_ref[u]
        m_new = jnp.maximum(m_old, bm_ref[j % 2, u])
        p = jnp.exp2(s_ref[j % 2, u] - m_new).astype(BF16)
        acc_ref[u] = jnp.exp2(m_old - m_new) * acc_ref[u] + _dot(v_ref[g, j], p)
        m_ref[u] = m_new

    @pl.when(pl.program_id(1) == 0)
    def _():
        for g, hh in heads:
            scores(q_ref, 0, g, hh)

    m_ref[...] = jnp.full(m_ref.shape, -1e30, F32)
    acc_ref[...] = jnp.zeros_like(acc_ref)
    for j in range(n_kb):
        for g, hh in heads:
            if j + 1 < n_kb:
                scores(q_ref, j + 1, g, hh)
            else:
                scores(qn_ref, 0, g, hh)
            accumulate(j, g, hh)
    for pair in range(ATT_HEADS // 2):
        ot = jnp.concatenate(
            [acc_ref[2 * pair + e, 0:ATT_DH, :] / acc_ref[2 * pair + e, ATT_DH:ATT_DH + 1, :]
             for e in range(2)], axis=0)
        o_ref[:, pair * LANES:(pair + 1) * LANES] = ot.T


def _attention(qt, k, vt, batch, seq_len):
    rows = k.shape[1]
    nq = seq_len // TQ
    n_kb = seq_len // TK
    q_blk = (ATT_KV, ATT_DH, ATT_GROUP * TQ)
    return pl.pallas_call(
        _attn_kernel,
        grid=(batch, nq),
        in_specs=[
            pl.BlockSpec(q_blk, lambda b, i: (0, 0, b * nq + i)),
            pl.BlockSpec(q_blk, lambda b, i: (0, 0, b * nq + jnp.minimum(i + 1, nq - 1))),
            pl.BlockSpec((ATT_KV, seq_len, ATT_DH), lambda b, i: (0, b, 0)),
            pl.BlockSpec((ATT_KV, n_kb, V_ROWS, TK), lambda b, i: (0, b, 0, 0)),
        ],
        out_specs=pl.BlockSpec((TQ, ATT_QW), lambda b, i: (b * nq + i, 0)),
        out_shape=jax.ShapeDtypeStruct((rows, ATT_QW), F32),
        scratch_shapes=[
            pltpu.VMEM((ATT_HEADS, 1, TQ), F32),
            pltpu.VMEM((ATT_HEADS, V_ROWS, TQ), F32),
            pltpu.VMEM((2, ATT_HEADS, TK, TQ), F32),
            pltpu.VMEM((2, ATT_HEADS, 1, TQ), F32),
        ],
        compiler_params=pltpu.CompilerParams(
            dimension_semantics=("arbitrary", "arbitrary"), vmem_limit_bytes=VMEM_LIMIT),
        name="attention",
    )(qt, qt, k, vt)


def _rms(y, w):
    ms = jnp.mean(y * y, axis=-1, keepdims=True)
    return y * lax.rsqrt(ms + EPS) * w


def _mix_ffn_kernel(of_ref, ob_ref, sg_ref, att_ref, x_ref, wo_ref, hgw_ref, attw_ref, n2w_ref,
                    wgu_ref, wd_ref, fw_ref, o_ref, *, d_ff, final_norm):
    o = of_ref[...] + ob_ref[...]
    hgw = hgw_ref[...]
    o_hg = jnp.concatenate(
        [_rms(o[:, hh * HG_D:(hh + 1) * HG_D], hgw) for hh in range(HG_HEADS)], axis=1)
    o_hg = (o_hg * sg_ref[...].astype(F32)).astype(BF16)
    o_att = _rms(att_ref[...], attw_ref[...]).astype(BF16)
    y = x_ref[...] + _dot(o_hg, wo_ref[0:HG_W, :]) + _dot(o_att, wo_ref[HG_W:HG_W + ATT_QW, :])
    h = _rms(y, n2w_ref[...]).astype(BF16)
    ffn = None
    for c in range(d_ff // FF_CHUNK):
        lo = c * FF_CHUNK
        gate = _dot(h, wgu_ref[:, lo:lo + FF_CHUNK])
        up = _dot(h, wgu_ref[:, d_ff + lo:d_ff + lo + FF_CHUNK])
        act = (_silu(gate) * up).astype(BF16)
        part = _dot(act, wd_ref[lo:lo + FF_CHUNK, :])
        ffn = part if ffn is None else ffn + part
    z = y + ffn
    o_ref[...] = _rms(z, fw_ref[...]) if final_norm else z


def _mix_ffn(o_f, o_b, sg, att, x2d, wo_bf, hgw, attw, n2w, wgu_bf, wd_bf, fw, final_norm):
    rows, d = x2d.shape
    d_ff = wd_bf.shape[0]
    tm = TM_FFN
    row_blk = lambda i: (i, 0)
    const = lambda i: (0, 0)
    half = pl.BlockSpec((tm, HG_W), row_blk)
    wide = pl.BlockSpec((tm, d), row_blk)
    small = lambda a: pl.BlockSpec(a.shape, const)
    resident = lambda a: pl.BlockSpec(a.shape, const, pipeline_mode=pl.Buffered(1))
    return pl.pallas_call(
        functools.partial(_mix_ffn_kernel, d_ff=d_ff, final_norm=final_norm),
        grid=(rows // tm,),
        in_specs=[half, half, half, half, wide, resident(wo_bf), small(hgw), small(attw), small(n2w),
                  resident(wgu_bf), resident(wd_bf), small(fw)],
        out_specs=wide,
        out_shape=jax.ShapeDtypeStruct((rows, d), F32),
        compiler_params=pltpu.CompilerParams(
            dimension_semantics=("arbitrary",), vmem_limit_bytes=VMEM_LIMIT),
        name="mix_ffn",
    )(o_f, o_b, sg, att, x2d, wo_bf, hgw, attw, n2w, wgu_bf, wd_bf, fw)


def _rope_tables(seq_len):
    f32 = np.float32
    rows = seq_len // GRID_W
    row = np.repeat(np.arange(rows), GRID_W).astype(f32)
    col = np.tile(np.arange(GRID_W), rows).astype(f32)
    axis_dim = ATT_DH // 2
    freqs = (f32(ROPE_THETA) ** (-np.arange(0, axis_dim, 2, dtype=f32) / f32(axis_dim))).astype(f32)
    ang = np.concatenate([row[:, None] * freqs, col[:, None] * freqs], axis=-1)
    cos = np.repeat(np.cos(ang), 2, axis=1)
    sin = np.repeat(np.sin(ang), 2, axis=1) * np.tile(np.array([-1.0, 1.0], f32), ATT_DH // 2)
    reps = LANES // ATT_DH
    return jnp.asarray(np.tile(cos, (1, reps)), F32), jnp.asarray(np.tile(sin, (1, reps)), F32)


def kernel(x, norm1_w, w_in, lb_logits, hg_norm_w, q_norm_w, k_norm_w, att_norm_w, w_out, norm2_w,
           w_gate_up, w_down, final_norm_w):
    batch, seq_len, d_model = x.shape
    depth = norm1_w.shape[0]
    rows = batch * seq_len
    assert seq_len % (2 * TK) == 0 and seq_len % TQ == 0
    assert seq_len % T_SCAN == 0 and T_SCAN % T_SEG == 0 and T_SEG % CHUNK == 0 and TK % TM_IN == 0
    assert TM_IN % TQ == 0 and rows % TM_FFN == 0 and w_in.shape[2] == D_IN

    lb_all = jnp.cumsum(jax.nn.softmax(lb_logits.astype(F32), axis=1), axis=1)
    cos_t, sin_t = _rope_tables(seq_len)
    blk = np.arange(ATT_QW) // ATT_DH
    mq = jnp.asarray(np.where(blk[:, None] == blk[None, :], 1.0 / ATT_DH, 0.0), BF16)
    t_idx = np.arange(T_SEG)
    same_chunk = (t_idx[:, None] // CHUNK) == (t_idx[None, :] // CHUNK)
    tri = jnp.asarray(np.stack([same_chunk & (t_idx[:, None] >= t_idx[None, :]),
                                same_chunk & (t_idx[None, :] >= t_idx[:, None])]).astype(np.float32),
                      BF16)
    q_scale = ATT_DH ** -0.5 * math.log2(math.e)

    x2d = x.reshape(rows, d_model)
    row = lambda v: v.astype(F32).reshape(1, -1)
    for l in range(depth):
        hq, gf, kf, gb, kb, iv, sg, qt, k, vt = _inproj(
            x2d, row(norm1_w[l]), w_in[l].astype(F32), row(lb_all[0, l]), row(lb_all[1, l]),
            row(jnp.tile(q_norm_w[l], ATT_HEADS)) * q_scale, row(jnp.tile(k_norm_w[l], ATT_KV)),
            cos_t, sin_t, mq, seq_len)
        o_f, o_b = _hgrn(hq, gf, kf, gb, kb, iv, tri, batch, seq_len)
        att = _attention(qt, k, vt, batch, seq_len)
        x2d = _mix_ffn(o_f, o_b, sg, att, x2d, w_out[l].astype(BF16),
                       row(hg_norm_w[l]), row(att_norm_w[l]), row(norm2_w[l]),
                       w_gate_up[l].astype(BF16), w_down[l].astype(BF16), row(final_norm_w), l == depth - 1)
    return x2d.reshape(batch, seq_len, d_model)
```

```python
import functools
import math

import jax
import jax.numpy as jnp
import numpy as np
from jax import lax
from jax.experimental import pallas as pl
from jax.experimental.pallas import tpu as pltpu

F32 = jnp.float32
BF16 = jnp.bfloat16

EPS = 1e-6
GRID_W = 64
ROPE_THETA = 10000.0

HG_HEADS = 4
HG_D = 128
HG_W = HG_HEADS * HG_D
CHUNK = 64

ATT_HEADS = 8
ATT_KV = 2
ATT_GROUP = ATT_HEADS // ATT_KV
ATT_DH = 64
ATT_QW = ATT_HEADS * ATT_DH
ATT_KVW = ATT_KV * ATT_DH
V_ROWS = ATT_DH + 16

C_Q, C_FF, C_FB, C_I, C_G = 0, HG_W, 2 * HG_W, 3 * HG_W, 4 * HG_W
C_AQ = 5 * HG_W
C_AK = C_AQ + ATT_QW
C_AV = C_AK + ATT_KVW
D_IN = C_AV + ATT_KVW

LANES = 128
VMEM_LIMIT = 56 * 1024 * 1024

TM_IN = 512
TQ = 256
TK = 512
T_SEG = 256
T_SCAN = 1024
TM_FFN = 512
FF_CHUNK = 256
PIECE = 256
WGU_PIECE, WD_PIECE, WO_PIECE = 128, 704, 256


def _silu(u):
    hu = 0.5 * u
    return hu + hu * jnp.tanh(hu)


def _dot(a, b):
    return jnp.dot(a, b, preferred_element_type=F32)


def _dot_nt(a, b):
    return lax.dot_general(a, b, (((1,), (1,)), ((), ())), preferred_element_type=F32)


def _dot_tn(a, b):
    return lax.dot_general(a, b, (((0,), (0,)), ((), ())), preferred_element_type=F32)


def _inproj_kernel(x_ref, n1w_ref, w32_ref, lbf_ref, lbb_ref, wq_ref, wk_ref, cos_ref, sin_ref, mq_ref,
                   hq_ref, gf_ref, kf_ref, gb_ref, kb_ref, iv_ref, sg_ref, qt_ref, k_ref, vt_ref, w_ref):
    @pl.when(pl.program_id(0) == 0)
    def _():
        w_ref[...] = w32_ref[...].astype(BF16)

    tm = x_ref.shape[0]
    x = x_ref[...]
    ms = jnp.mean(x * x, axis=-1, keepdims=True)
    h = (x * lax.rsqrt(ms + EPS) * n1w_ref[...]).astype(BF16)

    def proj(lo, hi):
        return _dot(h, w_ref[:, lo:hi])

    def gates(z, lb, g_ref, k_out_ref, cs):
        lb = lb[:, cs]
        hs = 0.5 * (1.0 - lb)
        ht = hs * jnp.tanh(0.5 * z)
        g_ref[:, cs] = jnp.log2((lb + hs) + ht)
        k_out_ref[:, cs] = (hs - ht).astype(BF16)

    cos = cos_ref[...]
    sin = sin_ref[...]
    lane = lax.broadcasted_iota(jnp.int32, (tm, LANES), 1)
    even = (lane & 1) == 0

    def rope(xc):
        partner = jnp.where(even, pltpu.roll(xc, LANES - 1, 1), pltpu.roll(xc, 1, 1))
        return xc * cos + partner * sin

    def head_rms(a, m, w):
        return a * lax.rsqrt(_dot((a * a).astype(BF16), m) + EPS) * w

    def ep_aq(aq, piece, cs):
        yq = head_rms(aq, mq_ref[0:PIECE, 0:PIECE], wq_ref[:, cs])
        for pair in range(PIECE // LANES):
            qct = rope(yq[:, pair * LANES:(pair + 1) * LANES]).T.astype(BF16)
            for e in range(2):
                hl = 2 * pair + e
                for blk in range(tm // TQ):
                    col = (blk * ATT_GROUP + hl) * TQ
                    qt_ref[piece, :, col:col + TQ] = qct[e * ATT_DH:(e + 1) * ATT_DH, blk * TQ:(blk + 1) * TQ]

    def ep_ak(ak, piece, cs):
        yk = rope(head_rms(ak, mq_ref[0:LANES, 0:LANES], wk_ref[...])).astype(BF16)
        k_ref[0] = yk[:, 0:ATT_DH]
        k_ref[1] = yk[:, ATT_DH:2 * ATT_DH]

    def ep_av(av, piece, cs):
        avt = av.T.astype(BF16)
        ones = jnp.ones((V_ROWS - ATT_DH, tm), BF16)
        for g in range(ATT_KV):
            vt_ref[g, 0, 0:ATT_DH, :] = avt[g * ATT_DH:(g + 1) * ATT_DH, :]
            vt_ref[g, 0, ATT_DH:V_ROWS, :] = ones

    def ep_ff(u, piece, cs):
        gates(u, lbf_ref[...], gf_ref, kf_ref, cs)

    def ep_fb(u, piece, cs):
        gates(u, lbb_ref[...], gb_ref, kb_ref, cs)

    def ep_q(u, piece, cs):
        hq_ref[:, cs] = _silu(u).astype(BF16)

    def ep_g(u, piece, cs):
        sg_ref[:, cs] = _silu(u).astype(BF16)

    def ep_i(u, piece, cs):
        iv_ref[:, cs] = u.astype(BF16)

    halves = lambda base, ep: [(base + p * PIECE, PIECE, ep, p) for p in range(HG_W // PIECE)]
    stages = (halves(C_AQ, ep_aq) + [(C_AK, ATT_KVW, ep_ak, 0), (C_AV, ATT_KVW, ep_av, 0)]
              + halves(C_FF, ep_ff) + halves(C_FB, ep_fb) + halves(C_Q, ep_q) + halves(C_G, ep_g)
              + halves(C_I, ep_i))
    pending = None
    for lo, width, epilogue, piece in stages:
        u = proj(lo, lo + width)
        if pending is not None:
            pending()
        pending = functools.partial(epilogue, u, piece, slice(piece * PIECE, (piece + 1) * PIECE))
    pending()


def _inproj(x2d, n1w, w32, lbf, lbb, wq, wk, cos_t, sin_t, mq, seq_len):
    rows = x2d.shape[0]
    tm = TM_IN
    n_tiles = rows // tm
    tiles_per_seq = seq_len // tm
    per_tk = TK // tm
    row_blk = lambda i: (i, 0)
    const = lambda i: (0, 0)
    full = lambda shape: pl.BlockSpec(shape, const)
    act_bf = jax.ShapeDtypeStruct((rows, HG_W), BF16)
    act_f32 = jax.ShapeDtypeStruct((rows, HG_W), F32)
    act_spec = pl.BlockSpec((tm, HG_W), row_blk)
    return pl.pallas_call(
        _inproj_kernel,
        grid=(n_tiles,),
        in_specs=[
            pl.BlockSpec((tm, x2d.shape[1]), row_blk),
            full(n1w.shape), pl.BlockSpec(w32.shape, const, pipeline_mode=pl.Buffered(1)),
            full(lbf.shape), full(lbb.shape),
            full(wq.shape), full(wk.shape),
            pl.BlockSpec((tm, LANES), lambda i: (i % tiles_per_seq, 0)),
            pl.BlockSpec((tm, LANES), lambda i: (i % tiles_per_seq, 0)),
            full(mq.shape),
        ],
        out_specs=[
            act_spec, act_spec, act_spec, act_spec, act_spec, act_spec, act_spec,
            pl.BlockSpec((ATT_KV, ATT_DH, ATT_GROUP * tm), lambda i: (0, 0, i)),
            pl.BlockSpec((ATT_KV, tm, ATT_DH), lambda i: (0, i, 0)),
            pl.BlockSpec((ATT_KV, 1, V_ROWS, tm), lambda i: (0, i // per_tk, 0, i % per_tk)),
        ],
        out_shape=[
            act_bf, act_f32, act_bf, act_f32, act_bf, act_bf, act_bf,
            jax.ShapeDtypeStruct((ATT_KV, ATT_DH, ATT_GROUP * rows), BF16),
            jax.ShapeDtypeStruct((ATT_KV, rows, ATT_DH), BF16),
            jax.ShapeDtypeStruct((ATT_KV, rows // TK, V_ROWS, TK), BF16),
        ],
        scratch_shapes=[pltpu.VMEM(w32.shape, BF16)],
        compiler_params=pltpu.CompilerParams(
            dimension_semantics=("arbitrary",), vmem_limit_bytes=VMEM_LIMIT),
        name="inproj",
    )(x2d, n1w, w32, lbf, lbb, wq, wk, cos_t, sin_t, mq)


def _hgrn_kernel(qf_ref, gf_ref, kf_ref, vf_ref, qb_ref, gb_ref, kb_ref, vb_ref, tri_ref,
                 of_ref, ob_ref, st_ref):
    @pl.when(pl.program_id(1) == 0)
    def _():
        st_ref[...] = jnp.zeros_like(st_ref)

    t_seg = tri_ref.shape[1]
    n_seg = qf_ref.shape[0] // t_seg
    n_chunks = t_seg // CHUNK
    row = lax.broadcasted_iota(jnp.int32, (CHUNK, CHUNK), 0)
    col = lax.broadcasted_iota(jnp.int32, (CHUNK, CHUNK), 1)
    plans = (
        (qf_ref, gf_ref, kf_ref, vf_ref, of_ref, row >= col, CHUNK // 2 - 1, CHUNK - 1, range(n_chunks)),
        (qb_ref, gb_ref, kb_ref, vb_ref, ob_ref, col >= row, CHUNK // 2, 0, range(n_chunks - 1, -1, -1)),
    )

    def sub(a, c, hh):
        return a[c * CHUNK:(c + 1) * CHUNK, hh * HG_D:(hh + 1) * HG_D]

    def stage(sv):
        out = []
        for d, (q_ref, g_ref, k_ref, v_ref, o_ref, keep, ref_row, end_row, order) in enumerate(plans):
            off = (sv if d == 0 else n_seg - 1 - sv) * t_seg
            rs = slice(off, off + t_seg)
            b = _split_dot_lhs(tri_ref[d], g_ref[rs, :])
            b_r = jnp.concatenate(
                [jnp.broadcast_to(b[c * CHUNK + ref_row:c * CHUNK + ref_row + 1, :], (CHUNK, HG_W))
                 for c in range(n_chunks)], axis=0)
            rel = b - b_r
            out.append((off, b, (q_ref[rs, :].astype(F32) * jnp.exp2(rel)).astype(BF16),
                        (k_ref[rs, :].astype(F32) * jnp.exp2(-rel)).astype(BF16)))
        return out

    def local_stage(seg, ci):
        work, scores, contrib = [], [], []
        for d, (q_ref, g_ref, k_ref, v_ref, o_ref, keep, ref_row, end_row, order) in enumerate(plans):
            off, _, q_seg, k_seg = seg[d]
            c = order[ci]
            for hh in range(HG_HEADS):
                q_in, k_in = sub(q_seg, c, hh), sub(k_seg, c, hh)
                v = v_ref[off + c * CHUNK:off + (c + 1) * CHUNK, hh * HG_D:(hh + 1) * HG_D]
                work.append((d, c, hh, q_in, v))
                scores.append(jnp.where(keep, _dot_nt(q_in, k_in), 0.0).astype(BF16))
                contrib.append(_dot_tn(v, k_in))
        return work, scores, contrib

    def state_stage(seg, work, scores, contrib):
        for (d, c, hh, q_in, v), sc, ct in zip(work, scores, contrib):
            o_ref, ref_row, end_row = plans[d][4], plans[d][6], plans[d][7]
            off, b = seg[d][0], seg[d][1]
            cs = slice(hh * HG_D, (hh + 1) * HG_D)
            b_r = b[c * CHUNK + ref_row:c * CHUNK + ref_row + 1, cs]
            b_end = b[c * CHUNK + end_row:c * CHUNK + end_row + 1, cs]
            st = st_ref[d * HG_HEADS + hh]
            lhs = jnp.concatenate([q_in, sc], axis=1)
            rhs = jnp.concatenate([(st * jnp.exp2(b_r)).astype(BF16), v.T], axis=1)
            o_ref[off + c * CHUNK:off + (c + 1) * CHUNK, cs] = _dot_nt(lhs, rhs)
            st_ref[d * HG_HEADS + hh] = st * jnp.exp2(b_end) + ct * jnp.exp2(b_end - b_r)

    seg = stage(0)
    for sv in range(n_seg):
        ahead = local_stage(seg, 0)
        nxt = stage(sv + 1) if sv + 1 < n_seg else None
        for ci in range(n_chunks):
            current, ahead = ahead, (local_stage(seg, ci + 1) if ci + 1 < n_chunks else None)
            state_stage(seg, *current)
        seg = nxt


def _split_dot_lhs(m, a):
    hi = a.astype(BF16)
    lo = (a - hi.astype(F32)).astype(BF16)
    return _dot(m, hi) + _dot(m, lo)


def _hgrn(hq, gf, kf, gb, kb, iv, tri, batch, seq_len):
    rows = hq.shape[0]
    t = T_SCAN
    nb = seq_len // t
    fwd = lambda b, n: (b * nb + n, 0)
    bwd = lambda b, n: (b * nb + nb - 1 - n, 0)
    sf = pl.BlockSpec((t, HG_W), fwd)
    sb = pl.BlockSpec((t, HG_W), bwd)
    out = jax.ShapeDtypeStruct((rows, HG_W), F32)
    return pl.pallas_call(
        _hgrn_kernel,
        grid=(batch, nb),
        in_specs=[sf, sf, sf, sf, sb, sb, sb, sb, pl.BlockSpec(tri.shape, lambda b, n: (0, 0, 0))],
        out_specs=[sf, sb],
        out_shape=[out, out],
        scratch_shapes=[pltpu.VMEM((2 * HG_HEADS, HG_D, HG_D), F32)],
        compiler_params=pltpu.CompilerParams(
            dimension_semantics=("arbitrary", "arbitrary"), vmem_limit_bytes=VMEM_LIMIT),
        name="hgrn_scan",
    )(hq, gf, kf, iv, hq, gb, kb, iv, tri)


def _attn_kernel(q_ref, qn_ref, k_ref, v_ref, o_ref, m_ref, acc_ref, s_ref, bm_ref):
    tq = o_ref.shape[0]
    tk = v_ref.shape[3]
    n_kb = v_ref.shape[1]
    heads = [(g, hh) for g in range(ATT_KV) for hh in range(ATT_GROUP)]

    def scores(qsrc_ref, j, g, hh):
        u = g * ATT_GROUP + hh
        s = _dot(k_ref[g, j * tk:(j + 1) * tk, :],
                 qsrc_ref[g, :, hh * tq:(hh + 1) * tq])
        s_ref[j % 2, u] = s
        bm_ref[j % 2, u] = jnp.max(s, axis=0, keepdims=True)

    def accumulate(j, g, hh):
        u = g * ATT_GROUP + hh
        m_old = m_ref[u]
        m_new = jnp.maximum(m_old, bm_ref[j % 2, u])
        p = jnp.exp2(s_ref[j % 2, u] - m_new).astype(BF16)
        acc_ref[u] = jnp.exp2(m_old - m_new) * acc_ref[u] + _dot(v_ref[g, j], p)
        m_ref[u] = m_new

    @pl.when(pl.program_id(1) == 0)
    def _():
        for g, hh in heads:
            scores(q_ref, 0, g, hh)

    m_ref[...] = jnp.full(m_ref.shape, -1e30, F32)
    acc_ref[...] = jnp.zeros_like(acc_ref)
    for j in range(n_kb):
        for g, hh in heads:
            if j + 1 < n_kb:
                scores(q_ref, j + 1, g, hh)
            else:
                scores(qn_ref, 0, g, hh)
            accumulate(j, g, hh)
    for pair in range(ATT_HEADS // 2):
        ot = jnp.concatenate(
            [acc_ref[2 * pair + e, 0:ATT_DH, :] / acc_ref[2 * pair + e, ATT_DH:ATT_DH + 1, :]
             for e in range(2)], axis=0)
        o_ref[:, pair * LANES:(pair + 1) * LANES] = ot.T


def _attention(qt, k, vt, batch, seq_len):
    rows = k.shape[1]
    nq = seq_len // TQ
    n_kb = seq_len // TK
    q_blk = (ATT_KV, ATT_DH, ATT_GROUP * TQ)
    return pl.pallas_call(
        _attn_kernel,
        grid=(batch, nq),
        in_specs=[
            pl.BlockSpec(q_blk, lambda b, i: (0, 0, b * nq + i)),
            pl.BlockSpec(q_blk, lambda b, i: (0, 0, b * nq + jnp.minimum(i + 1, nq - 1))),
            pl.BlockSpec((ATT_KV, seq_len, ATT_DH), lambda b, i: (0, b, 0)),
            pl.BlockSpec((ATT_KV, n_kb, V_ROWS, TK), lambda b, i: (0, b, 0, 0)),
        ],
        out_specs=pl.BlockSpec((TQ, ATT_QW), lambda b, i: (b * nq + i, 0)),
        out_shape=jax.ShapeDtypeStruct((rows, ATT_QW), F32),
        scratch_shapes=[
            pltpu.VMEM((ATT_HEADS, 1, TQ), F32),
            pltpu.VMEM((ATT_HEADS, V_ROWS, TQ), F32),
            pltpu.VMEM((2, ATT_HEADS, TK, TQ), F32),
            pltpu.VMEM((2, ATT_HEADS, 1, TQ), F32),
        ],
        compiler_params=pltpu.CompilerParams(
            dimension_semantics=("arbitrary", "arbitrary"), vmem_limit_bytes=VMEM_LIMIT),
        name="attention",
    )(qt, qt, k, vt)


def _rms(y, w):
    ms = jnp.mean(y * y, axis=-1, keepdims=True)
    return y * lax.rsqrt(ms + EPS) * w


def _mix_ffn_kernel(of_ref, ob_ref, sg_ref, att_ref, x_ref, wo32_ref, hgw_ref, attw_ref, n2w_ref,
                    wgu32_ref, wd32_ref, fw_ref, o_ref, wo_ref, wgu_ref, wd_ref, *, d_ff, final_norm, phases):
    i = pl.program_id(0)
    start = 0
    for src_ref, dst_ref, n_pieces in ((wgu32_ref, wgu_ref, phases[0]), (wd32_ref, wd_ref, phases[1]),
                                       (wo32_ref, wo_ref, phases[2])):
        rows = src_ref.shape[0]

        @pl.when((i >= start) & (i < start + n_pieces))
        def _(src_ref=src_ref, dst_ref=dst_ref, rows=rows, start=start):
            r0 = pl.multiple_of((i - start) * rows, 16)
            dst_ref[pl.ds(r0, rows), :] = src_ref[...].astype(BF16)

        start += n_pieces

    @pl.when(i >= start)
    def _():
        o = of_ref[...] + ob_ref[...]
        hgw = hgw_ref[...]
        o_hg = jnp.concatenate(
            [_rms(o[:, hh * HG_D:(hh + 1) * HG_D], hgw) for hh in range(HG_HEADS)], axis=1)
        o_hg = (o_hg * sg_ref[...].astype(F32)).astype(BF16)
        o_att = _rms(att_ref[...], attw_ref[...]).astype(BF16)
        y = x_ref[...] + _dot(o_hg, wo_ref[0:HG_W, :]) + _dot(o_att, wo_ref[HG_W:HG_W + ATT_QW, :])
        h = _rms(y, n2w_ref[...]).astype(BF16)
        ffn = None
        for c in range(d_ff // FF_CHUNK):
            lo = c * FF_CHUNK
            gate = _dot(h, wgu_ref[:, lo:lo + FF_CHUNK])
            up = _dot(h, wgu_ref[:, d_ff + lo:d_ff + lo + FF_CHUNK])
            act = (_silu(gate) * up).astype(BF16)
            part = _dot(act, wd_ref[lo:lo + FF_CHUNK, :])
            ffn = part if ffn is None else ffn + part
        z = y + ffn
        o_ref[...] = _rms(z, fw_ref[...]) if final_norm else z


def _mix_ffn(o_f, o_b, sg, att, x2d, wo32, hgw, attw, n2w, wgu32, wd32, fw, final_norm):
    rows, d = x2d.shape
    d_ff = wd32.shape[0]
    tm = TM_FFN
    phases = (wgu32.shape[0] // WGU_PIECE, wd32.shape[0] // WD_PIECE, wo32.shape[0] // WO_PIECE)
    n_load = sum(phases)
    tile = lambda i: (jnp.maximum(i - n_load, 0), 0)
    const = lambda i: (0, 0)
    half = pl.BlockSpec((tm, HG_W), tile)
    wide = pl.BlockSpec((tm, d), tile)
    small = lambda a: pl.BlockSpec(a.shape, const)

    def pieces(a, piece_rows, first, count):
        return pl.BlockSpec((piece_rows, a.shape[1]), lambda i: (jnp.clip(i - first, 0, count - 1), 0))

    return pl.pallas_call(
        functools.partial(_mix_ffn_kernel, d_ff=d_ff, final_norm=final_norm, phases=phases),
        grid=(n_load + rows // tm,),
        in_specs=[half, half, half, half, wide,
                  pieces(wo32, WO_PIECE, phases[0] + phases[1], phases[2]),
                  small(hgw), small(attw), small(n2w),
                  pieces(wgu32, WGU_PIECE, 0, phases[0]), pieces(wd32, WD_PIECE, phases[0], phases[1]),
                  small(fw)],
        out_specs=wide,
        out_shape=jax.ShapeDtypeStruct((rows, d), F32),
        scratch_shapes=[pltpu.VMEM(wo32.shape, BF16), pltpu.VMEM(wgu32.shape, BF16),
                        pltpu.VMEM(wd32.shape, BF16)],
        compiler_params=pltpu.CompilerParams(
            dimension_semantics=("arbitrary",), vmem_limit_bytes=VMEM_LIMIT),
        name="mix_ffn",
    )(o_f, o_b, sg, att, x2d, wo32, hgw, attw, n2w, wgu32, wd32, fw)


def _rope_tables(seq_len):
    f32 = np.float32
    rows = seq_len // GRID_W
    row = np.repeat(np.arange(rows), GRID_W).astype(f32)
    col = np.tile(np.arange(GRID_W), rows).astype(f32)
    axis_dim = ATT_DH // 2
    freqs = (f32(ROPE_THETA) ** (-np.arange(0, axis_dim, 2, dtype=f32) / f32(axis_dim))).astype(f32)
    ang = np.concatenate([row[:, None] * freqs, col[:, None] * freqs], axis=-1)
    cos = np.repeat(np.cos(ang), 2, axis=1)
    sin = np.repeat(np.sin(ang), 2, axis=1) * np.tile(np.array([-1.0, 1.0], f32), ATT_DH // 2)
    reps = LANES // ATT_DH
    return jnp.asarray(np.tile(cos, (1, reps)), F32), jnp.asarray(np.tile(sin, (1, reps)), F32)


def kernel(x, norm1_w, w_in, lb_logits, hg_norm_w, q_norm_w, k_norm_w, att_norm_w, w_out, norm2_w,
           w_gate_up, w_down, final_norm_w):
    batch, seq_len, d_model = x.shape
    depth = norm1_w.shape[0]
    rows = batch * seq_len
    assert seq_len % (2 * TK) == 0 and seq_len % TQ == 0
    assert seq_len % T_SCAN == 0 and T_SCAN % T_SEG == 0 and T_SEG % CHUNK == 0 and TK % TM_IN == 0
    assert TM_IN % TQ == 0 and rows % TM_FFN == 0 and w_in.shape[2] == D_IN

    lb_all = jnp.cumsum(jax.nn.softmax(lb_logits.astype(F32), axis=1), axis=1)
    cos_t, sin_t = _rope_tables(seq_len)
    blk = np.arange(ATT_QW) // ATT_DH
    mq = jnp.asarray(np.where(blk[:, None] == blk[None, :], 1.0 / ATT_DH, 0.0), BF16)
    t_idx = np.arange(T_SEG)
    same_chunk = (t_idx[:, None] // CHUNK) == (t_idx[None, :] // CHUNK)
    tri = jnp.asarray(np.stack([same_chunk & (t_idx[:, None] >= t_idx[None, :]),
                                same_chunk & (t_idx[None, :] >= t_idx[:, None])]).astype(np.float32),
                      BF16)
    q_scale = ATT_DH ** -0.5 * math.log2(math.e)

    x2d = x.reshape(rows, d_model)
    row = lambda v: v.astype(F32).reshape(1, -1)
    for l in range(depth):
        hq, gf, kf, gb, kb, iv, sg, qt, k, vt = _inproj(
            x2d, row(norm1_w[l]), w_in[l].astype(F32), row(lb_all[0, l]), row(lb_all[1, l]),
            row(jnp.tile(q_norm_w[l], ATT_HEADS)) * q_scale, row(jnp.tile(k_norm_w[l], ATT_KV)),
            cos_t, sin_t, mq, seq_len)
        o_f, o_b = _hgrn(hq, gf, kf, gb, kb, iv, tri, batch, seq_len)
        att = _attention(qt, k, vt, batch, seq_len)
        x2d = _mix_ffn(o_f, o_b, sg, att, x2d, w_out[l].astype(F32),
                       row(hg_norm_w[l]), row(att_norm_w[l]), row(norm2_w[l]),
                       w_gate_up[l].astype(F32), w_down[l].astype(F32), row(final_norm_w), l == depth - 1)
    return x2d.reshape(batch, seq_len, d_model)
```

```python
import functools
import math

import jax
import jax.numpy as jnp
import numpy as np
from jax import lax
from jax.experimental import pallas as pl
from jax.experimental.pallas import tpu as pltpu

F32 = jnp.float32
BF16 = jnp.bfloat16

EPS = 1e-6
GRID_W = 64
ROPE_THETA = 10000.0

HG_HEADS = 4
HG_D = 128
HG_W = HG_HEADS * HG_D
CHUNK = 64

ATT_HEADS = 8
ATT_KV = 2
ATT_GROUP = ATT_HEADS // ATT_KV
ATT_DH = 64
ATT_QW = ATT_HEADS * ATT_DH
ATT_KVW = ATT_KV * ATT_DH
V_ROWS = ATT_DH + 16

C_Q, C_FF, C_FB, C_I, C_G = 0, HG_W, 2 * HG_W, 3 * HG_W, 4 * HG_W
C_AQ = 5 * HG_W
C_AK = C_AQ + ATT_QW
C_AV = C_AK + ATT_KVW
D_IN = C_AV + ATT_KVW

LANES = 128
VMEM_LIMIT = 56 * 1024 * 1024

TM_IN = 512
TQ = 512
TK = 512
T_SEG = 256
T_SCAN = 1024
TM_FFN = 512
FF_CHUNK = 256
PIECE = 256
WGU_PIECE, WD_PIECE, WO_PIECE = 128, 704, 256


def _silu(u):
    hu = 0.5 * u
    return hu + hu * jnp.tanh(hu)


def _dot(a, b):
    return jnp.dot(a, b, preferred_element_type=F32)


def _dot_nt(a, b):
    return lax.dot_general(a, b, (((1,), (1,)), ((), ())), preferred_element_type=F32)


def _dot_tn(a, b):
    return lax.dot_general(a, b, (((0,), (0,)), ((), ())), preferred_element_type=F32)


def _inproj_kernel(x_ref, n1w_ref, w32_ref, lbf_ref, lbb_ref, wq_ref, wk_ref, cos_ref, sin_ref, mq_ref,
                   hq_ref, gf_ref, kf_ref, gb_ref, kb_ref, iv_ref, sg_ref, qt_ref, k_ref, vt_ref, w_ref):
    @pl.when(pl.program_id(0) == 0)
    def _():
        w_ref[...] = w32_ref[...].astype(BF16)

    tm = x_ref.shape[0]
    x = x_ref[...]
    ms = jnp.mean(x * x, axis=-1, keepdims=True)
    h = (x * lax.rsqrt(ms + EPS) * n1w_ref[...]).astype(BF16)

    def proj(lo, hi):
        return _dot(h, w_ref[:, lo:hi])

    def gates(z, lb, g_ref, k_out_ref, cs):
        lb = lb[:, cs]
        hs = 0.5 * (1.0 - lb)
        ht = hs * jnp.tanh(0.5 * z)
        g_ref[:, cs] = jnp.log2((lb + hs) + ht)
        k_out_ref[:, cs] = (hs - ht).astype(BF16)

    cos = cos_ref[...]
    sin = sin_ref[...]
    lane = lax.broadcasted_iota(jnp.int32, (tm, LANES), 1)
    even = (lane & 1) == 0

    def rope(xc):
        partner = jnp.where(even, pltpu.roll(xc, LANES - 1, 1), pltpu.roll(xc, 1, 1))
        return xc * cos + partner * sin

    def head_rms(a, m, w):
        return a * lax.rsqrt(_dot((a * a).astype(BF16), m) + EPS) * w

    def ep_aq(aq, piece, cs):
        yq = head_rms(aq, mq_ref[0:PIECE, 0:PIECE], wq_ref[:, cs])
        for pair in range(PIECE // LANES):
            qct = rope(yq[:, pair * LANES:(pair + 1) * LANES]).T.astype(BF16)
            for e in range(2):
                hl = 2 * pair + e
                for blk in range(tm // TQ):
                    col = (blk * ATT_GROUP + hl) * TQ
                    qt_ref[piece, :, col:col + TQ] = qct[e * ATT_DH:(e + 1) * ATT_DH, blk * TQ:(blk + 1) * TQ]

    def ep_ak(ak, piece, cs):
        yk = rope(head_rms(ak, mq_ref[0:LANES, 0:LANES], wk_ref[...])).astype(BF16)
        k_ref[0] = yk[:, 0:ATT_DH]
        k_ref[1] = yk[:, ATT_DH:2 * ATT_DH]

    def ep_av(av, piece, cs):
        avt = av.T.astype(BF16)
        ones = jnp.ones((V_ROWS - ATT_DH, tm), BF16)
        for g in range(ATT_KV):
            vt_ref[g, 0, 0:ATT_DH, :] = avt[g * ATT_DH:(g + 1) * ATT_DH, :]
            vt_ref[g, 0, ATT_DH:V_ROWS, :] = ones

    def ep_ff(u, piece, cs):
        gates(u, lbf_ref[...], gf_ref, kf_ref, cs)

    def ep_fb(u, piece, cs):
        gates(u, lbb_ref[...], gb_ref, kb_ref, cs)

    def ep_q(u, piece, cs):
        hq_ref[:, cs] = _silu(u).astype(BF16)

    def ep_g(u, piece, cs):
        sg_ref[:, cs] = _silu(u).astype(BF16)

    def ep_i(u, piece, cs):
        iv_ref[:, cs] = u.astype(BF16)

    halves = lambda base, ep: [(base + p * PIECE, PIECE, ep, p) for p in range(HG_W // PIECE)]
    stages = (halves(C_AQ, ep_aq) + [(C_AK, ATT_KVW, ep_ak, 0), (C_AV, ATT_KVW, ep_av, 0)]
              + halves(C_FF, ep_ff) + halves(C_FB, ep_fb) + halves(C_Q, ep_q) + halves(C_G, ep_g)
              + halves(C_I, ep_i))
    pending = None
    for lo, width, epilogue, piece in stages:
        u = proj(lo, lo + width)
        if pending is not None:
            pending()
        pending = functools.partial(epilogue, u, piece, slice(piece * PIECE, (piece + 1) * PIECE))
    pending()


def _inproj(x2d, n1w, w32, lbf, lbb, wq, wk, cos_t, sin_t, mq, seq_len):
    rows = x2d.shape[0]
    tm = TM_IN
    n_tiles = rows // tm
    tiles_per_seq = seq_len // tm
    per_tk = TK // tm
    row_blk = lambda i: (i, 0)
    const = lambda i: (0, 0)
    full = lambda shape: pl.BlockSpec(shape, const)
    act_bf = jax.ShapeDtypeStruct((rows, HG_W), BF16)
    act_f32 = jax.ShapeDtypeStruct((rows, HG_W), F32)
    act_spec = pl.BlockSpec((tm, HG_W), row_blk)
    return pl.pallas_call(
        _inproj_kernel,
        grid=(n_tiles,),
        in_specs=[
            pl.BlockSpec((tm, x2d.shape[1]), row_blk),
            full(n1w.shape), pl.BlockSpec(w32.shape, const, pipeline_mode=pl.Buffered(1)),
            full(lbf.shape), full(lbb.shape),
            full(wq.shape), full(wk.shape),
            pl.BlockSpec((tm, LANES), lambda i: (i % tiles_per_seq, 0)),
            pl.BlockSpec((tm, LANES), lambda i: (i % tiles_per_seq, 0)),
            full(mq.shape),
        ],
        out_specs=[
            act_spec, act_spec, act_spec, act_spec, act_spec, act_spec, act_spec,
            pl.BlockSpec((ATT_KV, ATT_DH, ATT_GROUP * tm), lambda i: (0, 0, i)),
            pl.BlockSpec((ATT_KV, tm, ATT_DH), lambda i: (0, i, 0)),
            pl.BlockSpec((ATT_KV, 1, V_ROWS, tm), lambda i: (0, i // per_tk, 0, i % per_tk)),
        ],
        out_shape=[
            act_bf, act_f32, act_bf, act_f32, act_bf, act_bf, act_bf,
            jax.ShapeDtypeStruct((ATT_KV, ATT_DH, ATT_GROUP * rows), BF16),
            jax.ShapeDtypeStruct((ATT_KV, rows, ATT_DH), BF16),
            jax.ShapeDtypeStruct((ATT_KV, rows // TK, V_ROWS, TK), BF16),
        ],
        scratch_shapes=[pltpu.VMEM(w32.shape, BF16)],
        compiler_params=pltpu.CompilerParams(
            dimension_semantics=("arbitrary",), vmem_limit_bytes=VMEM_LIMIT),
        name="inproj",
    )(x2d, n1w, w32, lbf, lbb, wq, wk, cos_t, sin_t, mq)


def _hgrn_kernel(qf_ref, gf_ref, kf_ref, vf_ref, qb_ref, gb_ref, kb_ref, vb_ref, tri_ref,
                 of_ref, ob_ref, st_ref):
    @pl.when(pl.program_id(1) == 0)
    def _():
        st_ref[...] = jnp.zeros_like(st_ref)

    t_seg = tri_ref.shape[1]
    n_seg = qf_ref.shape[0] // t_seg
    n_chunks = t_seg // CHUNK
    row = lax.broadcasted_iota(jnp.int32, (CHUNK, CHUNK), 0)
    col = lax.broadcasted_iota(jnp.int32, (CHUNK, CHUNK), 1)
    plans = (
        (qf_ref, gf_ref, kf_ref, vf_ref, of_ref, row >= col, CHUNK // 2 - 1, CHUNK - 1, range(n_chunks)),
        (qb_ref, gb_ref, kb_ref, vb_ref, ob_ref, col >= row, CHUNK // 2, 0, range(n_chunks - 1, -1, -1)),
    )

    def sub(a, c, hh):
        return a[c * CHUNK:(c + 1) * CHUNK, hh * HG_D:(hh + 1) * HG_D]

    def stage(sv):
        out = []
        for d, (q_ref, g_ref, k_ref, v_ref, o_ref, keep, ref_row, end_row, order) in enumerate(plans):
            off = (sv if d == 0 else n_seg - 1 - sv) * t_seg
            rs = slice(off, off + t_seg)
            b = _split_dot_lhs(tri_ref[d], g_ref[rs, :])
            b_r = jnp.concatenate(
                [jnp.broadcast_to(b[c * CHUNK + ref_row:c * CHUNK + ref_row + 1, :], (CHUNK, HG_W))
                 for c in range(n_chunks)], axis=0)
            rel = b - b_r
            out.append((off, b, (q_ref[rs, :].astype(F32) * jnp.exp2(rel)).astype(BF16),
                        (k_ref[rs, :].astype(F32) * jnp.exp2(-rel)).astype(BF16)))
        return out

    def local_stage(seg, ci):
        work, scores, contrib = [], [], []
        for d, (q_ref, g_ref, k_ref, v_ref, o_ref, keep, ref_row, end_row, order) in enumerate(plans):
            off, _, q_seg, k_seg = seg[d]
            c = order[ci]
            for hh in range(HG_HEADS):
                q_in, k_in = sub(q_seg, c, hh), sub(k_seg, c, hh)
                v = v_ref[off + c * CHUNK:off + (c + 1) * CHUNK, hh * HG_D:(hh + 1) * HG_D]
                work.append((d, c, hh, q_in, v))
                scores.append(jnp.where(keep, _dot_nt(q_in, k_in), 0.0).astype(BF16))
                contrib.append(_dot_tn(v, k_in))
        return work, scores, contrib

    def state_stage(seg, work, scores, contrib):
        for (d, c, hh, q_in, v), sc, ct in zip(work, scores, contrib):
            o_ref, ref_row, end_row = plans[d][4], plans[d][6], plans[d][7]
            off, b = seg[d][0], seg[d][1]
            cs = slice(hh * HG_D, (hh + 1) * HG_D)
            b_r = b[c * CHUNK + ref_row:c * CHUNK + ref_row + 1, cs]
            b_end = b[c * CHUNK + end_row:c * CHUNK + end_row + 1, cs]
            st = st_ref[d * HG_HEADS + hh]
            lhs = jnp.concatenate([q_in, sc], axis=1)
            rhs = jnp.concatenate([(st * jnp.exp2(b_r)).astype(BF16), v.T], axis=1)
            o_ref[off + c * CHUNK:off + (c + 1) * CHUNK, cs] = _dot_nt(lhs, rhs)
            st_ref[d * HG_HEADS + hh] = st * jnp.exp2(b_end) + ct * jnp.exp2(b_end - b_r)

    seg = stage(0)
    for sv in range(n_seg):
        ahead = local_stage(seg, 0)
        nxt = stage(sv + 1) if sv + 1 < n_seg else None
        for ci in range(n_chunks):
            current, ahead = ahead, (local_stage(seg, ci + 1) if ci + 1 < n_chunks else None)
            state_stage(seg, *current)
        seg = nxt


def _split_dot_lhs(m, a):
    hi = a.astype(BF16)
    lo = (a - hi.astype(F32)).astype(BF16)
    return _dot(m, hi) + _dot(m, lo)


def _hgrn(hq, gf, kf, gb, kb, iv, tri, batch, seq_len):
    rows = hq.shape[0]
    t = T_SCAN
    nb = seq_len // t
    fwd = lambda b, n: (b * nb + n, 0)
    bwd = lambda b, n: (b * nb + nb - 1 - n, 0)
    sf = pl.BlockSpec((t, HG_W), fwd)
    sb = pl.BlockSpec((t, HG_W), bwd)
    out = jax.ShapeDtypeStruct((rows, HG_W), F32)
    return pl.pallas_call(
        _hgrn_kernel,
        grid=(batch, nb),
        in_specs=[sf, sf, sf, sf, sb, sb, sb, sb, pl.BlockSpec(tri.shape, lambda b, n: (0, 0, 0))],
        out_specs=[sf, sb],
        out_shape=[out, out],
        scratch_shapes=[pltpu.VMEM((2 * HG_HEADS, HG_D, HG_D), F32)],
        compiler_params=pltpu.CompilerParams(
            dimension_semantics=("arbitrary", "arbitrary"), vmem_limit_bytes=VMEM_LIMIT),
        name="hgrn_scan",
    )(hq, gf, kf, iv, hq, gb, kb, iv, tri)


def _attn_kernel(q_ref, qn_ref, k_ref, v_ref, o_ref, m_ref, acc_ref, s_ref, bm_ref):
    tq = o_ref.shape[0]
    tk = v_ref.shape[3]
    n_kb = v_ref.shape[1]
    heads = [(g, hh) for g in range(ATT_KV) for hh in range(ATT_GROUP)]

    def scores(qsrc_ref, j, g, hh):
        u = g * ATT_GROUP + hh
        s = _dot(k_ref[g, j * tk:(j + 1) * tk, :],
                 qsrc_ref[g, :, hh * tq:(hh + 1) * tq])
        s_ref[j % 2, u] = s
        bm_ref[j % 2, u] = jnp.max(s, axis=0, keepdims=True)

    def accumulate(j, g, hh):
        u = g * ATT_GROUP + hh
        m_old = m_ref[u]
        m_new = jnp.maximum(m_old, bm_ref[j % 2, u])
        p = jnp.exp2(s_ref[j % 2, u] - m_new).astype(BF16)
        acc_ref[u] = jnp.exp2(m_old - m_new) * acc_ref[u] + _dot(v_ref[g, j], p)
        m_ref[u] = m_new

    @pl.when(pl.program_id(1) == 0)
    def _():
        for g, hh in heads:
            scores(q_ref, 0, g, hh)

    m_ref[...] = jnp.full(m_ref.shape, -1e30, F32)
    acc_ref[...] = jnp.zeros_like(acc_ref)
    for j in range(n_kb):
        for g, hh in heads:
            if j + 1 < n_kb:
                scores(q_ref, j + 1, g, hh)
            else:
                scores(qn_ref, 0, g, hh)
            accumulate(j, g, hh)
    for pair in range(ATT_HEADS // 2):
        ot = jnp.concatenate(
            [acc_ref[2 * pair + e, 0:ATT_DH, :] / acc_ref[2 * pair + e, ATT_DH:ATT_DH + 1, :]
             for e in range(2)], axis=0)
        o_ref[:, pair * LANES:(pair + 1) * LANES] = ot.T


def _attention(qt, k, vt, batch, seq_len):
    rows = k.shape[1]
    nq = seq_len // TQ
    n_kb = seq_len // TK
    q_blk = (ATT_KV, ATT_DH, ATT_GROUP * TQ)
    return pl.pallas_call(
        _attn_kernel,
        grid=(batch, nq),
        in_specs=[
            pl.BlockSpec(q_blk, lambda b, i: (0, 0, b * nq + i)),
            pl.BlockSpec(q_blk, lambda b, i: (0, 0, b * nq + jnp.minimum(i + 1, nq - 1))),
            pl.BlockSpec((ATT_KV, seq_len, ATT_DH), lambda b, i: (0, b, 0)),
            pl.BlockSpec((ATT_KV, n_kb, V_ROWS, TK), lambda b, i: (0, b, 0, 0)),
        ],
        out_specs=pl.BlockSpec((TQ, ATT_QW), lambda b, i: (b * nq + i, 0)),
        out_shape=jax.ShapeDtypeStruct((rows, ATT_QW), F32),
        scratch_shapes=[
            pltpu.VMEM((ATT_HEADS, 1, TQ), F32),
            pltpu.VMEM((ATT_HEADS, V_ROWS, TQ), F32),
            pltpu.VMEM((2, ATT_HEADS, TK, TQ), F32),
            pltpu.VMEM((2, ATT_HEADS, 1, TQ), F32),
        ],
        compiler_params=pltpu.CompilerParams(
            dimension_semantics=("arbitrary", "arbitrary"), vmem_limit_bytes=VMEM_LIMIT),
        name="attention",
    )(qt, qt, k, vt)


def _rms(y, w):
    ms = jnp.mean(y * y, axis=-1, keepdims=True)
    return y * lax.rsqrt(ms + EPS) * w


def _mix_ffn_kernel(of_ref, ob_ref, sg_ref, att_ref, x_ref, wo32_ref, hgw_ref, attw_ref, n2w_ref,
                    wgu32_ref, wd32_ref, fw_ref, o_ref, wo_ref, wgu_ref, wd_ref, *, d_ff, final_norm, phases):
    i = pl.program_id(0)
    start = 0
    for src_ref, dst_ref, n_pieces in ((wgu32_ref, wgu_ref, phases[0]), (wd32_ref, wd_ref, phases[1]),
                                       (wo32_ref, wo_ref, phases[2])):
        rows = src_ref.shape[0]

        @pl.when((i >= start) & (i < start + n_pieces))
        def _(src_ref=src_ref, dst_ref=dst_ref, rows=rows, start=start):
            r0 = pl.multiple_of((i - start) * rows, 16)
            dst_ref[pl.ds(r0, rows), :] = src_ref[...].astype(BF16)

        start += n_pieces

    @pl.when(i >= start)
    def _():
        o = of_ref[...] + ob_ref[...]
        hgw = hgw_ref[...]
        o_hg = jnp.concatenate(
            [_rms(o[:, hh * HG_D:(hh + 1) * HG_D], hgw) for hh in range(HG_HEADS)], axis=1)
        o_hg = (o_hg * sg_ref[...].astype(F32)).astype(BF16)
        o_att = _rms(att_ref[...], attw_ref[...]).astype(BF16)
        y = x_ref[...] + _dot(o_hg, wo_ref[0:HG_W, :]) + _dot(o_att, wo_ref[HG_W:HG_W + ATT_QW, :])
        h = _rms(y, n2w_ref[...]).astype(BF16)
        ffn = None
        for c in range(d_ff // FF_CHUNK):
            lo = c * FF_CHUNK
            gate = _dot(h, wgu_ref[:, lo:lo + FF_CHUNK])
            up = _dot(h, wgu_ref[:, d_ff + lo:d_ff + lo + FF_CHUNK])
            act = (_silu(gate) * up).astype(BF16)
            part = _dot(act, wd_ref[lo:lo + FF_CHUNK, :])
            ffn = part if ffn is None else ffn + part
        z = y + ffn
        o_ref[...] = _rms(z, fw_ref[...]) if final_norm else z


def _mix_ffn(o_f, o_b, sg, att, x2d, wo32, hgw, attw, n2w, wgu32, wd32, fw, final_norm):
    rows, d = x2d.shape
    d_ff = wd32.shape[0]
    tm = TM_FFN
    assert wgu32.shape[0] % WGU_PIECE == 0 and wd32.shape[0] % WD_PIECE == 0 and wo32.shape[0] % WO_PIECE == 0
    assert WGU_PIECE % 16 == 0 and WD_PIECE % 16 == 0 and WO_PIECE % 16 == 0 and d_ff % FF_CHUNK == 0
    phases = (wgu32.shape[0] // WGU_PIECE, wd32.shape[0] // WD_PIECE, wo32.shape[0] // WO_PIECE)
    n_load = sum(phases)
    tile = lambda i: (jnp.maximum(i - n_load, 0), 0)
    const = lambda i: (0, 0)
    half = pl.BlockSpec((tm, HG_W), tile)
    wide = pl.BlockSpec((tm, d), tile)
    small = lambda a: pl.BlockSpec(a.shape, const)

    def pieces(a, piece_rows, first, count):
        return pl.BlockSpec((piece_rows, a.shape[1]), lambda i: (jnp.clip(i - first, 0, count - 1), 0))

    return pl.pallas_call(
        functools.partial(_mix_ffn_kernel, d_ff=d_ff, final_norm=final_norm, phases=phases),
        grid=(n_load + rows // tm,),
        in_specs=[half, half, half, half, wide,
                  pieces(wo32, WO_PIECE, phases[0] + phases[1], phases[2]),
                  small(hgw), small(attw), small(n2w),
                  pieces(wgu32, WGU_PIECE, 0, phases[0]), pieces(wd32, WD_PIECE, phases[0], phases[1]),
                  small(fw)],
        out_specs=wide,
        out_shape=jax.ShapeDtypeStruct((rows, d), F32),
        scratch_shapes=[pltpu.VMEM(wo32.shape, BF16), pltpu.VMEM(wgu32.shape, BF16),
                        pltpu.VMEM(wd32.shape, BF16)],
        compiler_params=pltpu.CompilerParams(
            dimension_semantics=("arbitrary",), vmem_limit_bytes=VMEM_LIMIT),
        name="mix_ffn",
    )(o_f, o_b, sg, att, x2d, wo32, hgw, attw, n2w, wgu32, wd32, fw)


def _rope_tables(seq_len):
    f32 = np.float32
    rows = seq_len // GRID_W
    row = np.repeat(np.arange(rows), GRID_W).astype(f32)
    col = np.tile(np.arange(GRID_W), rows).astype(f32)
    axis_dim = ATT_DH // 2
    freqs = (f32(ROPE_THETA) ** (-np.arange(0, axis_dim, 2, dtype=f32) / f32(axis_dim))).astype(f32)
    ang = np.concatenate([row[:, None] * freqs, col[:, None] * freqs], axis=-1)
    cos = np.repeat(np.cos(ang), 2, axis=1)
    sin = np.repeat(np.sin(ang), 2, axis=1) * np.tile(np.array([-1.0, 1.0], f32), ATT_DH // 2)
    reps = LANES // ATT_DH
    return jnp.asarray(np.tile(cos, (1, reps)), F32), jnp.asarray(np.tile(sin, (1, reps)), F32)


def kernel(x, norm1_w, w_in, lb_logits, hg_norm_w, q_norm_w, k_norm_w, att_norm_w, w_out, norm2_w,
           w_gate_up, w_down, final_norm_w):
    batch, seq_len, d_model = x.shape
    depth = norm1_w.shape[0]
    rows = batch * seq_len
    assert seq_len % (2 * TK) == 0 and seq_len % TQ == 0
    assert seq_len % T_SCAN == 0 and T_SCAN % T_SEG == 0 and T_SEG % CHUNK == 0 and TK % TM_IN == 0
    assert TM_IN % TQ == 0 and rows % TM_FFN == 0 and w_in.shape[2] == D_IN

    lb_all = jnp.cumsum(jax.nn.softmax(lb_logits.astype(F32), axis=1), axis=1)
    cos_t, sin_t = _rope_tables(seq_len)
    blk = np.arange(ATT_QW) // ATT_DH
    mq = jnp.asarray(np.where(blk[:, None] == blk[None, :], 1.0 / ATT_DH, 0.0), BF16)
    t_idx = np.arange(T_SEG)
    same_chunk = (t_idx[:, None] // CHUNK) == (t_idx[None, :] // CHUNK)
    tri = jnp.asarray(np.stack([same_chunk & (t_idx[:, None] >= t_idx[None, :]),
                                same_chunk & (t_idx[None, :] >= t_idx[:, None])]).astype(np.float32),
                      BF16)
    q_scale = ATT_DH ** -0.5 * math.log2(math.e)

    x2d = x.reshape(rows, d_model)
    row = lambda v: v.astype(F32).reshape(1, -1)
    for l in range(depth):
        hq, gf, kf, gb, kb, iv, sg, qt, k, vt = _inproj(
            x2d, row(norm1_w[l]), w_in[l].astype(F32), row(lb_all[0, l]), row(lb_all[1, l]),
            row(jnp.tile(q_norm_w[l], ATT_HEADS)) * q_scale, row(jnp.tile(k_norm_w[l], ATT_KV)),
            cos_t, sin_t, mq, seq_len)
        o_f, o_b = _hgrn(hq, gf, kf, gb, kb, iv, tri, batch, seq_len)
        att = _attention(qt, k, vt, batch, seq_len)
        x2d = _mix_ffn(o_f, o_b, sg, att, x2d, w_out[l].astype(F32),
                       row(hg_norm_w[l]), row(att_norm_w[l]), row(norm2_w[l]),
                       w_gate_up[l].astype(F32), w_down[l].astype(F32), row(final_norm_w), l == depth - 1)
    return x2d.reshape(batch, seq_len, d_model)
```

```python
import functools
import math

import jax
import jax.numpy as jnp
import numpy as np
from jax import lax
from jax.experimental import pallas as pl
from jax.experimental.pallas import tpu as pltpu

F32 = jnp.float32
BF16 = jnp.bfloat16

EPS = 1e-6
GRID_W = 64
ROPE_THETA = 10000.0

HG_HEADS = 4
HG_D = 128
HG_W = HG_HEADS * HG_D
CHUNK = 64

ATT_HEADS = 8
ATT_KV = 2
ATT_GROUP = ATT_HEADS // ATT_KV
ATT_DH = 64
ATT_QW = ATT_HEADS * ATT_DH
ATT_KVW = ATT_KV * ATT_DH
V_ROWS = ATT_DH + 16

C_Q, C_FF, C_FB, C_I, C_G = 0, HG_W, 2 * HG_W, 3 * HG_W, 4 * HG_W
C_AQ = 5 * HG_W
C_AK = C_AQ + ATT_QW
C_AV = C_AK + ATT_KVW
D_IN = C_AV + ATT_KVW

LANES = 128
VMEM_LIMIT = 56 * 1024 * 1024

TM_IN = 512
TQ = 256
TK = 512
T_SEG = 256
T_SCAN = 1024
TM_FFN = 512
FF_CHUNK = 256
PIECE = 256
WGU_PIECE, WD_PIECE, WO_PIECE = 128, 704, 256


def _silu(u):
    hu = 0.5 * u
    return hu + hu * jnp.tanh(hu)


def _dot(a, b):
    return jnp.dot(a, b, preferred_element_type=F32)


def _dot_nt(a, b):
    return lax.dot_general(a, b, (((1,), (1,)), ((), ())), preferred_element_type=F32)


def _dot_tn(a, b):
    return lax.dot_general(a, b, (((0,), (0,)), ((), ())), preferred_element_type=F32)


def _inproj_kernel(x_ref, n1w_ref, w32_ref, lbf_ref, lbb_ref, wq_ref, wk_ref, cos_ref, sin_ref, mq_ref,
                   hq_ref, gf_ref, kf_ref, gb_ref, kb_ref, iv_ref, sg_ref, qt_ref, k_ref, vt_ref, w_ref):
    @pl.when(pl.program_id(0) == 0)
    def _():
        w_ref[...] = w32_ref[...].astype(BF16)

    tm = x_ref.shape[0]
    x = x_ref[...]
    ms = jnp.mean(x * x, axis=-1, keepdims=True)
    h = (x * lax.rsqrt(ms + EPS) * n1w_ref[...]).astype(BF16)

    def proj(lo, hi):
        return _dot(h, w_ref[:, lo:hi])

    def gates(z, lb, g_ref, k_out_ref, cs):
        lb = lb[:, cs]
        hs = 0.5 * (1.0 - lb)
        ht = hs * jnp.tanh(0.5 * z)
        g_ref[:, cs] = jnp.log2((lb + hs) + ht)
        k_out_ref[:, cs] = (hs - ht).astype(BF16)

    cos = cos_ref[...]
    sin = sin_ref[...]
    lane = lax.broadcasted_iota(jnp.int32, (tm, LANES), 1)
    even = (lane & 1) == 0

    def rope(xc):
        partner = jnp.where(even, pltpu.roll(xc, LANES - 1, 1), pltpu.roll(xc, 1, 1))
        return xc * cos + partner * sin

    def head_rms(a, m, w):
        return a * lax.rsqrt(_dot((a * a).astype(BF16), m) + EPS) * w

    def ep_aq(aq, piece, cs):
        yq = head_rms(aq, mq_ref[0:PIECE, 0:PIECE], wq_ref[:, cs])
        for pair in range(PIECE // LANES):
            qct = rope(yq[:, pair * LANES:(pair + 1) * LANES]).T.astype(BF16)
            for e in range(2):
                hl = 2 * pair + e
                for blk in range(tm // TQ):
                    col = (blk * ATT_GROUP + hl) * TQ
                    qt_ref[piece, :, col:col + TQ] = qct[e * ATT_DH:(e + 1) * ATT_DH, blk * TQ:(blk + 1) * TQ]

    def ep_ak(ak, piece, cs):
        yk = rope(head_rms(ak, mq_ref[0:LANES, 0:LANES], wk_ref[...])).astype(BF16)
        k_ref[0] = yk[:, 0:ATT_DH]
        k_ref[1] = yk[:, ATT_DH:2 * ATT_DH]

    def ep_av(av, piece, cs):
        avt = av.T.astype(BF16)
        ones = jnp.ones((V_ROWS - ATT_DH, tm), BF16)
        for g in range(ATT_KV):
            vt_ref[g, 0, 0:ATT_DH, :] = avt[g * ATT_DH:(g + 1) * ATT_DH, :]
            vt_ref[g, 0, ATT_DH:V_ROWS, :] = ones

    def ep_ff(u, piece, cs):
        gates(u, lbf_ref[...], gf_ref, kf_ref, cs)

    def ep_fb(u, piece, cs):
        gates(u, lbb_ref[...], gb_ref, kb_ref, cs)

    def ep_q(u, piece, cs):
        hq_ref[:, cs] = _silu(u).astype(BF16)

    def ep_g(u, piece, cs):
        sg_ref[:, cs] = _silu(u).astype(BF16)

    def ep_i(u, piece, cs):
        iv_ref[:, cs] = u.astype(BF16)

    halves = lambda base, ep: [(base + p * PIECE, PIECE, ep, p) for p in range(HG_W // PIECE)]
    stages = (halves(C_AQ, ep_aq) + [(C_AK, ATT_KVW, ep_ak, 0), (C_AV, ATT_KVW, ep_av, 0)]
              + halves(C_FF, ep_ff) + halves(C_FB, ep_fb) + halves(C_Q, ep_q) + halves(C_G, ep_g)
              + halves(C_I, ep_i))
    pending = None
    for lo, width, epilogue, piece in stages:
        u = proj(lo, lo + width)
        if pending is not None:
            pending()
        pending = functools.partial(epilogue, u, piece, slice(piece * PIECE, (piece + 1) * PIECE))
    pending()


def _inproj(x2d, n1w, w32, lbf, lbb, wq, wk, cos_t, sin_t, mq, seq_len):
    rows = x2d.shape[0]
    tm = TM_IN
    n_tiles = rows // tm
    tiles_per_seq = seq_len // tm
    per_tk = TK // tm
    row_blk = lambda i: (i, 0)
    const = lambda i: (0, 0)
    full = lambda shape: pl.BlockSpec(shape, const)
    act_bf = jax.ShapeDtypeStruct((rows, HG_W), BF16)
    act_f32 = jax.ShapeDtypeStruct((rows, HG_W), F32)
    act_spec = pl.BlockSpec((tm, HG_W), row_blk)
    return pl.pallas_call(
        _inproj_kernel,
        grid=(n_tiles,),
        in_specs=[
            pl.BlockSpec((tm, x2d.shape[1]), row_blk),
            full(n1w.shape), pl.BlockSpec(w32.shape, const, pipeline_mode=pl.Buffered(1)),
            full(lbf.shape), full(lbb.shape),
            full(wq.shape), full(wk.shape),
            pl.BlockSpec((tm, LANES), lambda i: (i % tiles_per_seq, 0)),
            pl.BlockSpec((tm, LANES), lambda i: (i % tiles_per_seq, 0)),
            full(mq.shape),
        ],
        out_specs=[
            act_spec, act_spec, act_spec, act_spec, act_spec, act_spec, act_spec,
            pl.BlockSpec((ATT_KV, ATT_DH, ATT_GROUP * tm), lambda i: (0, 0, i)),
            pl.BlockSpec((ATT_KV, tm, ATT_DH), lambda i: (0, i, 0)),
            pl.BlockSpec((ATT_KV, 1, V_ROWS, tm), lambda i: (0, i // per_tk, 0, i % per_tk)),
        ],
        out_shape=[
            act_bf, act_f32, act_bf, act_f32, act_bf, act_bf, act_bf,
            jax.ShapeDtypeStruct((ATT_KV, ATT_DH, ATT_GROUP * rows), BF16),
            jax.ShapeDtypeStruct((ATT_KV, rows, ATT_DH), BF16),
            jax.ShapeDtypeStruct((ATT_KV, rows // TK, V_ROWS, TK), BF16),
        ],
        scratch_shapes=[pltpu.VMEM(w32.shape, BF16)],
        compiler_params=pltpu.CompilerParams(
            dimension_semantics=("arbitrary",), vmem_limit_bytes=VMEM_LIMIT),
        name="inproj",
    )(x2d, n1w, w32, lbf, lbb, wq, wk, cos_t, sin_t, mq)


def _hgrn_kernel(qf_ref, gf_ref, kf_ref, vf_ref, qb_ref, gb_ref, kb_ref, vb_ref, tri_ref,
                 of_ref, ob_ref, st_ref):
    @pl.when(pl.program_id(1) == 0)
    def _():
        st_ref[...] = jnp.zeros_like(st_ref)

    t_seg = tri_ref.shape[1]
    n_seg = qf_ref.shape[0] // t_seg
    n_chunks = t_seg // CHUNK
    row = lax.broadcasted_iota(jnp.int32, (CHUNK, CHUNK), 0)
    col = lax.broadcasted_iota(jnp.int32, (CHUNK, CHUNK), 1)
    plans = (
        (qf_ref, gf_ref, kf_ref, vf_ref, of_ref, row >= col, CHUNK // 2 - 1, CHUNK - 1, range(n_chunks)),
        (qb_ref, gb_ref, kb_ref, vb_ref, ob_ref, col >= row, CHUNK // 2, 0, range(n_chunks - 1, -1, -1)),
    )

    def sub(a, c, hh):
        return a[c * CHUNK:(c + 1) * CHUNK, hh * HG_D:(hh + 1) * HG_D]

    def stage(sv):
        out = []
        for d, (q_ref, g_ref, k_ref, v_ref, o_ref, keep, ref_row, end_row, order) in enumerate(plans):
            off = (sv if d == 0 else n_seg - 1 - sv) * t_seg
            rs = slice(off, off + t_seg)
            b = _split_dot_lhs(tri_ref[d], g_ref[rs, :])
            b_r = jnp.concatenate(
                [jnp.broadcast_to(b[c * CHUNK + ref_row:c * CHUNK + ref_row + 1, :], (CHUNK, HG_W))
                 for c in range(n_chunks)], axis=0)
            rel = b - b_r
            out.append((off, b, (q_ref[rs, :].astype(F32) * jnp.exp2(rel)).astype(BF16),
                        (k_ref[rs, :].astype(F32) * jnp.exp2(-rel)).astype(BF16)))
        return out

    def local_stage(seg, ci):
        work, scores, contrib = [], [], []
        for d, (q_ref, g_ref, k_ref, v_ref, o_ref, keep, ref_row, end_row, order) in enumerate(plans):
            off, _, q_seg, k_seg = seg[d]
            c = order[ci]
            for hh in range(HG_HEADS):
                q_in, k_in = sub(q_seg, c, hh), sub(k_seg, c, hh)
                v = v_ref[off + c * CHUNK:off + (c + 1) * CHUNK, hh * HG_D:(hh + 1) * HG_D]
                work.append((d, c, hh, q_in, v))
                scores.append(jnp.where(keep, _dot_nt(q_in, k_in), 0.0).astype(BF16))
                contrib.append(_dot_tn(v, k_in))
        return work, scores, contrib

    def state_stage(seg, work, scores, contrib):
        for (d, c, hh, q_in, v), sc, ct in zip(work, scores, contrib):
            o_ref, ref_row, end_row = plans[d][4], plans[d][6], plans[d][7]
            off, b = seg[d][0], seg[d][1]
            cs = slice(hh * HG_D, (hh + 1) * HG_D)
            b_r = b[c * CHUNK + ref_row:c * CHUNK + ref_row + 1, cs]
            b_end = b[c * CHUNK + end_row:c * CHUNK + end_row + 1, cs]
            st = st_ref[d * HG_HEADS + hh]
            lhs = jnp.concatenate([q_in, sc], axis=1)
            rhs = jnp.concatenate([(st * jnp.exp2(b_r)).astype(BF16), v.T], axis=1)
            o_ref[off + c * CHUNK:off + (c + 1) * CHUNK, cs] = _dot_nt(lhs, rhs)
            st_ref[d * HG_HEADS + hh] = st * jnp.exp2(b_end) + ct * jnp.exp2(b_end - b_r)

    seg = stage(0)
    for sv in range(n_seg):
        ahead = local_stage(seg, 0)
        nxt = stage(sv + 1) if sv + 1 < n_seg else None
        for ci in range(n_chunks):
            current, ahead = ahead, (local_stage(seg, ci + 1) if ci + 1 < n_chunks else None)
            state_stage(seg, *current)
        seg = nxt


def _split_dot_lhs(m, a):
    hi = a.astype(BF16)
    lo = (a - hi.astype(F32)).astype(BF16)
    return _dot(m, hi) + _dot(m, lo)


def _hgrn(hq, gf, kf, gb, kb, iv, tri, batch, seq_len):
    rows = hq.shape[0]
    t = T_SCAN
    nb = seq_len // t
    fwd = lambda b, n: (b * nb + n, 0)
    bwd = lambda b, n: (b * nb + nb - 1 - n, 0)
    sf = pl.BlockSpec((t, HG_W), fwd)
    sb = pl.BlockSpec((t, HG_W), bwd)
    out = jax.ShapeDtypeStruct((rows, HG_W), F32)
    return pl.pallas_call(
        _hgrn_kernel,
        grid=(batch, nb),
        in_specs=[sf, sf, sf, sf, sb, sb, sb, sb, pl.BlockSpec(tri.shape, lambda b, n: (0, 0, 0))],
        out_specs=[sf, sb],
        out_shape=[out, out],
        scratch_shapes=[pltpu.VMEM((2 * HG_HEADS, HG_D, HG_D), F32)],
        compiler_params=pltpu.CompilerParams(
            dimension_semantics=("arbitrary", "arbitrary"), vmem_limit_bytes=VMEM_LIMIT),
        name="hgrn_scan",
    )(hq, gf, kf, iv, hq, gb, kb, iv, tri)


def _attn_kernel(q_ref, qn_ref, k_ref, v_ref, o_ref, m_ref, acc_ref, s_ref, bm_ref):
    tq = o_ref.shape[0]
    tk = v_ref.shape[3]
    n_kb = v_ref.shape[1]
    heads = [(g, hh) for g in range(ATT_KV) for hh in range(ATT_GROUP)]

    def scores(qsrc_ref, j, g, hh):
        u = g * ATT_GROUP + hh
        s = _dot(k_ref[g, j * tk:(j + 1) * tk, :],
                 qsrc_ref[g, :, hh * tq:(hh + 1) * tq])
        s_ref[j % 2, u] = s
        bm_ref[j % 2, u] = jnp.max(s, axis=0, keepdims=True)

    def accumulate(j, g, hh):
        u = g * ATT_GROUP + hh
        m_old = m_ref[u]
        m_new = jnp.maximum(m_old, bm_ref[j % 2, u])
        p = jnp.exp2(s_ref[j % 2, u] - m_new).astype(BF16)
        acc_ref[u] = jnp.exp2(m_old - m_new) * acc_ref[u] + _dot(v_ref[g, j], p)
        m_ref[u] = m_new

    @pl.when(pl.program_id(1) == 0)
    def _():
        for g, hh in heads:
            scores(q_ref, 0, g, hh)

    m_ref[...] = jnp.full(m_ref.shape, -1e30, F32)
    acc_ref[...] = jnp.zeros_like(acc_ref)
    def finalize(pair):
        ot = jnp.concatenate(
            [acc_ref[2 * pair + e, 0:ATT_DH, :] / acc_ref[2 * pair + e, ATT_DH:ATT_DH + 1, :]
             for e in range(2)], axis=0)
        o_ref[:, pair * LANES:(pair + 1) * LANES] = ot.T

    for j in range(n_kb):
        for u, (g, hh) in enumerate(heads):
            if j + 1 < n_kb:
                scores(q_ref, j + 1, g, hh)
            else:
                scores(qn_ref, 0, g, hh)
            accumulate(j, g, hh)
            if j + 1 == n_kb and u % 2 == 1:
                finalize(u // 2)


def _attention(qt, k, vt, batch, seq_len):
    rows = k.shape[1]
    nq = seq_len // TQ
    n_kb = seq_len // TK
    q_blk = (ATT_KV, ATT_DH, ATT_GROUP * TQ)
    return pl.pallas_call(
        _attn_kernel,
        grid=(batch, nq),
        in_specs=[
            pl.BlockSpec(q_blk, lambda b, i: (0, 0, b * nq + i)),
            pl.BlockSpec(q_blk, lambda b, i: (0, 0, b * nq + jnp.minimum(i + 1, nq - 1))),
            pl.BlockSpec((ATT_KV, seq_len, ATT_DH), lambda b, i: (0, b, 0)),
            pl.BlockSpec((ATT_KV, n_kb, V_ROWS, TK), lambda b, i: (0, b, 0, 0)),
        ],
        out_specs=pl.BlockSpec((TQ, ATT_QW), lambda b, i: (b * nq + i, 0)),
        out_shape=jax.ShapeDtypeStruct((rows, ATT_QW), F32),
        scratch_shapes=[
            pltpu.VMEM((ATT_HEADS, 1, TQ), F32),
            pltpu.VMEM((ATT_HEADS, V_ROWS, TQ), F32),
            pltpu.VMEM((2, ATT_HEADS, TK, TQ), F32),
            pltpu.VMEM((2, ATT_HEADS, 1, TQ), F32),
        ],
        compiler_params=pltpu.CompilerParams(
            dimension_semantics=("arbitrary", "arbitrary"), vmem_limit_bytes=VMEM_LIMIT),
        name="attention",
    )(qt, qt, k, vt)


def _rms(y, w):
    ms = jnp.mean(y * y, axis=-1, keepdims=True)
    return y * lax.rsqrt(ms + EPS) * w


def _mix_ffn_kernel(of_ref, ob_ref, sg_ref, att_ref, x_ref, wo32_ref, hgw_ref, attw_ref, n2w_ref,
                    wgu32_ref, wd32_ref, fw_ref, o_ref, wo_ref, wgu_ref, wd_ref, *, d_ff, final_norm, phases):
    i = pl.program_id(0)
    start = 0
    for src_ref, dst_ref, n_pieces in ((wgu32_ref, wgu_ref, phases[0]), (wd32_ref, wd_ref, phases[1]),
                                       (wo32_ref, wo_ref, phases[2])):
        rows = src_ref.shape[0]

        @pl.when((i >= start) & (i < start + n_pieces))
        def _(src_ref=src_ref, dst_ref=dst_ref, rows=rows, start=start):
            r0 = pl.multiple_of((i - start) * rows, 16)
            dst_ref[pl.ds(r0, rows), :] = src_ref[...].astype(BF16)

        start += n_pieces

    @pl.when(i >= start)
    def _():
        o = of_ref[...] + ob_ref[...]
        hgw = hgw_ref[...]
        o_hg = jnp.concatenate(
            [_rms(o[:, hh * HG_D:(hh + 1) * HG_D], hgw) for hh in range(HG_HEADS)], axis=1)
        o_hg = (o_hg * sg_ref[...].astype(F32)).astype(BF16)
        o_att = _rms(att_ref[...], attw_ref[...]).astype(BF16)
        y = x_ref[...] + _dot(o_hg, wo_ref[0:HG_W, :]) + _dot(o_att, wo_ref[HG_W:HG_W + ATT_QW, :])
        h = _rms(y, n2w_ref[...]).astype(BF16)
        ffn = None
        for c in range(d_ff // FF_CHUNK):
            lo = c * FF_CHUNK
            gate = _dot(h, wgu_ref[:, lo:lo + FF_CHUNK])
            up = _dot(h, wgu_ref[:, d_ff + lo:d_ff + lo + FF_CHUNK])
            act = (_silu(gate) * up).astype(BF16)
            part = _dot(act, wd_ref[lo:lo + FF_CHUNK, :])
            ffn = part if ffn is None else ffn + part
        z = y + ffn
        o_ref[...] = _rms(z, fw_ref[...]) if final_norm else z


def _mix_ffn(o_f, o_b, sg, att, x2d, wo32, hgw, attw, n2w, wgu32, wd32, fw, final_norm):
    rows, d = x2d.shape
    d_ff = wd32.shape[0]
    tm = TM_FFN
    assert wgu32.shape[0] % WGU_PIECE == 0 and wd32.shape[0] % WD_PIECE == 0 and wo32.shape[0] % WO_PIECE == 0
    assert WGU_PIECE % 16 == 0 and WD_PIECE % 16 == 0 and WO_PIECE % 16 == 0 and d_ff % FF_CHUNK == 0
    phases = (wgu32.shape[0] // WGU_PIECE, wd32.shape[0] // WD_PIECE, wo32.shape[0] // WO_PIECE)
    n_load = sum(phases)
    tile = lambda i: (jnp.maximum(i - n_load, 0), 0)
    const = lambda i: (0, 0)
    half = pl.BlockSpec((tm, HG_W), tile)
    wide = pl.BlockSpec((tm, d), tile)
    small = lambda a: pl.BlockSpec(a.shape, const)

    def pieces(a, piece_rows, first, count):
        return pl.BlockSpec((piece_rows, a.shape[1]), lambda i: (jnp.clip(i - first, 0, count - 1), 0))

    return pl.pallas_call(
        functools.partial(_mix_ffn_kernel, d_ff=d_ff, final_norm=final_norm, phases=phases),
        grid=(n_load + rows // tm,),
        in_specs=[half, half, half, half, wide,
                  pieces(wo32, WO_PIECE, phases[0] + phases[1], phases[2]),
                  small(hgw), small(attw), small(n2w),
                  pieces(wgu32, WGU_PIECE, 0, phases[0]), pieces(wd32, WD_PIECE, phases[0], phases[1]),
                  small(fw)],
        out_specs=wide,
        out_shape=jax.ShapeDtypeStruct((rows, d), F32),
        scratch_shapes=[pltpu.VMEM(wo32.shape, BF16), pltpu.VMEM(wgu32.shape, BF16),
                        pltpu.VMEM(wd32.shape, BF16)],
        compiler_params=pltpu.CompilerParams(
            dimension_semantics=("arbitrary",), vmem_limit_bytes=VMEM_LIMIT),
        name="mix_ffn",
    )(o_f, o_b, sg, att, x2d, wo32, hgw, attw, n2w, wgu32, wd32, fw)


def _rope_tables(seq_len):
    f32 = np.float32
    rows = seq_len // GRID_W
    row = np.repeat(np.arange(rows), GRID_W).astype(f32)
    col = np.tile(np.arange(GRID_W), rows).astype(f32)
    axis_dim = ATT_DH // 2
    freqs = (f32(ROPE_THETA) ** (-np.arange(0, axis_dim, 2, dtype=f32) / f32(axis_dim))).astype(f32)
    ang = np.concatenate([row[:, None] * freqs, col[:, None] * freqs], axis=-1)
    cos = np.repeat(np.cos(ang), 2, axis=1)
    sin = np.repeat(np.sin(ang), 2, axis=1) * np.tile(np.array([-1.0, 1.0], f32), ATT_DH // 2)
    reps = LANES // ATT_DH
    return jnp.asarray(np.tile(cos, (1, reps)), F32), jnp.asarray(np.tile(sin, (1, reps)), F32)


def kernel(x, norm1_w, w_in, lb_logits, hg_norm_w, q_norm_w, k_norm_w, att_norm_w, w_out, norm2_w,
           w_gate_up, w_down, final_norm_w):
    batch, seq_len, d_model = x.shape
    depth = norm1_w.shape[0]
    rows = batch * seq_len
    assert seq_len % (2 * TK) == 0 and seq_len % TQ == 0
    assert seq_len % T_SCAN == 0 and T_SCAN % T_SEG == 0 and T_SEG % CHUNK == 0 and TK % TM_IN == 0
    assert TM_IN % TQ == 0 and rows % TM_FFN == 0 and w_in.shape[2] == D_IN

    lb_all = jnp.cumsum(jax.nn.softmax(lb_logits.astype(F32), axis=1), axis=1)
    cos_t, sin_t = _rope_tables(seq_len)
    blk = np.arange(ATT_QW) // ATT_DH
    mq = jnp.asarray(np.where(blk[:, None] == blk[None, :], 1.0 / ATT_DH, 0.0), BF16)
    t_idx = np.arange(T_SEG)
    same_chunk = (t_idx[:, None] // CHUNK) == (t_idx[None, :] // CHUNK)
    tri = jnp.asarray(np.stack([same_chunk & (t_idx[:, None] >= t_idx[None, :]),
                                same_chunk & (t_idx[None, :] >= t_idx[:, None])]).astype(np.float32),
                      BF16)
    q_scale = ATT_DH ** -0.5 * math.log2(math.e)

    x2d = x.reshape(rows, d_model)
    row = lambda v: v.astype(F32).reshape(1, -1)
    for l in range(depth):
        hq, gf, kf, gb, kb, iv, sg, qt, k, vt = _inproj(
            x2d, row(norm1_w[l]), w_in[l].astype(F32), row(lb_all[0, l]), row(lb_all[1, l]),
            row(jnp.tile(q_norm_w[l], ATT_HEADS)) * q_scale, row(jnp.tile(k_norm_w[l], ATT_KV)),
            cos_t, sin_t, mq, seq_len)
        o_f, o_b = _hgrn(hq, gf, kf, gb, kb, iv, tri, batch, seq_len)
        att = _attention(qt, k, vt, batch, seq_len)
        x2d = _mix_ffn(o_f, o_b, sg, att, x2d, w_out[l].astype(F32),
                       row(hg_norm_w[l]), row(att_norm_w[l]), row(norm2_w[l]),
                       w_gate_up[l].astype(F32), w_down[l].astype(F32), row(final_norm_w), l == depth - 1)
    return x2d.reshape(batch, seq_len, d_model)
```

```python
import functools
import math

import jax
import jax.numpy as jnp
import numpy as np
from jax import lax
from jax.experimental import pallas as pl
from jax.experimental.pallas import tpu as pltpu

F32 = jnp.float32
BF16 = jnp.bfloat16

EPS = 1e-6
GRID_W = 64
ROPE_THETA = 10000.0

HG_HEADS = 4
HG_D = 128
HG_W = HG_HEADS * HG_D
CHUNK = 64

ATT_HEADS = 8
ATT_KV = 2
ATT_GROUP = ATT_HEADS // ATT_KV
ATT_DH = 64
ATT_QW = ATT_HEADS * ATT_DH
ATT_KVW = ATT_KV * ATT_DH
V_ROWS = ATT_DH + 16

C_Q, C_FF, C_FB, C_I, C_G = 0, HG_W, 2 * HG_W, 3 * HG_W, 4 * HG_W
C_AQ = 5 * HG_W
C_AK = C_AQ + ATT_QW
C_AV = C_AK + ATT_KVW
D_IN = C_AV + ATT_KVW

LANES = 128
VMEM_LIMIT = 56 * 1024 * 1024

TM_IN = 512
TQ = 256
TK = 512
T_SEG = 256
T_SCAN = 1024
TM_FFN = 512
FF_CHUNK = 256
PIECE = 256
EPILOGUES_BEHIND = 2
WGU_PIECE, WD_PIECE, WO_PIECE = 128, 704, 256


def _silu(u):
    hu = 0.5 * u
    return hu + hu * jnp.tanh(hu)


def _dot(a, b):
    return jnp.dot(a, b, preferred_element_type=F32)


def _dot_nt(a, b):
    return lax.dot_general(a, b, (((1,), (1,)), ((), ())), preferred_element_type=F32)


def _dot_tn(a, b):
    return lax.dot_general(a, b, (((0,), (0,)), ((), ())), preferred_element_type=F32)


def _inproj_kernel(x_ref, n1w_ref, w32_ref, lbf_ref, lbb_ref, wq_ref, wk_ref, cos_ref, sin_ref, mq_ref,
                   hq_ref, gf_ref, kf_ref, gb_ref, kb_ref, iv_ref, sg_ref, qt_ref, k_ref, vt_ref, w_ref):
    @pl.when(pl.program_id(0) == 0)
    def _():
        w_ref[...] = w32_ref[...].astype(BF16)

    tm = x_ref.shape[0]
    x = x_ref[...]
    ms = jnp.mean(x * x, axis=-1, keepdims=True)
    h = (x * lax.rsqrt(ms + EPS) * n1w_ref[...]).astype(BF16)

    def proj(lo, hi):
        return _dot(h, w_ref[:, lo:hi])

    def gates(z, lb, g_ref, k_out_ref, cs):
        lb = lb[:, cs]
        hs = 0.5 * (1.0 - lb)
        ht = hs * jnp.tanh(0.5 * z)
        g_ref[:, cs] = jnp.log2((lb + hs) + ht)
        k_out_ref[:, cs] = (hs - ht).astype(BF16)

    cos = cos_ref[...]
    sin = sin_ref[...]
    lane = lax.broadcasted_iota(jnp.int32, (tm, LANES), 1)
    even = (lane & 1) == 0

    def rope(xc):
        partner = jnp.where(even, pltpu.roll(xc, LANES - 1, 1), pltpu.roll(xc, 1, 1))
        return xc * cos + partner * sin

    def head_rms(a, m, w):
        return a * lax.rsqrt(_dot((a * a).astype(BF16), m) + EPS) * w

    def ep_aq(aq, piece, cs):
        yq = head_rms(aq, mq_ref[0:PIECE, 0:PIECE], wq_ref[:, cs])
        for pair in range(PIECE // LANES):
            qct = rope(yq[:, pair * LANES:(pair + 1) * LANES]).T.astype(BF16)
            for e in range(2):
                hl = 2 * pair + e
                for blk in range(tm // TQ):
                    col = (blk * ATT_GROUP + hl) * TQ
                    qt_ref[piece, :, col:col + TQ] = qct[e * ATT_DH:(e + 1) * ATT_DH, blk * TQ:(blk + 1) * TQ]

    def ep_ak(ak, piece, cs):
        yk = rope(head_rms(ak, mq_ref[0:LANES, 0:LANES], wk_ref[...])).astype(BF16)
        k_ref[0] = yk[:, 0:ATT_DH]
        k_ref[1] = yk[:, ATT_DH:2 * ATT_DH]

    def ep_av(av, piece, cs):
        avt = av.T.astype(BF16)
        ones = jnp.ones((V_ROWS - ATT_DH, tm), BF16)
        for g in range(ATT_KV):
            vt_ref[g, 0, 0:ATT_DH, :] = avt[g * ATT_DH:(g + 1) * ATT_DH, :]
            vt_ref[g, 0, ATT_DH:V_ROWS, :] = ones

    def ep_ff(u, piece, cs):
        gates(u, lbf_ref[...], gf_ref, kf_ref, cs)

    def ep_fb(u, piece, cs):
        gates(u, lbb_ref[...], gb_ref, kb_ref, cs)

    def ep_q(u, piece, cs):
        hq_ref[:, cs] = _silu(u).astype(BF16)

    def ep_g(u, piece, cs):
        sg_ref[:, cs] = _silu(u).astype(BF16)

    def ep_i(u, piece, cs):
        iv_ref[:, cs] = u.astype(BF16)

    halves = lambda base, ep: [(base + p * PIECE, PIECE, ep, p) for p in range(HG_W // PIECE)]
    ff, fb, hq, sg, iv = (halves(base, ep) for base, ep in (
        (C_FF, ep_ff), (C_FB, ep_fb), (C_Q, ep_q), (C_G, ep_g), (C_I, ep_i)))
    stages = (halves(C_AQ, ep_aq) + [(C_AK, ATT_KVW, ep_ak, 0), (C_AV, ATT_KVW, ep_av, 0)]
              + [ff[0], hq[0], ff[1], hq[1], fb[0], sg[0], fb[1], sg[1]] + iv)
    queue = []
    for lo, width, epilogue, piece in stages:
        u = proj(lo, lo + width)
        queue.append(functools.partial(epilogue, u, piece, slice(piece * PIECE, (piece + 1) * PIECE)))
        if len(queue) > EPILOGUES_BEHIND:
            queue.pop(0)()
    for epilogue in queue:
        epilogue()


def _inproj(x2d, n1w, w32, lbf, lbb, wq, wk, cos_t, sin_t, mq, seq_len):
    rows = x2d.shape[0]
    tm = TM_IN
    n_tiles = rows // tm
    tiles_per_seq = seq_len // tm
    per_tk = TK // tm
    row_blk = lambda i: (i, 0)
    const = lambda i: (0, 0)
    full = lambda shape: pl.BlockSpec(shape, const)
    act_bf = jax.ShapeDtypeStruct((rows, HG_W), BF16)
    act_f32 = jax.ShapeDtypeStruct((rows, HG_W), F32)
    act_spec = pl.BlockSpec((tm, HG_W), row_blk)
    return pl.pallas_call(
        _inproj_kernel,
        grid=(n_tiles,),
        in_specs=[
            pl.BlockSpec((tm, x2d.shape[1]), row_blk),
            full(n1w.shape), pl.BlockSpec(w32.shape, const, pipeline_mode=pl.Buffered(1)),
            full(lbf.shape), full(lbb.shape),
            full(wq.shape), full(wk.shape),
            pl.BlockSpec((tm, LANES), lambda i: (i % tiles_per_seq, 0)),
            pl.BlockSpec((tm, LANES), lambda i: (i % tiles_per_seq, 0)),
            full(mq.shape),
        ],
        out_specs=[
            act_spec, act_spec, act_spec, act_spec, act_spec, act_spec, act_spec,
            pl.BlockSpec((ATT_KV, ATT_DH, ATT_GROUP * tm), lambda i: (0, 0, i)),
            pl.BlockSpec((ATT_KV, tm, ATT_DH), lambda i: (0, i, 0)),
            pl.BlockSpec((ATT_KV, 1, V_ROWS, tm), lambda i: (0, i // per_tk, 0, i % per_tk)),
        ],
        out_shape=[
            act_bf, act_f32, act_bf, act_f32, act_bf, act_bf, act_bf,
            jax.ShapeDtypeStruct((ATT_KV, ATT_DH, ATT_GROUP * rows), BF16),
            jax.ShapeDtypeStruct((ATT_KV, rows, ATT_DH), BF16),
            jax.ShapeDtypeStruct((ATT_KV, rows // TK, V_ROWS, TK), BF16),
        ],
        scratch_shapes=[pltpu.VMEM(w32.shape, BF16)],
        compiler_params=pltpu.CompilerParams(
            dimension_semantics=("arbitrary",), vmem_limit_bytes=VMEM_LIMIT),
        name="inproj",
    )(x2d, n1w, w32, lbf, lbb, wq, wk, cos_t, sin_t, mq)


def _hgrn_kernel(qf_ref, gf_ref, kf_ref, vf_ref, qb_ref, gb_ref, kb_ref, vb_ref, tri_ref,
                 of_ref, ob_ref, st_ref):
    @pl.when(pl.program_id(1) == 0)
    def _():
        st_ref[...] = jnp.zeros_like(st_ref)

    t_seg = tri_ref.shape[1]
    n_seg = qf_ref.shape[0] // t_seg
    n_chunks = t_seg // CHUNK
    row = lax.broadcasted_iota(jnp.int32, (CHUNK, CHUNK), 0)
    col = lax.broadcasted_iota(jnp.int32, (CHUNK, CHUNK), 1)
    plans = (
        (qf_ref, gf_ref, kf_ref, vf_ref, of_ref, row >= col, CHUNK // 2 - 1, CHUNK - 1, range(n_chunks)),
        (qb_ref, gb_ref, kb_ref, vb_ref, ob_ref, col >= row, CHUNK // 2, 0, range(n_chunks - 1, -1, -1)),
    )

    def sub(a, c, hh):
        return a[c * CHUNK:(c + 1) * CHUNK, hh * HG_D:(hh + 1) * HG_D]

    def stage(sv):
        out = []
        for d, (q_ref, g_ref, k_ref, v_ref, o_ref, keep, ref_row, end_row, order) in enumerate(plans):
            off = (sv if d == 0 else n_seg - 1 - sv) * t_seg
            rs = slice(off, off + t_seg)
            b = _split_dot_lhs(tri_ref[d], g_ref[rs, :])
            b_r = jnp.concatenate(
                [jnp.broadcast_to(b[c * CHUNK + ref_row:c * CHUNK + ref_row + 1, :], (CHUNK, HG_W))
                 for c in range(n_chunks)], axis=0)
            rel = b - b_r
            out.append((off, b, (q_ref[rs, :].astype(F32) * jnp.exp2(rel)).astype(BF16),
                        (k_ref[rs, :].astype(F32) * jnp.exp2(-rel)).astype(BF16)))
        return out

    def local_stage(seg, ci):
        work, scores, contrib = [], [], []
        for d, (q_ref, g_ref, k_ref, v_ref, o_ref, keep, ref_row, end_row, order) in enumerate(plans):
            off, _, q_seg, k_seg = seg[d]
            c = order[ci]
            for hh in range(HG_HEADS):
                q_in, k_in = sub(q_seg, c, hh), sub(k_seg, c, hh)
                v = v_ref[off + c * CHUNK:off + (c + 1) * CHUNK, hh * HG_D:(hh + 1) * HG_D]
                work.append((d, c, hh, q_in, v))
                scores.append(jnp.where(keep, _dot_nt(q_in, k_in), 0.0).astype(BF16))
                contrib.append(_dot_tn(v, k_in))
        return work, scores, contrib

    def state_stage(seg, work, scores, contrib):
        for (d, c, hh, q_in, v), sc, ct in zip(work, scores, contrib):
            o_ref, ref_row, end_row = plans[d][4], plans[d][6], plans[d][7]
            off, b = seg[d][0], seg[d][1]
            cs = slice(hh * HG_D, (hh + 1) * HG_D)
            b_r = b[c * CHUNK + ref_row:c * CHUNK + ref_row + 1, cs]
            b_end = b[c * CHUNK + end_row:c * CHUNK + end_row + 1, cs]
            st = st_ref[d * HG_HEADS + hh]
            lhs = jnp.concatenate([q_in, sc], axis=1)
            rhs = jnp.concatenate([(st * jnp.exp2(b_r)).astype(BF16), v.T], axis=1)
            o_ref[off + c * CHUNK:off + (c + 1) * CHUNK, cs] = _dot_nt(lhs, rhs)
            st_ref[d * HG_HEADS + hh] = st * jnp.exp2(b_end) + ct * jnp.exp2(b_end - b_r)

    seg = stage(0)
    for sv in range(n_seg):
        ahead = local_stage(seg, 0)
        nxt = stage(sv + 1) if sv + 1 < n_seg else None
        for ci in range(n_chunks):
            current, ahead = ahead, (local_stage(seg, ci + 1) if ci + 1 < n_chunks else None)
            state_stage(seg, *current)
        seg = nxt


def _split_dot_lhs(m, a):
    hi = a.astype(BF16)
    lo = (a - hi.astype(F32)).astype(BF16)
    return _dot(m, hi) + _dot(m, lo)


def _hgrn(hq, gf, kf, gb, kb, iv, tri, batch, seq_len):
    rows = hq.shape[0]
    t = T_SCAN
    nb = seq_len // t
    fwd = lambda b, n: (b * nb + n, 0)
    bwd = lambda b, n: (b * nb + nb - 1 - n, 0)
    sf = pl.BlockSpec((t, HG_W), fwd)
    sb = pl.BlockSpec((t, HG_W), bwd)
    out = jax.ShapeDtypeStruct((rows, HG_W), F32)
    return pl.pallas_call(
        _hgrn_kernel,
        grid=(batch, nb),
        in_specs=[sf, sf, sf, sf, sb, sb, sb, sb, pl.BlockSpec(tri.shape, lambda b, n: (0, 0, 0))],
        out_specs=[sf, sb],
        out_shape=[out, out],
        scratch_shapes=[pltpu.VMEM((2 * HG_HEADS, HG_D, HG_D), F32)],
        compiler_params=pltpu.CompilerParams(
            dimension_semantics=("arbitrary", "arbitrary"), vmem_limit_bytes=VMEM_LIMIT),
        name="hgrn_scan",
    )(hq, gf, kf, iv, hq, gb, kb, iv, tri)


def _attn_kernel(q_ref, qn_ref, k_ref, v_ref, o_ref, m_ref, acc_ref, s_ref, bm_ref):
    tq = o_ref.shape[0]
    tk = v_ref.shape[3]
    n_kb = v_ref.shape[1]
    heads = [(g, hh) for g in range(ATT_KV) for hh in range(ATT_GROUP)]

    def scores(qsrc_ref, j, g, hh):
        u = g * ATT_GROUP + hh
        s = _dot(k_ref[g, j * tk:(j + 1) * tk, :],
                 qsrc_ref[g, :, hh * tq:(hh + 1) * tq])
        s_ref[j % 2, u] = s
        bm_ref[j % 2, u] = jnp.max(s, axis=0, keepdims=True)

    def accumulate(j, g, hh):
        u = g * ATT_GROUP + hh
        m_old = m_ref[u]
        m_new = jnp.maximum(m_old, bm_ref[j % 2, u])
        p = jnp.exp2(s_ref[j % 2, u] - m_new).astype(BF16)
        acc_ref[u] = jnp.exp2(m_old - m_new) * acc_ref[u] + _dot(v_ref[g, j], p)
        m_ref[u] = m_new

    @pl.when(pl.program_id(1) == 0)
    def _():
        for g, hh in heads:
            scores(q_ref, 0, g, hh)

    m_ref[...] = jnp.full(m_ref.shape, -1e30, F32)
    acc_ref[...] = jnp.zeros_like(acc_ref)
    def finalize(pair):
        ot = jnp.concatenate(
            [acc_ref[2 * pair + e, 0:ATT_DH, :] / acc_ref[2 * pair + e, ATT_DH:ATT_DH + 1, :]
             for e in range(2)], axis=0)
        o_ref[:, pair * LANES:(pair + 1) * LANES] = ot.T

    for j in range(n_kb):
        for u, (g, hh) in enumerate(heads):
            if j + 1 < n_kb:
                scores(q_ref, j + 1, g, hh)
            else:
                scores(qn_ref, 0, g, hh)
            accumulate(j, g, hh)
            if j + 1 == n_kb and u % 2 == 1:
                finalize(u // 2)


def _attention(qt, k, vt, batch, seq_len):
    rows = k.shape[1]
    nq = seq_len // TQ
    n_kb = seq_len // TK
    q_blk = (ATT_KV, ATT_DH, ATT_GROUP * TQ)
    return pl.pallas_call(
        _attn_kernel,
        grid=(batch, nq),
        in_specs=[
            pl.BlockSpec(q_blk, lambda b, i: (0, 0, b * nq + i)),
            pl.BlockSpec(q_blk, lambda b, i: (0, 0, b * nq + jnp.minimum(i + 1, nq - 1))),
            pl.BlockSpec((ATT_KV, seq_len, ATT_DH), lambda b, i: (0, b, 0)),
            pl.BlockSpec((ATT_KV, n_kb, V_ROWS, TK), lambda b, i: (0, b, 0, 0)),
        ],
        out_specs=pl.BlockSpec((TQ, ATT_QW), lambda b, i: (b * nq + i, 0)),
        out_shape=jax.ShapeDtypeStruct((rows, ATT_QW), F32),
        scratch_shapes=[
            pltpu.VMEM((ATT_HEADS, 1, TQ), F32),
            pltpu.VMEM((ATT_HEADS, V_ROWS, TQ), F32),
            pltpu.VMEM((2, ATT_HEADS, TK, TQ), F32),
            pltpu.VMEM((2, ATT_HEADS, 1, TQ), F32),
        ],
        compiler_params=pltpu.CompilerParams(
            dimension_semantics=("arbitrary", "arbitrary"), vmem_limit_bytes=VMEM_LIMIT),
        name="attention",
    )(qt, qt, k, vt)


def _rms(y, w):
    ms = jnp.mean(y * y, axis=-1, keepdims=True)
    return y * lax.rsqrt(ms + EPS) * w


def _mix_ffn_kernel(of_ref, ob_ref, sg_ref, att_ref, x_ref, wo32_ref, hgw_ref, attw_ref, n2w_ref,
                    wgu32_ref, wd32_ref, fw_ref, o_ref, wo_ref, wgu_ref, wd_ref, *, d_ff, final_norm, phases):
    i = pl.program_id(0)
    start = 0
    for src_ref, dst_ref, n_pieces in ((wgu32_ref, wgu_ref, phases[0]), (wd32_ref, wd_ref, phases[1]),
                                       (wo32_ref, wo_ref, phases[2])):
        rows = src_ref.shape[0]

        @pl.when((i >= start) & (i < start + n_pieces))
        def _(src_ref=src_ref, dst_ref=dst_ref, rows=rows, start=start):
            r0 = pl.multiple_of((i - start) * rows, 16)
            dst_ref[pl.ds(r0, rows), :] = src_ref[...].astype(BF16)

        start += n_pieces

    @pl.when(i >= start)
    def _():
        o = of_ref[...] + ob_ref[...]
        hgw = hgw_ref[...]
        o_hg = jnp.concatenate(
            [_rms(o[:, hh * HG_D:(hh + 1) * HG_D], hgw) for hh in range(HG_HEADS)], axis=1)
        o_hg = (o_hg * sg_ref[...].astype(F32)).astype(BF16)
        o_att = _rms(att_ref[...], attw_ref[...]).astype(BF16)
        y = x_ref[...] + _dot(o_hg, wo_ref[0:HG_W, :]) + _dot(o_att, wo_ref[HG_W:HG_W + ATT_QW, :])
        h = _rms(y, n2w_ref[...]).astype(BF16)
        ffn = None
        for c in range(d_ff // FF_CHUNK):
            lo = c * FF_CHUNK
            gate = _dot(h, wgu_ref[:, lo:lo + FF_CHUNK])
            up = _dot(h, wgu_ref[:, d_ff + lo:d_ff + lo + FF_CHUNK])
            act = (_silu(gate) * up).astype(BF16)
            part = _dot(act, wd_ref[lo:lo + FF_CHUNK, :])
            ffn = part if ffn is None else ffn + part
        z = y + ffn
        o_ref[...] = _rms(z, fw_ref[...]) if final_norm else z


def _mix_ffn(o_f, o_b, sg, att, x2d, wo32, hgw, attw, n2w, wgu32, wd32, fw, final_norm):
    rows, d = x2d.shape
    d_ff = wd32.shape[0]
    tm = TM_FFN
    assert wgu32.shape[0] % WGU_PIECE == 0 and wd32.shape[0] % WD_PIECE == 0 and wo32.shape[0] % WO_PIECE == 0
    assert WGU_PIECE % 16 == 0 and WD_PIECE % 16 == 0 and WO_PIECE % 16 == 0 and d_ff % FF_CHUNK == 0
    phases = (wgu32.shape[0] // WGU_PIECE, wd32.shape[0] // WD_PIECE, wo32.shape[0] // WO_PIECE)
    n_load = sum(phases)
    tile = lambda i: (jnp.maximum(i - n_load, 0), 0)
    const = lambda i: (0, 0)
    half = pl.BlockSpec((tm, HG_W), tile)
    wide = pl.BlockSpec((tm, d), tile)
    small = lambda a: pl.BlockSpec(a.shape, const)

    def pieces(a, piece_rows, first, count):
        return pl.BlockSpec((piece_rows, a.shape[1]), lambda i: (jnp.clip(i - first, 0, count - 1), 0))

    return pl.pallas_call(
        functools.partial(_mix_ffn_kernel, d_ff=d_ff, final_norm=final_norm, phases=phases),
        grid=(n_load + rows // tm,),
        in_specs=[half, half, half, half, wide,
                  pieces(wo32, WO_PIECE, phases[0] + phases[1], phases[2]),
                  small(hgw), small(attw), small(n2w),
                  pieces(wgu32, WGU_PIECE, 0, phases[0]), pieces(wd32, WD_PIECE, phases[0], phases[1]),
                  small(fw)],
        out_specs=wide,
        out_shape=jax.ShapeDtypeStruct((rows, d), F32),
        scratch_shapes=[pltpu.VMEM(wo32.shape, BF16), pltpu.VMEM(wgu32.shape, BF16),
                        pltpu.VMEM(wd32.shape, BF16)],
        compiler_params=pltpu.CompilerParams(
            dimension_semantics=("arbitrary",), vmem_limit_bytes=VMEM_LIMIT),
        name="mix_ffn",
    )(o_f, o_b, sg, att, x2d, wo32, hgw, attw, n2w, wgu32, wd32, fw)


def _rope_tables(seq_len):
    f32 = np.float32
    rows = seq_len // GRID_W
    row = np.repeat(np.arange(rows), GRID_W).astype(f32)
    col = np.tile(np.arange(GRID_W), rows).astype(f32)
    axis_dim = ATT_DH // 2
    freqs = (f32(ROPE_THETA) ** (-np.arange(0, axis_dim, 2, dtype=f32) / f32(axis_dim))).astype(f32)
    ang = np.concatenate([row[:, None] * freqs, col[:, None] * freqs], axis=-1)
    cos = np.repeat(np.cos(ang), 2, axis=1)
    sin = np.repeat(np.sin(ang), 2, axis=1) * np.tile(np.array([-1.0, 1.0], f32), ATT_DH // 2)
    reps = LANES // ATT_DH
    return jnp.asarray(np.tile(cos, (1, reps)), F32), jnp.asarray(np.tile(sin, (1, reps)), F32)


def kernel(x, norm1_w, w_in, lb_logits, hg_norm_w, q_norm_w, k_norm_w, att_norm_w, w_out, norm2_w,
           w_gate_up, w_down, final_norm_w):
    batch, seq_len, d_model = x.shape
    depth = norm1_w.shape[0]
    rows = batch * seq_len
    assert seq_len % (2 * TK) == 0 and seq_len % TQ == 0
    assert seq_len % T_SCAN == 0 and T_SCAN % T_SEG == 0 and T_SEG % CHUNK == 0 and TK % TM_IN == 0
    assert TM_IN % TQ == 0 and rows % TM_FFN == 0 and w_in.shape[2] == D_IN

    lb_all = jnp.cumsum(jax.nn.softmax(lb_logits.astype(F32), axis=1), axis=1)
    cos_t, sin_t = _rope_tables(seq_len)
    blk = np.arange(ATT_QW) // ATT_DH
    mq = jnp.asarray(np.where(blk[:, None] == blk[None, :], 1.0 / ATT_DH, 0.0), BF16)
    t_idx = np.arange(T_SEG)
    same_chunk = (t_idx[:, None] // CHUNK) == (t_idx[None, :] // CHUNK)
    tri = jnp.asarray(np.stack([same_chunk & (t_idx[:, None] >= t_idx[None, :]),
                                same_chunk & (t_idx[None, :] >= t_idx[:, None])]).astype(np.float32),
                      BF16)
    q_scale = ATT_DH ** -0.5 * math.log2(math.e)

    x2d = x.reshape(rows, d_model)
    row = lambda v: v.astype(F32).reshape(1, -1)
    for l in range(depth):
        hq, gf, kf, gb, kb, iv, sg, qt, k, vt = _inproj(
            x2d, row(norm1_w[l]), w_in[l].astype(F32), row(lb_all[0, l]), row(lb_all[1, l]),
            row(jnp.tile(q_norm_w[l], ATT_HEADS)) * q_scale, row(jnp.tile(k_norm_w[l], ATT_KV)),
            cos_t, sin_t, mq, seq_len)
        o_f, o_b = _hgrn(hq, gf, kf, gb, kb, iv, tri, batch, seq_len)
        att = _attention(qt, k, vt, batch, seq_len)
        x2d = _mix_ffn(o_f, o_b, sg, att, x2d, w_out[l].astype(F32),
                       row(hg_norm_w[l]), row(att_norm_w[l]), row(norm2_w[l]),
                       w_gate_up[l].astype(F32), w_down[l].astype(F32), row(final_norm_w), l == depth - 1)
    return x2d.reshape(batch, seq_len, d_model)
```

```python
import functools
import math

import jax
import jax.numpy as jnp
import numpy as np
from jax import lax
from jax.experimental import pallas as pl
from jax.experimental.pallas import tpu as pltpu

F32 = jnp.float32
BF16 = jnp.bfloat16

EPS = 1e-6
GRID_W = 64
ROPE_THETA = 10000.0

HG_HEADS = 4
HG_D = 128
HG_W = HG_HEADS * HG_D
CHUNK = 64

ATT_HEADS = 8
ATT_KV = 2
ATT_GROUP = ATT_HEADS // ATT_KV
ATT_DH = 64
ATT_QW = ATT_HEADS * ATT_DH
ATT_KVW = ATT_KV * ATT_DH
BF16_ROWS = 16
V_ROWS = ATT_DH + BF16_ROWS

C_Q, C_FF, C_FB, C_I, C_G = 0, HG_W, 2 * HG_W, 3 * HG_W, 4 * HG_W
C_AQ = 5 * HG_W
C_AK = C_AQ + ATT_QW
C_AV = C_AK + ATT_KVW
D_IN = C_AV + ATT_KVW

LANES = 128
VMEM_LIMIT = 56 * 1024 * 1024

TM_IN = 512
TQ = 256
TK = 512
T_SEG = 256
T_SCAN = 1024
TM_FFN = 512
FF_CHUNK = 256
PIECE = 256
EPILOGUES_BEHIND = 2
WGU_PIECE, WD_PIECE, WO_PIECE = 128, 704, 256


def _silu(u):
    hu = 0.5 * u
    return hu + hu * jnp.tanh(hu)


def _dot(a, b):
    return jnp.dot(a, b, preferred_element_type=F32)


def _dot_nt(a, b):
    return lax.dot_general(a, b, (((1,), (1,)), ((), ())), preferred_element_type=F32)


def _dot_tn(a, b):
    return lax.dot_general(a, b, (((0,), (0,)), ((), ())), preferred_element_type=F32)


def _inproj_kernel(x_ref, n1w_ref, w32_ref, lbf_ref, lbb_ref, wq_ref, wk_ref, cos_ref, sin_ref, mq_ref,
                   hq_ref, gf_ref, kf_ref, gb_ref, kb_ref, iv_ref, sg_ref, qt_ref, k_ref, vt_ref, w_ref):
    @pl.when(pl.program_id(0) == 0)
    def _():
        w_ref[...] = w32_ref[...].astype(BF16)

    tm = x_ref.shape[0]
    x = x_ref[...]
    ms = jnp.mean(x * x, axis=-1, keepdims=True)
    h = (x * lax.rsqrt(ms + EPS) * n1w_ref[...]).astype(BF16)

    def proj(lo, hi):
        return _dot(h, w_ref[:, lo:hi])

    def gates(z, lb, g_ref, k_out_ref, cs):
        lb = lb[:, cs]
        hs = 0.5 * (1.0 - lb)
        ht = hs * jnp.tanh(0.5 * z)
        g_ref[:, cs] = jnp.log2((lb + hs) + ht)
        k_out_ref[:, cs] = (hs - ht).astype(BF16)

    cos = cos_ref[...]
    sin = sin_ref[...]
    lane = lax.broadcasted_iota(jnp.int32, (tm, LANES), 1)
    even = (lane & 1) == 0

    def rope(xc):
        partner = jnp.where(even, pltpu.roll(xc, LANES - 1, 1), pltpu.roll(xc, 1, 1))
        return xc * cos + partner * sin

    def head_rms(a, m, w):
        return a * lax.rsqrt(_dot((a * a).astype(BF16), m) + EPS) * w

    def ep_aq(aq, piece, cs):
        yq = head_rms(aq, mq_ref[0:PIECE, 0:PIECE], wq_ref[:, cs])
        for pair in range(PIECE // LANES):
            qct = rope(yq[:, pair * LANES:(pair + 1) * LANES]).T.astype(BF16)
            for e in range(2):
                hl = 2 * pair + e
                for blk in range(tm // TQ):
                    col = (blk * ATT_GROUP + hl) * TQ
                    qt_ref[piece, :, col:col + TQ] = qct[e * ATT_DH:(e + 1) * ATT_DH, blk * TQ:(blk + 1) * TQ]

    def ep_ak(ak, piece, cs):
        yk = rope(head_rms(ak, mq_ref[0:LANES, 0:LANES], wk_ref[...])).astype(BF16)
        k_ref[0] = yk[:, 0:ATT_DH]
        k_ref[1] = yk[:, ATT_DH:2 * ATT_DH]

    def ep_av(av, piece, cs):
        avt = av.T.astype(BF16)
        ones = jnp.ones((V_ROWS - ATT_DH, tm), BF16)
        for g in range(ATT_KV):
            vt_ref[g, 0, 0:ATT_DH, :] = avt[g * ATT_DH:(g + 1) * ATT_DH, :]
            vt_ref[g, 0, ATT_DH:V_ROWS, :] = ones

    def ep_ff(u, piece, cs):
        gates(u, lbf_ref[...], gf_ref, kf_ref, cs)

    def ep_fb(u, piece, cs):
        gates(u, lbb_ref[...], gb_ref, kb_ref, cs)

    def ep_q(u, piece, cs):
        hq_ref[:, cs] = _silu(u).astype(BF16)

    def ep_g(u, piece, cs):
        sg_ref[:, cs] = _silu(u).astype(BF16)

    def ep_i(u, piece, cs):
        iv_ref[:, cs] = u.astype(BF16)

    halves = lambda base, ep: [(base + p * PIECE, PIECE, ep, p) for p in range(HG_W // PIECE)]
    p_ff, p_fb, p_q, p_g, p_i = (halves(base, ep) for base, ep in (
        (C_FF, ep_ff), (C_FB, ep_fb), (C_Q, ep_q), (C_G, ep_g), (C_I, ep_i)))
    stages = (halves(C_AQ, ep_aq) + [(C_AK, ATT_KVW, ep_ak, 0), (C_AV, ATT_KVW, ep_av, 0)]
              + [p_ff[0], p_q[0], p_ff[1], p_q[1], p_fb[0], p_g[0], p_fb[1], p_g[1]] + p_i)
    queue = []
    for lo, width, epilogue, piece in stages:
        u = proj(lo, lo + width)
        queue.append(functools.partial(epilogue, u, piece, slice(piece * PIECE, (piece + 1) * PIECE)))
        if len(queue) > EPILOGUES_BEHIND:
            queue.pop(0)()
    for epilogue in queue:
        epilogue()


def _inproj(x2d, n1w, w32, lbf, lbb, wq, wk, cos_t, sin_t, mq, seq_len):
    rows = x2d.shape[0]
    tm = TM_IN
    n_tiles = rows // tm
    tiles_per_seq = seq_len // tm
    per_tk = TK // tm
    row_blk = lambda i: (i, 0)
    const = lambda i: (0, 0)
    full = lambda shape: pl.BlockSpec(shape, const)
    act_bf = jax.ShapeDtypeStruct((rows, HG_W), BF16)
    act_f32 = jax.ShapeDtypeStruct((rows, HG_W), F32)
    act_spec = pl.BlockSpec((tm, HG_W), row_blk)
    return pl.pallas_call(
        _inproj_kernel,
        grid=(n_tiles,),
        in_specs=[
            pl.BlockSpec((tm, x2d.shape[1]), row_blk),
            full(n1w.shape), pl.BlockSpec(w32.shape, const, pipeline_mode=pl.Buffered(1)),
            full(lbf.shape), full(lbb.shape),
            full(wq.shape), full(wk.shape),
            pl.BlockSpec((tm, LANES), lambda i: (i % tiles_per_seq, 0)),
            pl.BlockSpec((tm, LANES), lambda i: (i % tiles_per_seq, 0)),
            full(mq.shape),
        ],
        out_specs=[
            act_spec, act_spec, act_spec, act_spec, act_spec, act_spec, act_spec,
            pl.BlockSpec((ATT_KV, ATT_DH, ATT_GROUP * tm), lambda i: (0, 0, i)),
            pl.BlockSpec((ATT_KV, tm, ATT_DH), lambda i: (0, i, 0)),
            pl.BlockSpec((ATT_KV, 1, V_ROWS, tm), lambda i: (0, i // per_tk, 0, i % per_tk)),
        ],
        out_shape=[
            act_bf, act_f32, act_bf, act_f32, act_bf, act_bf, act_bf,
            jax.ShapeDtypeStruct((ATT_KV, ATT_DH, ATT_GROUP * rows), BF16),
            jax.ShapeDtypeStruct((ATT_KV, rows, ATT_DH), BF16),
            jax.ShapeDtypeStruct((ATT_KV, rows // TK, V_ROWS, TK), BF16),
        ],
        scratch_shapes=[pltpu.VMEM(w32.shape, BF16)],
        compiler_params=pltpu.CompilerParams(
            dimension_semantics=("arbitrary",), vmem_limit_bytes=VMEM_LIMIT),
        name="inproj",
    )(x2d, n1w, w32, lbf, lbb, wq, wk, cos_t, sin_t, mq)


def _hgrn_kernel(qf_ref, gf_ref, kf_ref, vf_ref, qb_ref, gb_ref, kb_ref, vb_ref, tri_ref,
                 of_ref, ob_ref, st_ref):
    @pl.when(pl.program_id(1) == 0)
    def _():
        st_ref[...] = jnp.zeros_like(st_ref)

    t_seg = tri_ref.shape[1]
    n_seg = qf_ref.shape[0] // t_seg
    n_chunks = t_seg // CHUNK
    row = lax.broadcasted_iota(jnp.int32, (CHUNK, CHUNK), 0)
    col = lax.broadcasted_iota(jnp.int32, (CHUNK, CHUNK), 1)
    plans = (
        (qf_ref, gf_ref, kf_ref, vf_ref, of_ref, row >= col, CHUNK // 2 - 1, CHUNK - 1, range(n_chunks)),
        (qb_ref, gb_ref, kb_ref, vb_ref, ob_ref, col >= row, CHUNK // 2, 0, range(n_chunks - 1, -1, -1)),
    )

    def sub(a, c, hh):
        return a[c * CHUNK:(c + 1) * CHUNK, hh * HG_D:(hh + 1) * HG_D]

    def stage(sv):
        out = []
        for d, (q_ref, g_ref, k_ref, v_ref, o_ref, keep, ref_row, end_row, order) in enumerate(plans):
            off = (sv if d == 0 else n_seg - 1 - sv) * t_seg
            rs = slice(off, off + t_seg)
            b = _split_dot_lhs(tri_ref[d], g_ref[rs, :])
            b_r = jnp.concatenate(
                [jnp.broadcast_to(b[c * CHUNK + ref_row:c * CHUNK + ref_row + 1, :], (CHUNK, HG_W))
                 for c in range(n_chunks)], axis=0)
            rel = b - b_r
            out.append((off, b, (q_ref[rs, :].astype(F32) * jnp.exp2(rel)).astype(BF16),
                        (k_ref[rs, :].astype(F32) * jnp.exp2(-rel)).astype(BF16)))
        return out

    def local_stage(seg, ci):
        work, scores, contrib = [], [], []
        for d, (q_ref, g_ref, k_ref, v_ref, o_ref, keep, ref_row, end_row, order) in enumerate(plans):
            off, _, q_seg, k_seg = seg[d]
            c = order[ci]
            for hh in range(HG_HEADS):
                q_in, k_in = sub(q_seg, c, hh), sub(k_seg, c, hh)
                v = v_ref[off + c * CHUNK:off + (c + 1) * CHUNK, hh * HG_D:(hh + 1) * HG_D]
                work.append((d, c, hh, q_in, v))
                scores.append(jnp.where(keep, _dot_nt(q_in, k_in), 0.0).astype(BF16))
                contrib.append(_dot_tn(v, k_in))
        return work, scores, contrib

    def state_stage(seg, work, scores, contrib):
        for (d, c, hh, q_in, v), sc, ct in zip(work, scores, contrib):
            o_ref, ref_row, end_row = plans[d][4], plans[d][6], plans[d][7]
            off, b = seg[d][0], seg[d][1]
            cs = slice(hh * HG_D, (hh + 1) * HG_D)
            b_r = b[c * CHUNK + ref_row:c * CHUNK + ref_row + 1, cs]
            b_end = b[c * CHUNK + end_row:c * CHUNK + end_row + 1, cs]
            st = st_ref[d * HG_HEADS + hh]
            lhs = jnp.concatenate([q_in, sc], axis=1)
            rhs = jnp.concatenate([(st * jnp.exp2(b_r)).astype(BF16), v.T], axis=1)
            o_ref[off + c * CHUNK:off + (c + 1) * CHUNK, cs] = _dot_nt(lhs, rhs)
            st_ref[d * HG_HEADS + hh] = st * jnp.exp2(b_end) + ct * jnp.exp2(b_end - b_r)

    seg = stage(0)
    for sv in range(n_seg):
        ahead = local_stage(seg, 0)
        nxt = stage(sv + 1) if sv + 1 < n_seg else None
        for ci in range(n_chunks):
            current, ahead = ahead, (local_stage(seg, ci + 1) if ci + 1 < n_chunks else None)
            state_stage(seg, *current)
        seg = nxt


def _split_dot_lhs(m, a):
    hi = a.astype(BF16)
    lo = (a - hi.astype(F32)).astype(BF16)
    return _dot(m, hi) + _dot(m, lo)


def _hgrn(hq, gf, kf, gb, kb, iv, tri, batch, seq_len):
    rows = hq.shape[0]
    t = T_SCAN
    nb = seq_len // t
    fwd = lambda b, n: (b * nb + n, 0)
    bwd = lambda b, n: (b * nb + nb - 1 - n, 0)
    sf = pl.BlockSpec((t, HG_W), fwd)
    sb = pl.BlockSpec((t, HG_W), bwd)
    out = jax.ShapeDtypeStruct((rows, HG_W), F32)
    return pl.pallas_call(
        _hgrn_kernel,
        grid=(batch, nb),
        in_specs=[sf, sf, sf, sf, sb, sb, sb, sb, pl.BlockSpec(tri.shape, lambda b, n: (0, 0, 0))],
        out_specs=[sf, sb],
        out_shape=[out, out],
        scratch_shapes=[pltpu.VMEM((2 * HG_HEADS, HG_D, HG_D), F32)],
        compiler_params=pltpu.CompilerParams(
            dimension_semantics=("arbitrary", "arbitrary"), vmem_limit_bytes=VMEM_LIMIT),
        name="hgrn_scan",
    )(hq, gf, kf, iv, hq, gb, kb, iv, tri)


def _attn_kernel(q_ref, qn_ref, k_ref, v_ref, o_ref, m_ref, acc_ref, s_ref, bm_ref):
    tq = o_ref.shape[0]
    tk = v_ref.shape[3]
    n_kb = v_ref.shape[1]
    heads = [(g, hh) for g in range(ATT_KV) for hh in range(ATT_GROUP)]

    def scores(qsrc_ref, j, g, hh):
        u = g * ATT_GROUP + hh
        s = _dot(k_ref[g, j * tk:(j + 1) * tk, :],
                 qsrc_ref[g, :, hh * tq:(hh + 1) * tq])
        s_ref[j % 2, u] = s
        bm_ref[j % 2, u] = jnp.max(s, axis=0, keepdims=True)

    def accumulate(j, g, hh):
        u = g * ATT_GROUP + hh
        m_old = m_ref[u]
        m_new = jnp.maximum(m_old, bm_ref[j % 2, u])
        p = jnp.exp2(s_ref[j % 2, u] - m_new).astype(BF16)
        acc_ref[u] = jnp.exp2(m_old - m_new) * acc_ref[u] + _dot(v_ref[g, j], p)
        m_ref[u] = m_new

    @pl.when(pl.program_id(1) == 0)
    def _():
        for g, hh in heads:
            scores(q_ref, 0, g, hh)

    m_ref[...] = jnp.full(m_ref.shape, -1e30, F32)
    acc_ref[...] = jnp.zeros_like(acc_ref)
    def finalize(pair):
        ot = jnp.concatenate(
            [acc_ref[2 * pair + e, 0:ATT_DH, :] / acc_ref[2 * pair + e, ATT_DH:ATT_DH + 1, :]
             for e in range(2)], axis=0)
        o_ref[:, pair * LANES:(pair + 1) * LANES] = ot.T

    for j in range(n_kb):
        for u, (g, hh) in enumerate(heads):
            if j + 1 < n_kb:
                scores(q_ref, j + 1, g, hh)
            else:
                scores(qn_ref, 0, g, hh)
            accumulate(j, g, hh)
            if j + 1 == n_kb and u % 2 == 1:
                finalize(u // 2)


def _attention(qt, k, vt, batch, seq_len):
    rows = k.shape[1]
    nq = seq_len // TQ
    n_kb = seq_len // TK
    q_blk = (ATT_KV, ATT_DH, ATT_GROUP * TQ)
    return pl.pallas_call(
        _attn_kernel,
        grid=(batch, nq),
        in_specs=[
            pl.BlockSpec(q_blk, lambda b, i: (0, 0, b * nq + i)),
            pl.BlockSpec(q_blk, lambda b, i: (0, 0, b * nq + jnp.minimum(i + 1, nq - 1))),
            pl.BlockSpec((ATT_KV, seq_len, ATT_DH), lambda b, i: (0, b, 0)),
            pl.BlockSpec((ATT_KV, n_kb, V_ROWS, TK), lambda b, i: (0, b, 0, 0)),
        ],
        out_specs=pl.BlockSpec((TQ, ATT_QW), lambda b, i: (b * nq + i, 0)),
        out_shape=jax.ShapeDtypeStruct((rows, ATT_QW), F32),
        scratch_shapes=[
            pltpu.VMEM((ATT_HEADS, 1, TQ), F32),
            pltpu.VMEM((ATT_HEADS, V_ROWS, TQ), F32),
            pltpu.VMEM((2, ATT_HEADS, TK, TQ), F32),
            pltpu.VMEM((2, ATT_HEADS, 1, TQ), F32),
        ],
        compiler_params=pltpu.CompilerParams(
            dimension_semantics=("arbitrary", "arbitrary"), vmem_limit_bytes=VMEM_LIMIT),
        name="attention",
    )(qt, qt, k, vt)


def _rms(y, w):
    ms = jnp.mean(y * y, axis=-1, keepdims=True)
    return y * lax.rsqrt(ms + EPS) * w


def _mix_ffn_kernel(of_ref, ob_ref, sg_ref, att_ref, x_ref, wo32_ref, hgw_ref, attw_ref, n2w_ref,
                    wgu32_ref, wd32_ref, fw_ref, o_ref, wo_ref, wgu_ref, wd_ref, *, d_ff, final_norm, phases):
    i = pl.program_id(0)
    start = 0
    for src_ref, dst_ref, n_pieces in ((wgu32_ref, wgu_ref, phases[0]), (wd32_ref, wd_ref, phases[1]),
                                       (wo32_ref, wo_ref, phases[2])):
        rows = src_ref.shape[0]

        @pl.when((i >= start) & (i < start + n_pieces))
        def _(src_ref=src_ref, dst_ref=dst_ref, rows=rows, start=start):
            r0 = pl.multiple_of((i - start) * rows, BF16_ROWS)
            dst_ref[pl.ds(r0, rows), :] = src_ref[...].astype(BF16)

        start += n_pieces

    @pl.when(i >= start)
    def _():
        o = of_ref[...] + ob_ref[...]
        hgw = hgw_ref[...]
        o_hg = jnp.concatenate(
            [_rms(o[:, hh * HG_D:(hh + 1) * HG_D], hgw) for hh in range(HG_HEADS)], axis=1)
        o_hg = (o_hg * sg_ref[...].astype(F32)).astype(BF16)
        o_att = _rms(att_ref[...], attw_ref[...]).astype(BF16)
        y = x_ref[...] + _dot(o_hg, wo_ref[0:HG_W, :]) + _dot(o_att, wo_ref[HG_W:HG_W + ATT_QW, :])
        h = _rms(y, n2w_ref[...]).astype(BF16)
        ffn = None
        for c in range(d_ff // FF_CHUNK):
            lo = c * FF_CHUNK
            gate = _dot(h, wgu_ref[:, lo:lo + FF_CHUNK])
            up = _dot(h, wgu_ref[:, d_ff + lo:d_ff + lo + FF_CHUNK])
            act = (_silu(gate) * up).astype(BF16)
            part = _dot(act, wd_ref[lo:lo + FF_CHUNK, :])
            ffn = part if ffn is None else ffn + part
        z = y + ffn
        o_ref[...] = _rms(z, fw_ref[...]) if final_norm else z


def _mix_ffn(o_f, o_b, sg, att, x2d, wo32, hgw, attw, n2w, wgu32, wd32, fw, final_norm):
    rows, d = x2d.shape
    d_ff = wd32.shape[0]
    tm = TM_FFN
    assert wgu32.shape[0] % WGU_PIECE == 0 and wd32.shape[0] % WD_PIECE == 0 and wo32.shape[0] % WO_PIECE == 0
    assert all(p % BF16_ROWS == 0 for p in (WGU_PIECE, WD_PIECE, WO_PIECE)) and d_ff % FF_CHUNK == 0
    phases = (wgu32.shape[0] // WGU_PIECE, wd32.shape[0] // WD_PIECE, wo32.shape[0] // WO_PIECE)
    n_load = sum(phases)
    tile = lambda i: (jnp.maximum(i - n_load, 0), 0)
    const = lambda i: (0, 0)
    half = pl.BlockSpec((tm, HG_W), tile)
    wide = pl.BlockSpec((tm, d), tile)
    small = lambda a: pl.BlockSpec(a.shape, const)

    def pieces(a, piece_rows, first, count):
        return pl.BlockSpec((piece_rows, a.shape[1]), lambda i: (jnp.clip(i - first, 0, count - 1), 0))

    return pl.pallas_call(
        functools.partial(_mix_ffn_kernel, d_ff=d_ff, final_norm=final_norm, phases=phases),
        grid=(n_load + rows // tm,),
        in_specs=[half, half, half, half, wide,
                  pieces(wo32, WO_PIECE, phases[0] + phases[1], phases[2]),
                  small(hgw), small(attw), small(n2w),
                  pieces(wgu32, WGU_PIECE, 0, phases[0]), pieces(wd32, WD_PIECE, phases[0], phases[1]),
                  small(fw)],
        out_specs=wide,
        out_shape=jax.ShapeDtypeStruct((rows, d), F32),
        scratch_shapes=[pltpu.VMEM(wo32.shape, BF16), pltpu.VMEM(wgu32.shape, BF16),
                        pltpu.VMEM(wd32.shape, BF16)],
        compiler_params=pltpu.CompilerParams(
            dimension_semantics=("arbitrary",), vmem_limit_bytes=VMEM_LIMIT),
        name="mix_ffn",
    )(o_f, o_b, sg, att, x2d, wo32, hgw, attw, n2w, wgu32, wd32, fw)


def _rope_tables(seq_len):
    f32 = np.float32
    rows = seq_len // GRID_W
    row = np.repeat(np.arange(rows), GRID_W).astype(f32)
    col = np.tile(np.arange(GRID_W), rows).astype(f32)
    axis_dim = ATT_DH // 2
    freqs = (f32(ROPE_THETA) ** (-np.arange(0, axis_dim, 2, dtype=f32) / f32(axis_dim))).astype(f32)
    ang = np.concatenate([row[:, None] * freqs, col[:, None] * freqs], axis=-1)
    cos = np.repeat(np.cos(ang), 2, axis=1)
    sin = np.repeat(np.sin(ang), 2, axis=1) * np.tile(np.array([-1.0, 1.0], f32), ATT_DH // 2)
    reps = LANES // ATT_DH
    return jnp.asarray(np.tile(cos, (1, reps)), F32), jnp.asarray(np.tile(sin, (1, reps)), F32)


def kernel(x, norm1_w, w_in, lb_logits, hg_norm_w, q_norm_w, k_norm_w, att_norm_w, w_out, norm2_w,
           w_gate_up, w_down, final_norm_w):
    batch, seq_len, d_model = x.shape
    depth = norm1_w.shape[0]
    rows = batch * seq_len
    assert seq_len % (2 * TK) == 0 and seq_len % TQ == 0
    assert seq_len % T_SCAN == 0 and T_SCAN % T_SEG == 0 and T_SEG % CHUNK == 0 and TK % TM_IN == 0
    assert TM_IN % TQ == 0 and rows % TM_FFN == 0 and w_in.shape[2] == D_IN

    lb_all = jnp.cumsum(jax.nn.softmax(lb_logits.astype(F32), axis=1), axis=1)
    cos_t, sin_t = _rope_tables(seq_len)
    blk = np.arange(ATT_QW) // ATT_DH
    mq = jnp.asarray(np.where(blk[:, None] == blk[None, :], 1.0 / ATT_DH, 0.0), BF16)
    t_idx = np.arange(T_SEG)
    same_chunk = (t_idx[:, None] // CHUNK) == (t_idx[None, :] // CHUNK)
    tri = jnp.asarray(np.stack([same_chunk & (t_idx[:, None] >= t_idx[None, :]),
                                same_chunk & (t_idx[None, :] >= t_idx[:, None])]).astype(np.float32),
                      BF16)
    q_scale = ATT_DH ** -0.5 * math.log2(math.e)

    x2d = x.reshape(rows, d_model)
    row = lambda v: v.astype(F32).reshape(1, -1)
    for l in range(depth):
        hq, gf, kf, gb, kb, iv, sg, qt, k, vt = _inproj(
            x2d, row(norm1_w[l]), w_in[l].astype(F32), row(lb_all[0, l]), row(lb_all[1, l]),
            row(jnp.tile(q_norm_w[l], ATT_HEADS)) * q_scale, row(jnp.tile(k_norm_w[l], ATT_KV)),
            cos_t, sin_t, mq, seq_len)
        o_f, o_b = _hgrn(hq, gf, kf, gb, kb, iv, tri, batch, seq_len)
        att = _attention(qt, k, vt, batch, seq_len)
        x2d = _mix_ffn(o_f, o_b, sg, att, x2d, w_out[l].astype(F32),
                       row(hg_norm_w[l]), row(att_norm_w[l]), row(norm2_w[l]),
                       w_gate_up[l].astype(F32), w_down[l].astype(F32), row(final_norm_w), l == depth - 1)
    return x2d.reshape(batch, seq_len, d_model)
```

```python
import functools
import math

import jax
import jax.numpy as jnp
import numpy as np
from jax import lax
from jax.experimental import pallas as pl
from jax.experimental.pallas import tpu as pltpu

F32 = jnp.float32
BF16 = jnp.bfloat16

EPS = 1e-6
GRID_W = 64
ROPE_THETA = 10000.0

HG_HEADS = 4
HG_D = 128
HG_W = HG_HEADS * HG_D
CHUNK = 64

ATT_HEADS = 8
ATT_KV = 2
ATT_GROUP = ATT_HEADS // ATT_KV
ATT_DH = 64
ATT_QW = ATT_HEADS * ATT_DH
ATT_KVW = ATT_KV * ATT_DH
BF16_ROWS = 16
V_ROWS = ATT_DH + BF16_ROWS

C_Q, C_FF, C_FB, C_I, C_G = 0, HG_W, 2 * HG_W, 3 * HG_W, 4 * HG_W
C_AQ = 5 * HG_W
C_AK = C_AQ + ATT_QW
C_AV = C_AK + ATT_KVW
D_IN = C_AV + ATT_KVW

LANES = 128
VMEM_LIMIT = 56 * 1024 * 1024

TM_IN = 512
TQ = 256
TK = 512
T_SEG = 256
T_SCAN = 1024
TM_FFN = 512
FF_CHUNK = 256
PIECE = 256
EPILOGUES_BEHIND = 2
MIXER_AFTER_CHUNK = 1
WGU_PIECE, WD_PIECE, WO_PIECE = 128, 704, 256


def _silu(u):
    hu = 0.5 * u
    return hu + hu * jnp.tanh(hu)


def _dot(a, b):
    return jnp.dot(a, b, preferred_element_type=F32)


def _dot_nt(a, b):
    return lax.dot_general(a, b, (((1,), (1,)), ((), ())), preferred_element_type=F32)


def _dot_tn(a, b):
    return lax.dot_general(a, b, (((0,), (0,)), ((), ())), preferred_element_type=F32)


def _inproj_kernel(x_ref, n1w_ref, w32_ref, lbf_ref, lbb_ref, wq_ref, wk_ref, cos_ref, sin_ref, mq_ref,
                   hq_ref, gf_ref, kf_ref, gb_ref, kb_ref, iv_ref, sg_ref, qt_ref, k_ref, vt_ref, w_ref):
    @pl.when(pl.program_id(0) == 0)
    def _():
        w_ref[...] = w32_ref[...].astype(BF16)

    tm = x_ref.shape[0]
    x = x_ref[...]
    ms = jnp.mean(x * x, axis=-1, keepdims=True)
    h = (x * lax.rsqrt(ms + EPS) * n1w_ref[...]).astype(BF16)

    def proj(lo, hi):
        return _dot(h, w_ref[:, lo:hi])

    def gates(z, lb, g_ref, k_out_ref, cs):
        lb = lb[:, cs]
        hs = 0.5 * (1.0 - lb)
        ht = hs * jnp.tanh(0.5 * z)
        g_ref[:, cs] = jnp.log2((lb + hs) + ht)
        k_out_ref[:, cs] = (hs - ht).astype(BF16)

    cos = cos_ref[...]
    sin = sin_ref[...]
    lane = lax.broadcasted_iota(jnp.int32, (tm, LANES), 1)
    even = (lane & 1) == 0

    def rope(xc):
        partner = jnp.where(even, pltpu.roll(xc, LANES - 1, 1), pltpu.roll(xc, 1, 1))
        return xc * cos + partner * sin

    def head_rms(a, m, w):
        return a * lax.rsqrt(_dot((a * a).astype(BF16), m) + EPS) * w

    def ep_aq(aq, piece, cs):
        yq = head_rms(aq, mq_ref[0:PIECE, 0:PIECE], wq_ref[:, cs])
        for pair in range(PIECE // LANES):
            qct = rope(yq[:, pair * LANES:(pair + 1) * LANES]).T.astype(BF16)
            for e in range(2):
                hl = 2 * pair + e
                for blk in range(tm // TQ):
                    col = (blk * ATT_GROUP + hl) * TQ
                    qt_ref[piece, :, col:col + TQ] = qct[e * ATT_DH:(e + 1) * ATT_DH, blk * TQ:(blk + 1) * TQ]

    def ep_ak(ak, piece, cs):
        yk = rope(head_rms(ak, mq_ref[0:LANES, 0:LANES], wk_ref[...])).astype(BF16)
        k_ref[0] = yk[:, 0:ATT_DH]
        k_ref[1] = yk[:, ATT_DH:2 * ATT_DH]

    def ep_av(av, piece, cs):
        avt = av.T.astype(BF16)
        ones = jnp.ones((V_ROWS - ATT_DH, tm), BF16)
        for g in range(ATT_KV):
            vt_ref[g, 0, 0:ATT_DH, :] = avt[g * ATT_DH:(g + 1) * ATT_DH, :]
            vt_ref[g, 0, ATT_DH:V_ROWS, :] = ones

    def ep_ff(u, piece, cs):
        gates(u, lbf_ref[...], gf_ref, kf_ref, cs)

    def ep_fb(u, piece, cs):
        gates(u, lbb_ref[...], gb_ref, kb_ref, cs)

    def ep_q(u, piece, cs):
        hq_ref[:, cs] = _silu(u).astype(BF16)

    def ep_g(u, piece, cs):
        sg_ref[:, cs] = _silu(u).astype(BF16)

    def ep_i(u, piece, cs):
        iv_ref[:, cs] = u.astype(BF16)

    halves = lambda base, ep: [(base + p * PIECE, PIECE, ep, p) for p in range(HG_W // PIECE)]
    p_ff, p_fb, p_q, p_g, p_i = (halves(base, ep) for base, ep in (
        (C_FF, ep_ff), (C_FB, ep_fb), (C_Q, ep_q), (C_G, ep_g), (C_I, ep_i)))
    stages = (halves(C_AQ, ep_aq) + [(C_AK, ATT_KVW, ep_ak, 0), (C_AV, ATT_KVW, ep_av, 0)]
              + [p_ff[0], p_q[0], p_ff[1], p_q[1], p_fb[0], p_g[0], p_fb[1], p_g[1]] + p_i)
    queue = []
    for lo, width, epilogue, piece in stages:
        u = proj(lo, lo + width)
        queue.append(functools.partial(epilogue, u, piece, slice(piece * PIECE, (piece + 1) * PIECE)))
        if len(queue) > EPILOGUES_BEHIND:
            queue.pop(0)()
    for epilogue in queue:
        epilogue()


def _inproj(x2d, n1w, w32, lbf, lbb, wq, wk, cos_t, sin_t, mq, seq_len):
    rows = x2d.shape[0]
    tm = TM_IN
    n_tiles = rows // tm
    tiles_per_seq = seq_len // tm
    per_tk = TK // tm
    row_blk = lambda i: (i, 0)
    const = lambda i: (0, 0)
    full = lambda shape: pl.BlockSpec(shape, const)
    act_bf = jax.ShapeDtypeStruct((rows, HG_W), BF16)
    act_f32 = jax.ShapeDtypeStruct((rows, HG_W), F32)
    act_spec = pl.BlockSpec((tm, HG_W), row_blk)
    return pl.pallas_call(
        _inproj_kernel,
        grid=(n_tiles,),
        in_specs=[
            pl.BlockSpec((tm, x2d.shape[1]), row_blk),
            full(n1w.shape), pl.BlockSpec(w32.shape, const, pipeline_mode=pl.Buffered(1)),
            full(lbf.shape), full(lbb.shape),
            full(wq.shape), full(wk.shape),
            pl.BlockSpec((tm, LANES), lambda i: (i % tiles_per_seq, 0)),
            pl.BlockSpec((tm, LANES), lambda i: (i % tiles_per_seq, 0)),
            full(mq.shape),
        ],
        out_specs=[
            act_spec, act_spec, act_spec, act_spec, act_spec, act_spec, act_spec,
            pl.BlockSpec((ATT_KV, ATT_DH, ATT_GROUP * tm), lambda i: (0, 0, i)),
            pl.BlockSpec((ATT_KV, tm, ATT_DH), lambda i: (0, i, 0)),
            pl.BlockSpec((ATT_KV, 1, V_ROWS, tm), lambda i: (0, i // per_tk, 0, i % per_tk)),
        ],
        out_shape=[
            act_bf, act_f32, act_bf, act_f32, act_bf, act_bf, act_bf,
            jax.ShapeDtypeStruct((ATT_KV, ATT_DH, ATT_GROUP * rows), BF16),
            jax.ShapeDtypeStruct((ATT_KV, rows, ATT_DH), BF16),
            jax.ShapeDtypeStruct((ATT_KV, rows // TK, V_ROWS, TK), BF16),
        ],
        scratch_shapes=[pltpu.VMEM(w32.shape, BF16)],
        compiler_params=pltpu.CompilerParams(
            dimension_semantics=("arbitrary",), vmem_limit_bytes=VMEM_LIMIT),
        name="inproj",
    )(x2d, n1w, w32, lbf, lbb, wq, wk, cos_t, sin_t, mq)


def _hgrn_kernel(qf_ref, gf_ref, kf_ref, vf_ref, qb_ref, gb_ref, kb_ref, vb_ref, tri_ref,
                 of_ref, ob_ref, st_ref):
    @pl.when(pl.program_id(1) == 0)
    def _():
        st_ref[...] = jnp.zeros_like(st_ref)

    t_seg = tri_ref.shape[1]
    n_seg = qf_ref.shape[0] // t_seg
    n_chunks = t_seg // CHUNK
    row = lax.broadcasted_iota(jnp.int32, (CHUNK, CHUNK), 0)
    col = lax.broadcasted_iota(jnp.int32, (CHUNK, CHUNK), 1)
    plans = (
        (qf_ref, gf_ref, kf_ref, vf_ref, of_ref, row >= col, CHUNK // 2 - 1, CHUNK - 1, range(n_chunks)),
        (qb_ref, gb_ref, kb_ref, vb_ref, ob_ref, col >= row, CHUNK // 2, 0, range(n_chunks - 1, -1, -1)),
    )

    def sub(a, c, hh):
        return a[c * CHUNK:(c + 1) * CHUNK, hh * HG_D:(hh + 1) * HG_D]

    def stage(sv):
        out = []
        for d, (q_ref, g_ref, k_ref, v_ref, o_ref, keep, ref_row, end_row, order) in enumerate(plans):
            off = (sv if d == 0 else n_seg - 1 - sv) * t_seg
            rs = slice(off, off + t_seg)
            b = _split_dot_lhs(tri_ref[d], g_ref[rs, :])
            b_r = jnp.concatenate(
                [jnp.broadcast_to(b[c * CHUNK + ref_row:c * CHUNK + ref_row + 1, :], (CHUNK, HG_W))
                 for c in range(n_chunks)], axis=0)
            rel = b - b_r
            out.append((off, b, (q_ref[rs, :].astype(F32) * jnp.exp2(rel)).astype(BF16),
                        (k_ref[rs, :].astype(F32) * jnp.exp2(-rel)).astype(BF16)))
        return out

    def local_stage(seg, ci):
        work, scores, contrib = [], [], []
        for d, (q_ref, g_ref, k_ref, v_ref, o_ref, keep, ref_row, end_row, order) in enumerate(plans):
            off, _, q_seg, k_seg = seg[d]
            c = order[ci]
            for hh in range(HG_HEADS):
                q_in, k_in = sub(q_seg, c, hh), sub(k_seg, c, hh)
                v = v_ref[off + c * CHUNK:off + (c + 1) * CHUNK, hh * HG_D:(hh + 1) * HG_D]
                work.append((d, c, hh, q_in, v))
                scores.append(jnp.where(keep, _dot_nt(q_in, k_in), 0.0).astype(BF16))
                contrib.append(_dot_tn(v, k_in))
        return work, scores, contrib

    def state_stage(seg, work, scores, contrib):
        for (d, c, hh, q_in, v), sc, ct in zip(work, scores, contrib):
            o_ref, ref_row, end_row = plans[d][4], plans[d][6], plans[d][7]
            off, b = seg[d][0], seg[d][1]
            cs = slice(hh * HG_D, (hh + 1) * HG_D)
            b_r = b[c * CHUNK + ref_row:c * CHUNK + ref_row + 1, cs]
            b_end = b[c * CHUNK + end_row:c * CHUNK + end_row + 1, cs]
            st = st_ref[d * HG_HEADS + hh]
            lhs = jnp.concatenate([q_in, sc], axis=1)
            rhs = jnp.concatenate([(st * jnp.exp2(b_r)).astype(BF16), v.T], axis=1)
            o_ref[off + c * CHUNK:off + (c + 1) * CHUNK, cs] = _dot_nt(lhs, rhs)
            st_ref[d * HG_HEADS + hh] = st * jnp.exp2(b_end) + ct * jnp.exp2(b_end - b_r)

    seg = stage(0)
    for sv in range(n_seg):
        ahead = local_stage(seg, 0)
        nxt = stage(sv + 1) if sv + 1 < n_seg else None
        for ci in range(n_chunks):
            current, ahead = ahead, (local_stage(seg, ci + 1) if ci + 1 < n_chunks else None)
            state_stage(seg, *current)
        seg = nxt


def _split_dot_lhs(m, a):
    hi = a.astype(BF16)
    lo = (a - hi.astype(F32)).astype(BF16)
    return _dot(m, hi) + _dot(m, lo)


def _hgrn(hq, gf, kf, gb, kb, iv, tri, batch, seq_len):
    rows = hq.shape[0]
    t = T_SCAN
    nb = seq_len // t
    fwd = lambda b, n: (b * nb + n, 0)
    bwd = lambda b, n: (b * nb + nb - 1 - n, 0)
    sf = pl.BlockSpec((t, HG_W), fwd)
    sb = pl.BlockSpec((t, HG_W), bwd)
    out = jax.ShapeDtypeStruct((rows, HG_W), F32)
    return pl.pallas_call(
        _hgrn_kernel,
        grid=(batch, nb),
        in_specs=[sf, sf, sf, sf, sb, sb, sb, sb, pl.BlockSpec(tri.shape, lambda b, n: (0, 0, 0))],
        out_specs=[sf, sb],
        out_shape=[out, out],
        scratch_shapes=[pltpu.VMEM((2 * HG_HEADS, HG_D, HG_D), F32)],
        compiler_params=pltpu.CompilerParams(
            dimension_semantics=("arbitrary", "arbitrary"), vmem_limit_bytes=VMEM_LIMIT),
        name="hgrn_scan",
    )(hq, gf, kf, iv, hq, gb, kb, iv, tri)


def _attn_kernel(q_ref, qn_ref, k_ref, v_ref, o_ref, m_ref, acc_ref, s_ref, bm_ref):
    tq = o_ref.shape[0]
    tk = v_ref.shape[3]
    n_kb = v_ref.shape[1]
    heads = [(g, hh) for g in range(ATT_KV) for hh in range(ATT_GROUP)]

    def scores(qsrc_ref, j, g, hh):
        u = g * ATT_GROUP + hh
        s = _dot(k_ref[g, j * tk:(j + 1) * tk, :],
                 qsrc_ref[g, :, hh * tq:(hh + 1) * tq])
        s_ref[j % 2, u] = s
        bm_ref[j % 2, u] = jnp.max(s, axis=0, keepdims=True)

    def accumulate(j, g, hh):
        u = g * ATT_GROUP + hh
        m_old = m_ref[u]
        m_new = jnp.maximum(m_old, bm_ref[j % 2, u])
        p = jnp.exp2(s_ref[j % 2, u] - m_new).astype(BF16)
        acc_ref[u] = jnp.exp2(m_old - m_new) * acc_ref[u] + _dot(v_ref[g, j], p)
        m_ref[u] = m_new

    @pl.when(pl.program_id(1) == 0)
    def _():
        for g, hh in heads:
            scores(q_ref, 0, g, hh)

    m_ref[...] = jnp.full(m_ref.shape, -1e30, F32)
    acc_ref[...] = jnp.zeros_like(acc_ref)
    def finalize(pair):
        ot = jnp.concatenate(
            [acc_ref[2 * pair + e, 0:ATT_DH, :] / acc_ref[2 * pair + e, ATT_DH:ATT_DH + 1, :]
             for e in range(2)], axis=0)
        o_ref[:, pair * LANES:(pair + 1) * LANES] = ot.T

    for j in range(n_kb):
        for u, (g, hh) in enumerate(heads):
            if j + 1 < n_kb:
                scores(q_ref, j + 1, g, hh)
            else:
                scores(qn_ref, 0, g, hh)
            accumulate(j, g, hh)
            if j + 1 == n_kb and u % 2 == 1:
                finalize(u // 2)


def _attention(qt, k, vt, batch, seq_len):
    rows = k.shape[1]
    nq = seq_len // TQ
    n_kb = seq_len // TK
    q_blk = (ATT_KV, ATT_DH, ATT_GROUP * TQ)
    return pl.pallas_call(
        _attn_kernel,
        grid=(batch, nq),
        in_specs=[
            pl.BlockSpec(q_blk, lambda b, i: (0, 0, b * nq + i)),
            pl.BlockSpec(q_blk, lambda b, i: (0, 0, b * nq + jnp.minimum(i + 1, nq - 1))),
            pl.BlockSpec((ATT_KV, seq_len, ATT_DH), lambda b, i: (0, b, 0)),
            pl.BlockSpec((ATT_KV, n_kb, V_ROWS, TK), lambda b, i: (0, b, 0, 0)),
        ],
        out_specs=pl.BlockSpec((TQ, ATT_QW), lambda b, i: (b * nq + i, 0)),
        out_shape=jax.ShapeDtypeStruct((rows, ATT_QW), F32),
        scratch_shapes=[
            pltpu.VMEM((ATT_HEADS, 1, TQ), F32),
            pltpu.VMEM((ATT_HEADS, V_ROWS, TQ), F32),
            pltpu.VMEM((2, ATT_HEADS, TK, TQ), F32),
            pltpu.VMEM((2, ATT_HEADS, 1, TQ), F32),
        ],
        compiler_params=pltpu.CompilerParams(
            dimension_semantics=("arbitrary", "arbitrary"), vmem_limit_bytes=VMEM_LIMIT),
        name="attention",
    )(qt, qt, k, vt)


def _rms(y, w):
    ms = jnp.mean(y * y, axis=-1, keepdims=True)
    return y * lax.rsqrt(ms + EPS) * w


def _mix_ffn_kernel(of_ref, ob_ref, sg_ref, att_ref, x_ref, wo32_ref, hgw_ref, attw_ref, n2w_ref,
                    wgu32_ref, wd32_ref, fw_ref, o_ref, wo_ref, wgu_ref, wd_ref, y_ref, h_ref, *, d_ff, final_norm,
                    phases):
    i = pl.program_id(0)
    start = 0
    for src_ref, dst_ref, n_pieces in ((wgu32_ref, wgu_ref, phases[0]), (wd32_ref, wd_ref, phases[1]),
                                       (wo32_ref, wo_ref, phases[2])):
        rows = src_ref.shape[0]

        @pl.when((i >= start) & (i < start + n_pieces))
        def _(src_ref=src_ref, dst_ref=dst_ref, rows=rows, start=start):
            r0 = pl.multiple_of((i - start) * rows, BF16_ROWS)
            dst_ref[pl.ds(r0, rows), :] = src_ref[...].astype(BF16)

        start += n_pieces

    def mixer():
        o = of_ref[...] + ob_ref[...]
        hgw = hgw_ref[...]
        o_hg = jnp.concatenate(
            [_rms(o[:, hh * HG_D:(hh + 1) * HG_D], hgw) for hh in range(HG_HEADS)], axis=1)
        o_hg = (o_hg * sg_ref[...].astype(F32)).astype(BF16)
        o_att = _rms(att_ref[...], attw_ref[...]).astype(BF16)
        y = x_ref[...] + _dot(o_hg, wo_ref[0:HG_W, :]) + _dot(o_att, wo_ref[HG_W:HG_W + ATT_QW, :])
        return y, _rms(y, n2w_ref[...]).astype(BF16)

    @pl.when(i == start - 1)
    def _():
        y_ref[...], h_ref[...] = mixer()

    @pl.when(i >= start)
    def _():
        h = h_ref[...]
        ffn = None
        nxt = None
        for c in range(d_ff // FF_CHUNK):
            lo = c * FF_CHUNK
            gate = _dot(h, wgu_ref[:, lo:lo + FF_CHUNK])
            up = _dot(h, wgu_ref[:, d_ff + lo:d_ff + lo + FF_CHUNK])
            act = (_silu(gate) * up).astype(BF16)
            part = _dot(act, wd_ref[lo:lo + FF_CHUNK, :])
            ffn = part if ffn is None else ffn + part
            if c == MIXER_AFTER_CHUNK:
                nxt = mixer()
        z = y_ref[...] + ffn
        o_ref[...] = _rms(z, fw_ref[...]) if final_norm else z
        y_ref[...], h_ref[...] = nxt


def _mix_ffn(o_f, o_b, sg, att, x2d, wo32, hgw, attw, n2w, wgu32, wd32, fw, final_norm):
    rows, d = x2d.shape
    d_ff = wd32.shape[0]
    tm = TM_FFN
    assert wgu32.shape[0] % WGU_PIECE == 0 and wd32.shape[0] % WD_PIECE == 0 and wo32.shape[0] % WO_PIECE == 0
    assert all(p % BF16_ROWS == 0 for p in (WGU_PIECE, WD_PIECE, WO_PIECE)) and d_ff % FF_CHUNK == 0
    phases = (wgu32.shape[0] // WGU_PIECE, wd32.shape[0] // WD_PIECE, wo32.shape[0] // WO_PIECE)
    n_load = sum(phases)
    n_tiles = rows // tm
    tile_in = lambda i: (jnp.clip(i + 1 - n_load, 0, n_tiles - 1), 0)
    tile_out = lambda i: (jnp.maximum(i - n_load, 0), 0)
    const = lambda i: (0, 0)
    half = pl.BlockSpec((tm, HG_W), tile_in)
    wide = pl.BlockSpec((tm, d), tile_in)
    small = lambda a: pl.BlockSpec(a.shape, const)

    def pieces(a, piece_rows, first, count):
        return pl.BlockSpec((piece_rows, a.shape[1]), lambda i: (jnp.clip(i - first, 0, count - 1), 0))

    return pl.pallas_call(
        functools.partial(_mix_ffn_kernel, d_ff=d_ff, final_norm=final_norm, phases=phases),
        grid=(n_load + n_tiles,),
        in_specs=[half, half, half, half, wide,
                  pieces(wo32, WO_PIECE, phases[0] + phases[1], phases[2]),
                  small(hgw), small(attw), small(n2w),
                  pieces(wgu32, WGU_PIECE, 0, phases[0]), pieces(wd32, WD_PIECE, phases[0], phases[1]),
                  small(fw)],
        out_specs=pl.BlockSpec((tm, d), tile_out),
        out_shape=jax.ShapeDtypeStruct((rows, d), F32),
        scratch_shapes=[pltpu.VMEM(wo32.shape, BF16), pltpu.VMEM(wgu32.shape, BF16),
                        pltpu.VMEM(wd32.shape, BF16),
                        pltpu.VMEM((tm, d), F32), pltpu.VMEM((tm, d), BF16)],
        compiler_params=pltpu.CompilerParams(
            dimension_semantics=("arbitrary",), vmem_limit_bytes=VMEM_LIMIT),
        name="mix_ffn",
    )(o_f, o_b, sg, att, x2d, wo32, hgw, attw, n2w, wgu32, wd32, fw)


def _rope_tables(seq_len):
    f32 = np.float32
    rows = seq_len // GRID_W
    row = np.repeat(np.arange(rows), GRID_W).astype(f32)
    col = np.tile(np.arange(GRID_W), rows).astype(f32)
    axis_dim = ATT_DH // 2
    freqs = (f32(ROPE_THETA) ** (-np.arange(0, axis_dim, 2, dtype=f32) / f32(axis_dim))).astype(f32)
    ang = np.concatenate([row[:, None] * freqs, col[:, None] * freqs], axis=-1)
    cos = np.repeat(np.cos(ang), 2, axis=1)
    sin = np.repeat(np.sin(ang), 2, axis=1) * np.tile(np.array([-1.0, 1.0], f32), ATT_DH // 2)
    reps = LANES // ATT_DH
    return jnp.asarray(np.tile(cos, (1, reps)), F32), jnp.asarray(np.tile(sin, (1, reps)), F32)


def kernel(x, norm1_w, w_in, lb_logits, hg_norm_w, q_norm_w, k_norm_w, att_norm_w, w_out, norm2_w,
           w_gate_up, w_down, final_norm_w):
    batch, seq_len, d_model = x.shape
    depth = norm1_w.shape[0]
    rows = batch * seq_len
    assert seq_len % (2 * TK) == 0 and seq_len % TQ == 0
    assert seq_len % T_SCAN == 0 and T_SCAN % T_SEG == 0 and T_SEG % CHUNK == 0 and TK % TM_IN == 0
    assert TM_IN % TQ == 0 and rows % TM_FFN == 0 and w_in.shape[2] == D_IN

    lb_all = jnp.cumsum(jax.nn.softmax(lb_logits.astype(F32), axis=1), axis=1)
    cos_t, sin_t = _rope_tables(seq_len)
    blk = np.arange(ATT_QW) // ATT_DH
    mq = jnp.asarray(np.where(blk[:, None] == blk[None, :], 1.0 / ATT_DH, 0.0), BF16)
    t_idx = np.arange(T_SEG)
    same_chunk = (t_idx[:, None] // CHUNK) == (t_idx[None, :] // CHUNK)
    tri = jnp.asarray(np.stack([same_chunk & (t_idx[:, None] >= t_idx[None, :]),
                                same_chunk & (t_idx[None, :] >= t_idx[:, None])]).astype(np.float32),
                      BF16)
    q_scale = ATT_DH ** -0.5 * math.log2(math.e)

    x2d = x.reshape(rows, d_model)
    row = lambda v: v.astype(F32).reshape(1, -1)
    for l in range(depth):
        hq, gf, kf, gb, kb, iv, sg, qt, k, vt = _inproj(
            x2d, row(norm1_w[l]), w_in[l].astype(F32), row(lb_all[0, l]), row(lb_all[1, l]),
            row(jnp.tile(q_norm_w[l], ATT_HEADS)) * q_scale, row(jnp.tile(k_norm_w[l], ATT_KV)),
            cos_t, sin_t, mq, seq_len)
        o_f, o_b = _hgrn(hq, gf, kf, gb, kb, iv, tri, batch, seq_len)
        att = _attention(qt, k, vt, batch, seq_len)
        x2d = _mix_ffn(o_f, o_b, sg, att, x2d, w_out[l].astype(F32),
                       row(hg_norm_w[l]), row(att_norm_w[l]), row(norm2_w[l]),
                       w_gate_up[l].astype(F32), w_down[l].astype(F32), row(final_norm_w), l == depth - 1)
    return x2d.reshape(batch, seq_len, d_model)
```

```python
import functools
import math

import jax
import jax.numpy as jnp
import numpy as np
from jax import lax
from jax.experimental import pallas as pl
from jax.experimental.pallas import tpu as pltpu

F32 = jnp.float32
BF16 = jnp.bfloat16

EPS = 1e-6
GRID_W = 64
ROPE_THETA = 10000.0

HG_HEADS = 4
HG_D = 128
HG_W = HG_HEADS * HG_D
CHUNK = 64

ATT_HEADS = 8
ATT_KV = 2
ATT_GROUP = ATT_HEADS // ATT_KV
ATT_DH = 64
ATT_QW = ATT_HEADS * ATT_DH
ATT_KVW = ATT_KV * ATT_DH
BF16_ROWS = 16
V_ROWS = ATT_DH + BF16_ROWS

C_Q, C_FF, C_FB, C_I, C_G = 0, HG_W, 2 * HG_W, 3 * HG_W, 4 * HG_W
C_AQ = 5 * HG_W
C_AK = C_AQ + ATT_QW
C_AV = C_AK + ATT_KVW
D_IN = C_AV + ATT_KVW

LANES = 128
VMEM_LIMIT = 56 * 1024 * 1024

TM_IN = 512
TQ = 256
TK = 512
T_SEG = 256
T_SCAN = 1024
TM_FFN = 512
FF_CHUNK = 256
PIECE = 256
EPILOGUES_BEHIND = 2
WGU_PIECE, WD_PIECE, WO_PIECE = 128, 704, 256


def _silu(u):
    hu = 0.5 * u
    return hu + hu * jnp.tanh(hu)


def _dot(a, b):
    return jnp.dot(a, b, preferred_element_type=F32)


def _dot_nt(a, b):
    return lax.dot_general(a, b, (((1,), (1,)), ((), ())), preferred_element_type=F32)


def _dot_tn(a, b):
    return lax.dot_general(a, b, (((0,), (0,)), ((), ())), preferred_element_type=F32)


def _inproj_kernel(x_ref, n1w_ref, w32_ref, lbf_ref, lbb_ref, wq_ref, wk_ref, cos_ref, sin_ref, mq_ref,
                   hq_ref, gf_ref, kf_ref, gb_ref, kb_ref, iv_ref, sg_ref, qt_ref, k_ref, vt_ref, w_ref):
    @pl.when(pl.program_id(0) == 0)
    def _():
        w_ref[...] = w32_ref[...].astype(BF16)

    tm = x_ref.shape[0]
    x = x_ref[...]
    ms = jnp.mean(x * x, axis=-1, keepdims=True)
    h = (x * lax.rsqrt(ms + EPS) * n1w_ref[...]).astype(BF16)

    def proj(lo, hi):
        return _dot(h, w_ref[:, lo:hi])

    def gates(z, lb, g_ref, k_out_ref, cs):
        lb = lb[:, cs]
        hs = 0.5 * (1.0 - lb)
        ht = hs * jnp.tanh(0.5 * z)
        g_ref[:, cs] = jnp.log2((lb + hs) + ht)
        k_out_ref[:, cs] = (hs - ht).astype(BF16)

    cos = cos_ref[...]
    sin = sin_ref[...]
    lane = lax.broadcasted_iota(jnp.int32, (tm, LANES), 1)
    even = (lane & 1) == 0

    def rope(xc):
        partner = jnp.where(even, pltpu.roll(xc, LANES - 1, 1), pltpu.roll(xc, 1, 1))
        return xc * cos + partner * sin

    def head_rms(a, m, w):
        return a * lax.rsqrt(_dot((a * a).astype(BF16), m) + EPS) * w

    def ep_aq(aq, piece, cs):
        yq = head_rms(aq, mq_ref[0:PIECE, 0:PIECE], wq_ref[:, cs])
        for pair in range(PIECE // LANES):
            qct = rope(yq[:, pair * LANES:(pair + 1) * LANES]).T.astype(BF16)
            for e in range(2):
                hl = 2 * pair + e
                for blk in range(tm // TQ):
                    col = (blk * ATT_GROUP + hl) * TQ
                    qt_ref[piece, :, col:col + TQ] = qct[e * ATT_DH:(e + 1) * ATT_DH, blk * TQ:(blk + 1) * TQ]

    def ep_ak(ak, piece, cs):
        yk = rope(head_rms(ak, mq_ref[0:LANES, 0:LANES], wk_ref[...])).astype(BF16)
        k_ref[0] = yk[:, 0:ATT_DH]
        k_ref[1] = yk[:, ATT_DH:2 * ATT_DH]

    def ep_av(av, piece, cs):
        avt = av.T.astype(BF16)
        ones = jnp.ones((V_ROWS - ATT_DH, tm), BF16)
        for g in range(ATT_KV):
            vt_ref[g, 0, 0:ATT_DH, :] = avt[g * ATT_DH:(g + 1) * ATT_DH, :]
            vt_ref[g, 0, ATT_DH:V_ROWS, :] = ones

    def ep_ff(u, piece, cs):
        gates(u, lbf_ref[...], gf_ref, kf_ref, cs)

    def ep_fb(u, piece, cs):
        gates(u, lbb_ref[...], gb_ref, kb_ref, cs)

    def ep_q(u, piece, cs):
        hq_ref[:, cs] = _silu(u).astype(BF16)

    def ep_g(u, piece, cs):
        sg_ref[:, cs] = _silu(u).astype(BF16)

    def ep_i(u, piece, cs):
        iv_ref[:, cs] = u.astype(BF16)

    halves = lambda base, ep: [(base + p * PIECE, PIECE, ep, p) for p in range(HG_W // PIECE)]
    p_ff, p_fb, p_q, p_g, p_i = (halves(base, ep) for base, ep in (
        (C_FF, ep_ff), (C_FB, ep_fb), (C_Q, ep_q), (C_G, ep_g), (C_I, ep_i)))
    stages = (halves(C_AQ, ep_aq) + [(C_AK, ATT_KVW, ep_ak, 0), (C_AV, ATT_KVW, ep_av, 0)]
              + [p_ff[0], p_q[0], p_ff[1], p_q[1], p_fb[0], p_g[0], p_fb[1], p_g[1]] + p_i)
    queue = []
    for lo, width, epilogue, piece in stages:
        u = proj(lo, lo + width)
        queue.append(functools.partial(epilogue, u, piece, slice(piece * PIECE, (piece + 1) * PIECE)))
        if len(queue) > EPILOGUES_BEHIND:
            queue.pop(0)()
    for epilogue in queue:
        epilogue()


def _inproj(x2d, n1w, w32, lbf, lbb, wq, wk, cos_t, sin_t, mq, seq_len):
    rows = x2d.shape[0]
    tm = TM_IN
    n_tiles = rows // tm
    tiles_per_seq = seq_len // tm
    per_tk = TK // tm
    row_blk = lambda i: (i, 0)
    const = lambda i: (0, 0)
    full = lambda shape: pl.BlockSpec(shape, const)
    act_bf = jax.ShapeDtypeStruct((rows, HG_W), BF16)
    act_f32 = jax.ShapeDtypeStruct((rows, HG_W), F32)
    act_spec = pl.BlockSpec((tm, HG_W), row_blk)
    return pl.pallas_call(
        _inproj_kernel,
        grid=(n_tiles,),
        in_specs=[
            pl.BlockSpec((tm, x2d.shape[1]), row_blk),
            full(n1w.shape), pl.BlockSpec(w32.shape, const, pipeline_mode=pl.Buffered(1)),
            full(lbf.shape), full(lbb.shape),
            full(wq.shape), full(wk.shape),
            pl.BlockSpec((tm, LANES), lambda i: (i % tiles_per_seq, 0)),
            pl.BlockSpec((tm, LANES), lambda i: (i % tiles_per_seq, 0)),
            full(mq.shape),
        ],
        out_specs=[
            act_spec, act_spec, act_spec, act_spec, act_spec, act_spec, act_spec,
            pl.BlockSpec((ATT_KV, ATT_DH, ATT_GROUP * tm), lambda i: (0, 0, i)),
            pl.BlockSpec((ATT_KV, tm, ATT_DH), lambda i: (0, i, 0)),
            pl.BlockSpec((ATT_KV, 1, V_ROWS, tm), lambda i: (0, i // per_tk, 0, i % per_tk)),
        ],
        out_shape=[
            act_bf, act_f32, act_bf, act_f32, act_bf, act_bf, act_bf,
            jax.ShapeDtypeStruct((ATT_KV, ATT_DH, ATT_GROUP * rows), BF16),
            jax.ShapeDtypeStruct((ATT_KV, rows, ATT_DH), BF16),
            jax.ShapeDtypeStruct((ATT_KV, rows // TK, V_ROWS, TK), BF16),
        ],
        scratch_shapes=[pltpu.VMEM(w32.shape, BF16)],
        compiler_params=pltpu.CompilerParams(
            dimension_semantics=("arbitrary",), vmem_limit_bytes=VMEM_LIMIT),
        name="inproj",
    )(x2d, n1w, w32, lbf, lbb, wq, wk, cos_t, sin_t, mq)


def _hgrn_kernel(qf_ref, gf_ref, kf_ref, vf_ref, qb_ref, gb_ref, kb_ref, vb_ref, tri_ref,
                 of_ref, ob_ref, st_ref):
    @pl.when(pl.program_id(1) == 0)
    def _():
        st_ref[...] = jnp.zeros_like(st_ref)

    t_seg = tri_ref.shape[1]
    n_seg = qf_ref.shape[0] // t_seg
    n_chunks = t_seg // CHUNK
    row = lax.broadcasted_iota(jnp.int32, (CHUNK, CHUNK), 0)
    col = lax.broadcasted_iota(jnp.int32, (CHUNK, CHUNK), 1)
    plans = (
        (qf_ref, gf_ref, kf_ref, vf_ref, of_ref, row >= col, CHUNK // 2 - 1, CHUNK - 1, range(n_chunks)),
        (qb_ref, gb_ref, kb_ref, vb_ref, ob_ref, col >= row, CHUNK // 2, 0, range(n_chunks - 1, -1, -1)),
    )

    def sub(a, c, hh):
        return a[c * CHUNK:(c + 1) * CHUNK, hh * HG_D:(hh + 1) * HG_D]

    def stage(sv):
        out = []
        for d, (q_ref, g_ref, k_ref, v_ref, o_ref, keep, ref_row, end_row, order) in enumerate(plans):
            off = (sv if d == 0 else n_seg - 1 - sv) * t_seg
            rs = slice(off, off + t_seg)
            b = _split_dot_lhs(tri_ref[d], g_ref[rs, :])
            b_r = jnp.concatenate(
                [jnp.broadcast_to(b[c * CHUNK + ref_row:c * CHUNK + ref_row + 1, :], (CHUNK, HG_W))
                 for c in range(n_chunks)], axis=0)
            rel = b - b_r
            out.append((off, b, (q_ref[rs, :].astype(F32) * jnp.exp2(rel)).astype(BF16),
                        (k_ref[rs, :].astype(F32) * jnp.exp2(-rel)).astype(BF16)))
        return out

    def local_stage(seg, ci):
        work, scores, contrib = [], [], []
        for d, (q_ref, g_ref, k_ref, v_ref, o_ref, keep, ref_row, end_row, order) in enumerate(plans):
            off, _, q_seg, k_seg = seg[d]
            c = order[ci]
            for hh in range(HG_HEADS):
                q_in, k_in = sub(q_seg, c, hh), sub(k_seg, c, hh)
                v = v_ref[off + c * CHUNK:off + (c + 1) * CHUNK, hh * HG_D:(hh + 1) * HG_D]
                work.append((d, c, hh, q_in, v))
                scores.append(jnp.where(keep, _dot_nt(q_in, k_in), 0.0).astype(BF16))
                contrib.append(_dot_tn(v, k_in))
        return work, scores, contrib

    def state_stage(seg, work, scores, contrib):
        for (d, c, hh, q_in, v), sc, ct in zip(work, scores, contrib):
            o_ref, ref_row, end_row = plans[d][4], plans[d][6], plans[d][7]
            off, b = seg[d][0], seg[d][1]
            cs = slice(hh * HG_D, (hh + 1) * HG_D)
            b_r = b[c * CHUNK + ref_row:c * CHUNK + ref_row + 1, cs]
            b_end = b[c * CHUNK + end_row:c * CHUNK + end_row + 1, cs]
            st = st_ref[d * HG_HEADS + hh]
            lhs = jnp.concatenate([q_in, sc], axis=1)
            rhs = jnp.concatenate([(st * jnp.exp2(b_r)).astype(BF16), v.T], axis=1)
            o_ref[off + c * CHUNK:off + (c + 1) * CHUNK, cs] = _dot_nt(lhs, rhs)
            st_ref[d * HG_HEADS + hh] = st * jnp.exp2(b_end) + ct * jnp.exp2(b_end - b_r)

    seg = stage(0)
    for sv in range(n_seg):
        ahead = local_stage(seg, 0)
        nxt = stage(sv + 1) if sv + 1 < n_seg else None
        for ci in range(n_chunks):
            current, ahead = ahead, (local_stage(seg, ci + 1) if ci + 1 < n_chunks else None)
            state_stage(seg, *current)
        seg = nxt


def _split_dot_lhs(m, a):
    hi = a.astype(BF16)
    lo = (a - hi.astype(F32)).astype(BF16)
    return _dot(m, hi) + _dot(m, lo)


def _hgrn(hq, gf, kf, gb, kb, iv, tri, batch, seq_len):
    rows = hq.shape[0]
    t = T_SCAN
    nb = seq_len // t
    fwd = lambda b, n: (b * nb + n, 0)
    bwd = lambda b, n: (b * nb + nb - 1 - n, 0)
    sf = pl.BlockSpec((t, HG_W), fwd)
    sb = pl.BlockSpec((t, HG_W), bwd)
    out = jax.ShapeDtypeStruct((rows, HG_W), F32)
    return pl.pallas_call(
        _hgrn_kernel,
        grid=(batch, nb),
        in_specs=[sf, sf, sf, sf, sb, sb, sb, sb, pl.BlockSpec(tri.shape, lambda b, n: (0, 0, 0))],
        out_specs=[sf, sb],
        out_shape=[out, out],
        scratch_shapes=[pltpu.VMEM((2 * HG_HEADS, HG_D, HG_D), F32)],
        compiler_params=pltpu.CompilerParams(
            dimension_semantics=("arbitrary", "arbitrary"), vmem_limit_bytes=VMEM_LIMIT),
        name="hgrn_scan",
    )(hq, gf, kf, iv, hq, gb, kb, iv, tri)


def _attn_kernel(q_ref, qn_ref, k_ref, v_ref, o_ref, m_ref, acc_ref, s_ref, bm_ref):
    tq = o_ref.shape[0]
    tk = v_ref.shape[3]
    n_kb = v_ref.shape[1]
    heads = [(g, hh) for g in range(ATT_KV) for hh in range(ATT_GROUP)]

    def scores(qsrc_ref, j, g, hh):
        u = g * ATT_GROUP + hh
        s = _dot(k_ref[g, j * tk:(j + 1) * tk, :],
                 qsrc_ref[g, :, hh * tq:(hh + 1) * tq])
        s_ref[j % 2, u] = s
        bm_ref[j % 2, u] = jnp.max(s, axis=0, keepdims=True)

    def accumulate(j, g, hh):
        u = g * ATT_GROUP + hh
        m_old = m_ref[u]
        m_new = jnp.maximum(m_old, bm_ref[j % 2, u])
        p = jnp.exp2(s_ref[j % 2, u] - m_new).astype(BF16)
        acc_ref[u] = jnp.exp2(m_old - m_new) * acc_ref[u] + _dot(v_ref[g, j], p)
        m_ref[u] = m_new

    @pl.when(pl.program_id(1) == 0)
    def _():
        for g, hh in heads:
            scores(q_ref, 0, g, hh)

    m_ref[...] = jnp.full(m_ref.shape, -1e30, F32)
    acc_ref[...] = jnp.zeros_like(acc_ref)
    def finalize():
        ots = [jnp.concatenate(
            [acc_ref[2 * pair + e, 0:ATT_DH, :] / acc_ref[2 * pair + e, ATT_DH:ATT_DH + 1, :]
             for e in range(2)], axis=0) for pair in range(ATT_HEADS // 2)]
        ssq = sum(jnp.sum(ot * ot, axis=0, keepdims=True) for ot in ots)
        scale = lax.rsqrt(ssq * (1.0 / ATT_QW) + EPS)
        for pair, ot in enumerate(ots):
            o_ref[:, pair * LANES:(pair + 1) * LANES] = (ot * scale).T.astype(BF16)

    for j in range(n_kb):
        for u, (g, hh) in enumerate(heads):
            if j + 1 < n_kb:
                scores(q_ref, j + 1, g, hh)
            else:
                scores(qn_ref, 0, g, hh)
            accumulate(j, g, hh)
    finalize()


def _attention(qt, k, vt, batch, seq_len):
    rows = k.shape[1]
    nq = seq_len // TQ
    n_kb = seq_len // TK
    q_blk = (ATT_KV, ATT_DH, ATT_GROUP * TQ)
    return pl.pallas_call(
        _attn_kernel,
        grid=(batch, nq),
        in_specs=[
            pl.BlockSpec(q_blk, lambda b, i: (0, 0, b * nq + i)),
            pl.BlockSpec(q_blk, lambda b, i: (0, 0, b * nq + jnp.minimum(i + 1, nq - 1))),
            pl.BlockSpec((ATT_KV, seq_len, ATT_DH), lambda b, i: (0, b, 0)),
            pl.BlockSpec((ATT_KV, n_kb, V_ROWS, TK), lambda b, i: (0, b, 0, 0)),
        ],
        out_specs=pl.BlockSpec((TQ, ATT_QW), lambda b, i: (b * nq + i, 0)),
        out_shape=jax.ShapeDtypeStruct((rows, ATT_QW), BF16),
        scratch_shapes=[
            pltpu.VMEM((ATT_HEADS, 1, TQ), F32),
            pltpu.VMEM((ATT_HEADS, V_ROWS, TQ), F32),
            pltpu.VMEM((2, ATT_HEADS, TK, TQ), F32),
            pltpu.VMEM((2, ATT_HEADS, 1, TQ), F32),
        ],
        compiler_params=pltpu.CompilerParams(
            dimension_semantics=("arbitrary", "arbitrary"), vmem_limit_bytes=VMEM_LIMIT),
        name="attention",
    )(qt, qt, k, vt)


def _rms(y, w):
    ms = jnp.mean(y * y, axis=-1, keepdims=True)
    return y * lax.rsqrt(ms + EPS) * w


def _mix_ffn_kernel(of_ref, ob_ref, sg_ref, att_ref, x_ref, wo32_ref, hgw_ref, attw_ref, n2w_ref,
                    wgu32_ref, wd32_ref, fw_ref, o_ref, wo_ref, wgu_ref, wd_ref, *, d_ff, final_norm, phases):
    i = pl.program_id(0)
    start = 0
    for src_ref, dst_ref, n_pieces in ((wgu32_ref, wgu_ref, phases[0]), (wd32_ref, wd_ref, phases[1]),
                                       (wo32_ref, wo_ref, phases[2])):
        rows = src_ref.shape[0]

        @pl.when((i >= start) & (i < start + n_pieces))
        def _(src_ref=src_ref, dst_ref=dst_ref, rows=rows, start=start):
            r0 = pl.multiple_of((i - start) * rows, BF16_ROWS)
            dst_ref[pl.ds(r0, rows), :] = src_ref[...].astype(BF16)

        start += n_pieces

    @pl.when(i >= start)
    def _():
        o = of_ref[...] + ob_ref[...]
        hgw = hgw_ref[...]
        o_hg = jnp.concatenate(
            [_rms(o[:, hh * HG_D:(hh + 1) * HG_D], hgw) for hh in range(HG_HEADS)], axis=1)
        o_hg = (o_hg * sg_ref[...].astype(F32)).astype(BF16)
        o_att = (att_ref[...].astype(F32) * attw_ref[...]).astype(BF16)
        y = x_ref[...] + _dot(o_hg, wo_ref[0:HG_W, :]) + _dot(o_att, wo_ref[HG_W:HG_W + ATT_QW, :])
        h = _rms(y, n2w_ref[...]).astype(BF16)
        ffn = None
        for c in range(d_ff // FF_CHUNK):
            lo = c * FF_CHUNK
            gate = _dot(h, wgu_ref[:, lo:lo + FF_CHUNK])
            up = _dot(h, wgu_ref[:, d_ff + lo:d_ff + lo + FF_CHUNK])
            act = (_silu(gate) * up).astype(BF16)
            part = _dot(act, wd_ref[lo:lo + FF_CHUNK, :])
            ffn = part if ffn is None else ffn + part
        z = y + ffn
        o_ref[...] = _rms(z, fw_ref[...]) if final_norm else z


def _mix_ffn(o_f, o_b, sg, att, x2d, wo32, hgw, attw, n2w, wgu32, wd32, fw, final_norm):
    rows, d = x2d.shape
    d_ff = wd32.shape[0]
    tm = TM_FFN
    assert wgu32.shape[0] % WGU_PIECE == 0 and wd32.shape[0] % WD_PIECE == 0 and wo32.shape[0] % WO_PIECE == 0
    assert all(p % BF16_ROWS == 0 for p in (WGU_PIECE, WD_PIECE, WO_PIECE)) and d_ff % FF_CHUNK == 0
    phases = (wgu32.shape[0] // WGU_PIECE, wd32.shape[0] // WD_PIECE, wo32.shape[0] // WO_PIECE)
    n_load = sum(phases)
    tile = lambda i: (jnp.maximum(i - n_load, 0), 0)
    const = lambda i: (0, 0)
    half = pl.BlockSpec((tm, HG_W), tile)
    wide = pl.BlockSpec((tm, d), tile)
    small = lambda a: pl.BlockSpec(a.shape, const)

    def pieces(a, piece_rows, first, count):
        return pl.BlockSpec((piece_rows, a.shape[1]), lambda i: (jnp.clip(i - first, 0, count - 1), 0))

    return pl.pallas_call(
        functools.partial(_mix_ffn_kernel, d_ff=d_ff, final_norm=final_norm, phases=phases),
        grid=(n_load + rows // tm,),
        in_specs=[half, half, half, half, wide,
                  pieces(wo32, WO_PIECE, phases[0] + phases[1], phases[2]),
                  small(hgw), small(attw), small(n2w),
                  pieces(wgu32, WGU_PIECE, 0, phases[0]), pieces(wd32, WD_PIECE, phases[0], phases[1]),
                  small(fw)],
        out_specs=wide,
        out_shape=jax.ShapeDtypeStruct((rows, d), F32),
        scratch_shapes=[pltpu.VMEM(wo32.shape, BF16), pltpu.VMEM(wgu32.shape, BF16),
                        pltpu.VMEM(wd32.shape, BF16)],
        compiler_params=pltpu.CompilerParams(
            dimension_semantics=("arbitrary",), vmem_limit_bytes=VMEM_LIMIT),
        name="mix_ffn",
    )(o_f, o_b, sg, att, x2d, wo32, hgw, attw, n2w, wgu32, wd32, fw)


def _rope_tables(seq_len):
    f32 = np.float32
    rows = seq_len // GRID_W
    row = np.repeat(np.arange(rows), GRID_W).astype(f32)
    col = np.tile(np.arange(GRID_W), rows).astype(f32)
    axis_dim = ATT_DH // 2
    freqs = (f32(ROPE_THETA) ** (-np.arange(0, axis_dim, 2, dtype=f32) / f32(axis_dim))).astype(f32)
    ang = np.concatenate([row[:, None] * freqs, col[:, None] * freqs], axis=-1)
    cos = np.repeat(np.cos(ang), 2, axis=1)
    sin = np.repeat(np.sin(ang), 2, axis=1) * np.tile(np.array([-1.0, 1.0], f32), ATT_DH // 2)
    reps = LANES // ATT_DH
    return jnp.asarray(np.tile(cos, (1, reps)), F32), jnp.asarray(np.tile(sin, (1, reps)), F32)


def kernel(x, norm1_w, w_in, lb_logits, hg_norm_w, q_norm_w, k_norm_w, att_norm_w, w_out, norm2_w,
           w_gate_up, w_down, final_norm_w):
    batch, seq_len, d_model = x.shape
    depth = norm1_w.shape[0]
    rows = batch * seq_len
    assert seq_len % (2 * TK) == 0 and seq_len % TQ == 0
    assert seq_len % T_SCAN == 0 and T_SCAN % T_SEG == 0 and T_SEG % CHUNK == 0 and TK % TM_IN == 0
    assert TM_IN % TQ == 0 and rows % TM_FFN == 0 and w_in.shape[2] == D_IN

    lb_all = jnp.cumsum(jax.nn.softmax(lb_logits.astype(F32), axis=1), axis=1)
    cos_t, sin_t = _rope_tables(seq_len)
    blk = np.arange(ATT_QW) // ATT_DH
    mq = jnp.asarray(np.where(blk[:, None] == blk[None, :], 1.0 / ATT_DH, 0.0), BF16)
    t_idx = np.arange(T_SEG)
    same_chunk = (t_idx[:, None] // CHUNK) == (t_idx[None, :] // CHUNK)
    tri = jnp.asarray(np.stack([same_chunk & (t_idx[:, None] >= t_idx[None, :]),
                                same_chunk & (t_idx[None, :] >= t_idx[:, None])]).astype(np.float32),
                      BF16)
    q_scale = ATT_DH ** -0.5 * math.log2(math.e)

    x2d = x.reshape(rows, d_model)
    row = lambda v: v.astype(F32).reshape(1, -1)
    for l in range(depth):
        hq, gf, kf, gb, kb, iv, sg, qt, k, vt = _inproj(
            x2d, row(norm1_w[l]), w_in[l].astype(F32), row(lb_all[0, l]), row(lb_all[1, l]),
            row(jnp.tile(q_norm_w[l], ATT_HEADS)) * q_scale, row(jnp.tile(k_norm_w[l], ATT_KV)),
            cos_t, sin_t, mq, seq_len)
        o_f, o_b = _hgrn(hq, gf, kf, gb, kb, iv, tri, batch, seq_len)
        att = _attention(qt, k, vt, batch, seq_len)
        x2d = _mix_ffn(o_f, o_b, sg, att, x2d, w_out[l].astype(F32),
                       row(hg_norm_w[l]), row(att_norm_w[l]), row(norm2_w[l]),
                       w_gate_up[l].astype(F32), w_down[l].astype(F32), row(final_norm_w), l == depth - 1)
    return x2d.reshape(batch, seq_len, d_model)
```
